```python
import jax, jax.numpy as jnp
from jax import lax
import numpy as np

D_MODEL = 1024
BATCH = 2
SEQ = 8192
DEPTH = 2

N_A_LAYERS = DEPTH // 2
N_B_LAYERS = DEPTH - N_A_LAYERS
POOL_WINDOWS = (2, 4, 8, 16)
N_POOL_GROUPS = len(POOL_WINDOWS)
POOL_GROUP = D_MODEL // N_POOL_GROUPS
MAX_POOL = max(POOL_WINDOWS)
HEAD_DIM = 64
N_Q_HEADS = D_MODEL // HEAD_DIM
N_KV_HEADS = 4
Q_PER_KV = N_Q_HEADS // N_KV_HEADS
WINDOW = 128
BLOCK = 128
ROPE_THETA = 10000.0
N_GROUPS = 4
EXPERTS_PER_GROUP = 8
N_EXPERTS = N_GROUPS * EXPERTS_PER_GROUP
TOP_K_IN_GROUP = 2
D_EXPERT = 512
PLE_DIM = 256
EPS = 1e-6

kernel_name = 'yoco_pool_swa_sink_hier_moe_ple'


def rmsnorm(x, g):
    xf = x.astype(jnp.float32)
    y = xf * lax.rsqrt(jnp.mean(xf * xf, axis=-1, keepdims=True) + EPS)
    return (y * g.astype(jnp.float32)).astype(x.dtype)


def rope_tables(positions):
    inv = ROPE_THETA ** (-jnp.arange(0, HEAD_DIM, 2, dtype=jnp.float32) / HEAD_DIM)
    ang = positions.astype(jnp.float32)[..., None] * inv
    return jnp.cos(ang)[:, :, None, :], jnp.sin(ang)[:, :, None, :]


def apply_rope(x, cos, sin):
    xf = x.astype(jnp.float32)
    x1, x2 = jnp.split(xf, 2, axis=-1)
    out = jnp.concatenate([x1 * cos - x2 * sin, x2 * cos + x1 * sin], axis=-1)
    return out.astype(x.dtype)


def pool_mixer(hn, w_pool, scale):
    B, S, D = hn.shape
    xg = hn.reshape(B, S, N_POOL_GROUPS, POOL_GROUP).astype(jnp.float32)
    c = jnp.cumsum(xg, axis=1)
    c = jnp.pad(c, ((0, 0), (MAX_POOL, 0), (0, 0), (0, 0)))
    t = jnp.arange(S)
    means = []
    for gi, w in enumerate(POOL_WINDOWS):
        win_sum = c[:, MAX_POOL:, gi] - c[:, MAX_POOL - w:MAX_POOL - w + S, gi]
        cnt = jnp.minimum(t + 1, w).astype(jnp.float32)
        means.append(win_sum / cnt[None, :, None])
    d = (jnp.stack(means, axis=2) - xg).astype(hn.dtype)
    y = jnp.einsum('bsgc,gce->bsge', d, w_pool).reshape(B, S, D)
    return y * scale


def shared_kv(h, g_kv, w_k, w_v, cos, sin):
    B, S, _ = h.shape
    hn = rmsnorm(h, g_kv)
    k = apply_rope((hn @ w_k).reshape(B, S, N_KV_HEADS, HEAD_DIM), cos, sin)
    v = (hn @ w_v).reshape(B, S, N_KV_HEADS, HEAD_DIM)
    return k, v


def banded_blocks(t, nb):
    B = t.shape[0]
    tp = jnp.pad(t, ((0, 0), (BLOCK, 0), (0, 0), (0, 0)))
    tb = tp.reshape(B, nb + 1, BLOCK, N_KV_HEADS, HEAD_DIM)
    return jnp.concatenate([tb[:, :-1], tb[:, 1:]], axis=2)


def swa_sink_attention(hn, w_q, w_o, sinks, k, v, cos, sin):
    B, S, D = hn.shape
    nb = S // BLOCK
    q = apply_rope((hn @ w_q).reshape(B, S, N_Q_HEADS, HEAD_DIM), cos, sin)
    q = (q * (HEAD_DIM ** -0.5)).reshape(B, nb, BLOCK, N_KV_HEADS, Q_PER_KV, HEAD_DIM)
    kb = banded_blocks(k, nb)
    vb = banded_blocks(v, nb)
    s = jnp.einsum('bnqkgd,bnjkd->bnkgqj', q, kb).astype(jnp.float32)
    i = jnp.arange(BLOCK)[:, None]
    j = jnp.arange(2 * BLOCK)[None, :]
    rel = i + BLOCK - j
    band = (rel >= 0) & (rel < WINDOW)
    valid = (jnp.arange(nb)[:, None, None] > 0) | (j >= BLOCK)[None]
    mask = (band[None] & valid)[None, :, None, None]
    s = jnp.where(mask, s, -jnp.inf)
    sink = sinks.astype(jnp.float32).reshape(1, 1, N_KV_HEADS, Q_PER_KV, 1, 1)
    m = jnp.maximum(jnp.max(s, axis=-1, keepdims=True), sink)
    e = jnp.exp(s - m)
    pr = (e / (jnp.sum(e, axis=-1, keepdims=True) + jnp.exp(sink - m))).astype(vb.dtype)
    o = jnp.einsum('bnkgqj,bnjkd->bnqkgd', pr, vb).reshape(B, S, N_Q_HEADS * HEAD_DIM)
    return o @ w_o


def hierarchical_moe(hn, w_grp, b_grp, w_exp_r, b_exp_r, w_gate, w_up, w_down):
    B, S, D = hn.shape
    T = B * S
    xt = hn.reshape(T, D)
    gl = (xt @ w_grp).astype(jnp.float32) + b_grp.astype(jnp.float32)
    gp = jax.nn.softmax(gl, axis=-1)
    pg, gsel = lax.top_k(gp, 1)
    el = ((xt @ w_exp_r).astype(jnp.float32) + b_exp_r.astype(jnp.float32))
    el = el.reshape(T, N_GROUPS, EXPERTS_PER_GROUP)
    el_sel = jnp.take_along_axis(el, gsel[:, :, None], axis=1)[:, 0]
    top_logit, eidx = lax.top_k(el_sel, TOP_K_IN_GROUP)
    gates = pg * jax.nn.softmax(top_logit, axis=-1)
    expert_id = gsel * EXPERTS_PER_GROUP + eidx
    flat_e = expert_id.reshape(-1)
    flat_tok = jnp.repeat(jnp.arange(T, dtype=jnp.int32), TOP_K_IN_GROUP)
    flat_g = gates.reshape(-1)
    order = jnp.argsort(flat_e)
    stok = flat_tok[order]
    sg = flat_g[order]
    group_sizes = jnp.bincount(flat_e, length=N_EXPERTS).astype(jnp.int32)
    xs = xt[stok]
    a = jax.nn.silu(lax.ragged_dot(xs, w_gate, group_sizes)) * lax.ragged_dot(xs, w_up, group_sizes)
    y = lax.ragged_dot(a, w_down, group_sizes) * sg[:, None].astype(xs.dtype)
    out = jnp.zeros((T, D), y.dtype).at[stok].add(y)
    return out.reshape(B, S, D).astype(hn.dtype)


def per_layer_embedding(h, p_i, g_ple, w_ple_gate, w_ple_up):
    gate = jax.nn.sigmoid(rmsnorm(h, g_ple) @ w_ple_gate)
    return h + gate * (p_i @ w_ple_up)


def setup_inputs(seed: int = 0) -> dict:
    key = jax.random.key(seed)
    ks = jax.random.split(key, 32)
    f32 = jnp.float32

    def nrm(k, shape, scale):
        return jax.random.normal(k, shape, f32) * scale

    D = D_MODEL
    QD = N_Q_HEADS * HEAD_DIM
    KD = N_KV_HEADS * HEAD_DIM
    positions = (jnp.arange(SEQ, dtype=jnp.int32)[None, :]
                 + jax.random.randint(ks[2], (BATCH, 1), 0, 1024, dtype=jnp.int32))
    return {
        'x': nrm(ks[0], (BATCH, SEQ, D), 1.0),
        'p': nrm(ks[1], (DEPTH, BATCH, SEQ, PLE_DIM), 1.0),
        'positions': positions,
        'g_mix': 1.0 + nrm(ks[3], (DEPTH, D), 0.05),
        'g_ffn': 1.0 + nrm(ks[4], (DEPTH, D), 0.05),
        'pool_w': nrm(ks[5], (N_A_LAYERS, N_POOL_GROUPS, POOL_GROUP, POOL_GROUP), POOL_GROUP ** -0.5),
        'pool_scale': 1.0 + nrm(ks[6], (N_A_LAYERS, D), 0.1),
        'g_kv': 1.0 + nrm(ks[7], (D,), 0.05),
        'w_k': nrm(ks[8], (D, KD), D ** -0.5),
        'w_v': nrm(ks[9], (D, KD), D ** -0.5),
        'w_q': nrm(ks[10], (N_B_LAYERS, D, QD), D ** -0.5),
        'w_o': nrm(ks[11], (N_B_LAYERS, QD, D), QD ** -0.5),
        'sinks': nrm(ks[12], (N_B_LAYERS, N_Q_HEADS), 0.5),
        'w_group_router': nrm(ks[13], (DEPTH, D, N_GROUPS), D ** -0.5),
        'b_group_router': nrm(ks[14], (DEPTH, N_GROUPS), 0.01),
        'w_expert_router': nrm(ks[15], (DEPTH, D, N_EXPERTS), D ** -0.5),
        'b_expert_router': nrm(ks[16], (DEPTH, N_EXPERTS), 0.01),
        'w_exp_gate': nrm(ks[17], (DEPTH, N_EXPERTS, D, D_EXPERT), D ** -0.5),
        'w_exp_up': nrm(ks[18], (DEPTH, N_EXPERTS, D, D_EXPERT), D ** -0.5),
        'w_exp_down': nrm(ks[19], (DEPTH, N_EXPERTS, D_EXPERT, D), D_EXPERT ** -0.5),
        'g_ple': 1.0 + nrm(ks[20], (DEPTH, D), 0.05),
        'w_ple_gate': nrm(ks[21], (DEPTH, D, D), D ** -0.5),
        'w_ple_up': nrm(ks[22], (DEPTH, PLE_DIM, D), 0.5 * PLE_DIM ** -0.5),
        'g_final': 1.0 + nrm(ks[23], (D,), 0.05),
    }


def reference(x, p, positions, g_mix, g_ffn, pool_w, pool_scale, g_kv, w_k, w_v, w_q, w_o, sinks,
              w_group_router, b_group_router, w_expert_router, b_expert_router,
              w_exp_gate, w_exp_up, w_exp_down, g_ple, w_ple_gate, w_ple_up, g_final):
    cos, sin = rope_tables(positions)
    h = x
    k_sh = None
    v_sh = None
    for i in range(DEPTH):
        hn = rmsnorm(h, g_mix[i])
        if i < N_A_LAYERS:
            h = h + pool_mixer(hn, pool_w[i], pool_scale[i])
        else:
            jb = i - N_A_LAYERS
            h = h + swa_sink_attention(hn, w_q[jb], w_o[jb], sinks[jb], k_sh, v_sh, cos, sin)
        h = h + hierarchical_moe(rmsnorm(h, g_ffn[i]), w_group_router[i], b_group_router[i],
                                 w_expert_router[i], b_expert_router[i],
                                 w_exp_gate[i], w_exp_up[i], w_exp_down[i])
        h = per_layer_embedding(h, p[i], g_ple[i], w_ple_gate[i], w_ple_up[i])
        if i == N_A_LAYERS - 1:
            k_sh, v_sh = shared_kv(h, g_kv, w_k, w_v, cos, sin)
    return rmsnorm(h, g_final)
```

```python
import functools

import jax
import jax.numpy as jnp
from jax import lax
from jax.experimental import pallas as pl
from jax.experimental.pallas import tpu as pltpu

D_MODEL = 1024
BATCH = 2
SEQ = 8192
N_TOK = BATCH * SEQ
POOL_WINDOWS = (2, 4, 8, 16)
POOL_GROUP = D_MODEL // len(POOL_WINDOWS)
POOL_HALO = 16
HEAD_DIM = 64
N_Q_HEADS = 16
N_KV_HEADS = 4
Q_PER_KV = 4
KV_DIM = N_KV_HEADS * HEAD_DIM
WINDOW = 128
ATT_BLOCK = 128
ROPE_THETA = 10000.0
N_GROUPS = 4
EXPERTS_PER_GROUP = 8
N_EXPERTS = 32
D_EXPERT = 512
PLE_DIM = 256
EPS = 1e-6

LANES = 128
SUBLANES = 8
CHUNKS = D_MODEL // LANES
N_ASSIGN = 2 * N_TOK

TOK_TILE = 256
DISPATCH_TILE = 512
EXPERT_TILE = 256
N_ROW_TILES = N_ASSIGN // EXPERT_TILE
N_VISITS = N_ROW_TILES + N_EXPERTS - 1
GROUP_LANE0 = N_EXPERTS

F32 = jnp.float32
BF16 = jnp.bfloat16
NEG_INF = float("-inf")


def _rms(x, g):
    return x * lax.rsqrt(jnp.mean(x * x, axis=-1, keepdims=True) + EPS) * g


def _store_slab(ref, val):
    n = val.shape[0]
    for c in range(CHUNKS):
        ref[pl.ds(c, n, stride=CHUNKS), :] = val[:, c * LANES:(c + 1) * LANES]


def _load_slab(ref, n):
    return jnp.concatenate([ref[pl.ds(c, n, stride=CHUNKS), :] for c in range(CHUNKS)], axis=1)


def _route_and_rank(xn, wr_ref, br_ref, run_ref, mi_ref, mf_ref, cnt_ref):
    n = xn.shape[0]
    logits = jnp.dot(xn, wr_ref[...], preferred_element_type=F32, precision=lax.Precision.HIGHEST) + br_ref[...]
    lane = lax.broadcasted_iota(jnp.int32, (n, LANES), 1).astype(F32)
    big = float(LANES)
    is_g = (lane >= GROUP_LANE0) & (lane < GROUP_LANE0 + N_GROUPS)
    gl = jnp.where(is_g, logits, NEG_INF)
    gmax = jnp.max(gl, axis=1, keepdims=True)
    gidx = jnp.min(jnp.where(gl == gmax, lane, big), axis=1, keepdims=True) - GROUP_LANE0
    pg = 1.0 / jnp.sum(jnp.exp(gl - gmax), axis=1, keepdims=True)
    lo = gidx * EXPERTS_PER_GROUP
    el = jnp.where((lane >= lo) & (lane < lo + EXPERTS_PER_GROUP), logits, NEG_INF)
    m1 = jnp.max(el, axis=1, keepdims=True)
    i1 = jnp.min(jnp.where(el == m1, lane, big), axis=1, keepdims=True)
    el2 = jnp.where(lane == i1, NEG_INF, el)
    m2 = jnp.max(el2, axis=1, keepdims=True)
    i2 = jnp.min(jnp.where(el2 == m2, lane, big), axis=1, keepdims=True)
    z = jnp.exp(m2 - m1)
    gate0 = pg / (1.0 + z)
    gate1 = pg * z / (1.0 + z)

    oh0 = jnp.where(lane == i1, 1.0, 0.0)
    oh1 = jnp.where(lane == i2, 1.0, 0.0)
    oh = oh0 + oh1
    r = lax.broadcasted_iota(jnp.int32, (n, n), 0)
    c = lax.broadcasted_iota(jnp.int32, (n, n), 1)
    lower = jnp.where(r > c, 1.0, 0.0).astype(BF16)
    prefix = jnp.dot(lower, oh.astype(BF16), preferred_element_type=F32)
    tot = prefix + run_ref[0:1, :]
    rank0 = jnp.sum(oh0 * tot, axis=1, keepdims=True)
    rank1 = jnp.sum(oh1 * tot, axis=1, keepdims=True)
    new_run = run_ref[0:1, :] + jnp.sum(oh, axis=0, keepdims=True)
    run_ref[0:1, :] = new_run
    cnt_ref[...] = jnp.broadcast_to(new_run, cnt_ref.shape)

    meta = jnp.where(lane == 0.0, i1, jnp.where(lane == 1.0, i2,
                     jnp.where(lane == 2.0, rank0, jnp.where(lane == 3.0, rank1, 0.0))))
    mi_ref[...] = meta.astype(jnp.int32)
    mf_ref[...] = jnp.where(lane == 0.0, gate0, jnp.where(lane == 1.0, gate1, 0.0))


def _mixer0_kernel(x_ref, halo_ref, gmix_ref, poolw_ref, pscale_ref, gffn_ref, wr_ref, br_ref,
                   h_ref, xn_ref, mi_ref, mf_ref, cnt_ref, run_ref):
    i = pl.program_id(0)
    tiles_per_seq = SEQ // TOK_TILE

    @pl.when(i == 0)
    def _():
        run_ref[...] = jnp.zeros_like(run_ref)

    seq_tile = i % tiles_per_seq
    x = x_ref[...]
    g = gmix_ref[...]
    hn = _rms(x, g)
    halo = jnp.where(seq_tile == 0, 0.0, _rms(halo_ref[...], g))
    full = jnp.concatenate([halo, hn], axis=0)
    t = (seq_tile * TOK_TILE + lax.broadcasted_iota(jnp.int32, (TOK_TILE, 1), 0) + 1).astype(F32)
    outs = []
    for gi, w in enumerate(POOL_WINDOWS):
        acc = full[:, gi * POOL_GROUP:(gi + 1) * POOL_GROUP]
        shift = 1
        while shift < w:
            acc = acc + pltpu.roll(acc, shift, axis=0)
            shift *= 2
        win = acc[POOL_HALO:, :]
        mean = win / jnp.minimum(t, float(w))
        dlt = (mean - hn[:, gi * POOL_GROUP:(gi + 1) * POOL_GROUP]).astype(BF16)
        outs.append(jnp.dot(dlt, poolw_ref[gi], preferred_element_type=F32))
    y = jnp.concatenate(outs, axis=1)
    h = x + y * pscale_ref[...]
    h_ref[...] = h
    xn = _rms(h, gffn_ref[...])
    _store_slab(xn_ref, xn)
    _route_and_rank(xn, wr_ref, br_ref, run_ref, mi_ref, mf_ref, cnt_ref)


def _mixer0(x, g_mix, pool_w, pool_scale, g_ffn, w_r, b_r):
    n_tiles = N_TOK // TOK_TILE
    halo_per_tile = TOK_TILE // POOL_HALO
    vec = lambda: pl.BlockSpec((1, D_MODEL), lambda i: (0, 0))
    return pl.pallas_call(
        _mixer0_kernel,
        grid=(n_tiles,),
        in_specs=[
            pl.BlockSpec((TOK_TILE, D_MODEL), lambda i: (i, 0)),
            pl.BlockSpec((POOL_HALO, D_MODEL), lambda i: (jnp.maximum(i * halo_per_tile - 1, 0), 0)),
            vec(),
            pl.BlockSpec((len(POOL_WINDOWS), POOL_GROUP, POOL_GROUP), lambda i: (0, 0, 0)),
            vec(), vec(),
            pl.BlockSpec((D_MODEL, LANES), lambda i: (0, 0)),
            pl.BlockSpec((1, LANES), lambda i: (0, 0)),
        ],
        out_specs=[
            pl.BlockSpec((TOK_TILE, D_MODEL), lambda i: (i, 0)),
            pl.BlockSpec((TOK_TILE * CHUNKS, LANES), lambda i: (i, 0)),
            pl.BlockSpec((TOK_TILE, LANES), lambda i: (i, 0)),
            pl.BlockSpec((TOK_TILE, LANES), lambda i: (i, 0)),
            pl.BlockSpec((SUBLANES, LANES), lambda i: (0, 0)),
        ],
        out_shape=[
            jax.ShapeDtypeStruct((N_TOK, D_MODEL), F32),
            jax.ShapeDtypeStruct((N_TOK * CHUNKS, LANES), F32),
            jax.ShapeDtypeStruct((N_TOK, LANES), jnp.int32),
            jax.ShapeDtypeStruct((N_TOK, LANES), F32),
            jax.ShapeDtypeStruct((SUBLANES, LANES), F32),
        ],
        scratch_shapes=[pltpu.VMEM((SUBLANES, LANES), F32)],
        compiler_params=pltpu.CompilerParams(dimension_semantics=("arbitrary",), vmem_limit_bytes=48 << 20),
        name="mixer0",
    )(x, x, g_mix, pool_w, pool_scale, g_ffn, w_r, b_r)


def _row_copy(src, src_row, dst, dst_row, sem):
    return pltpu.make_async_copy(
        src.at[pl.ds(pl.multiple_of(src_row * CHUNKS, CHUNKS), CHUNKS), :],
        dst.at[pl.ds(pl.multiple_of(dst_row * CHUNKS, CHUNKS), CHUNKS), :],
        sem)


def _dispatch_kernel(pos0_ref, pos1_ref, src_ref, dst_ref, sem):
    base = pl.program_id(0) * DISPATCH_TILE

    def issue(j, carry):
        _row_copy(src_ref, base + j, dst_ref, pos0_ref[j], sem).start()
        _row_copy(src_ref, base + j, dst_ref, pos1_ref[j], sem).start()
        return carry

    lax.fori_loop(0, DISPATCH_TILE, issue, 0)

    def drain(j, carry):
        _row_copy(src_ref, 0, dst_ref, 0, sem).wait()
        _row_copy(src_ref, 0, dst_ref, 0, sem).wait()
        return carry

    lax.fori_loop(0, DISPATCH_TILE, drain, 0)


def _dispatch(pos0, pos1, xn_slab):
    smem = lambda: pl.BlockSpec((DISPATCH_TILE,), lambda i: (i,), memory_space=pltpu.SMEM)
    return pl.pallas_call(
        _dispatch_kernel,
        grid=(N_TOK // DISPATCH_TILE,),
        in_specs=[smem(), smem(), pl.BlockSpec(memory_space=pl.ANY)],
        out_specs=pl.BlockSpec(memory_space=pl.ANY),
        out_shape=jax.ShapeDtypeStruct((N_ASSIGN * CHUNKS, LANES), F32),
        scratch_shapes=[pltpu.SemaphoreType.DMA(())],
        compiler_params=pltpu.CompilerParams(dimension_semantics=("arbitrary",), has_side_effects=True),
        name="dispatch",
    )(pos0, pos1, xn_slab)


def _experts_kernel(vt_ref, ve_ref, gs_ref, ge_ref, nv_ref, xs_ref, wg_ref, wu_ref, wd_ref, ys_ref,
                    wgu_s, wd_s):
    i = pl.program_id(0)
    prev = jnp.maximum(i - 1, 0)
    e = ve_ref[i]
    tile = vt_ref[i]
    valid = i < nv_ref[0]
    new_expert = (i == 0) | (e != ve_ref[prev])
    new_tile = (i == 0) | (tile != vt_ref[prev])

    @pl.when(valid & new_expert)
    def _():
        wgu_s[:, :D_EXPERT] = wg_ref[...].astype(BF16)
        wgu_s[:, D_EXPERT:] = wu_ref[...].astype(BF16)
        wd_s[...] = wd_ref[...].astype(BF16)

    @pl.when(valid)
    def _():
        x = _load_slab(xs_ref, EXPERT_TILE).astype(BF16)
        gu = jnp.dot(x, wgu_s[...], preferred_element_type=F32)
        gate = gu[:, :D_EXPERT]
        act = (gate / (1.0 + jnp.exp(-gate)) * gu[:, D_EXPERT:]).astype(BF16)
        y = jnp.dot(act, wd_s[...], preferred_element_type=F32)
        rows = tile * EXPERT_TILE + lax.broadcasted_iota(jnp.int32, (EXPERT_TILE, 1), 0)
        mine = (rows >= gs_ref[e]) & (rows < ge_ref[e])

        @pl.when(new_tile)
        def _():
            _store_slab(ys_ref, jnp.where(mine, y, 0.0))

        @pl.when(jnp.logical_not(new_tile))
        def _():
            _store_slab(ys_ref, jnp.where(mine, y, _load_slab(ys_ref, EXPERT_TILE)))


def _experts(visit_tile, visit_expert, g_start, g_end, n_visits, xs_slab, w_gate, w_up, w_down):
    rows = EXPERT_TILE * CHUNKS
    grid_spec = pltpu.PrefetchScalarGridSpec(
        num_scalar_prefetch=5,
        grid=(N_VISITS,),
        in_specs=[
            pl.BlockSpec((rows, LANES), lambda i, vt, ve, gs, ge, nv: (vt[i], 0)),
            pl.BlockSpec((None, D_MODEL, D_EXPERT), lambda i, vt, ve, gs, ge, nv: (ve[i], 0, 0)),
            pl.BlockSpec((None, D_MODEL, D_EXPERT), lambda i, vt, ve, gs, ge, nv: (ve[i], 0, 0)),
            pl.BlockSpec((None, D_EXPERT, D_MODEL), lambda i, vt, ve, gs, ge, nv: (ve[i], 0, 0)),
        ],
        out_specs=pl.BlockSpec((rows, LANES), lambda i, vt, ve, gs, ge, nv: (vt[i], 0)),
        scratch_shapes=[pltpu.VMEM((D_MODEL, 2 * D_EXPERT), BF16), pltpu.VMEM((D_EXPERT, D_MODEL), BF16)],
    )
    return pl.pallas_call(
        _experts_kernel,
        grid_spec=grid_spec,
        out_shape=jax.ShapeDtypeStruct((N_ASSIGN * CHUNKS, LANES), F32),
        compiler_params=pltpu.CompilerParams(dimension_semantics=("arbitrary",), vmem_limit_bytes=48 << 20),
        name="experts",
    )(visit_tile, visit_expert, g_start, g_end, n_visits, xs_slab, w_gate, w_up, w_down)


def _gather_two(pos0_ref, pos1_ref, ys_ref, buf0, buf1, sem):
    def issue(j, carry):
        _row_copy(ys_ref, pos0_ref[j], buf0, j, sem).start()
        _row_copy(ys_ref, pos1_ref[j], buf1, j, sem).start()
        return carry

    lax.fori_loop(0, TOK_TILE, issue, 0)

    def drain(j, carry):
        _row_copy(ys_ref, 0, buf0, 0, sem).wait()
        _row_copy(ys_ref, 0, buf1, 0, sem).wait()
        return carry

    lax.fori_loop(0, TOK_TILE, drain, 0)
    return _load_slab(buf0, TOK_TILE), _load_slab(buf1, TOK_TILE)


def _moe_residual_and_ple(h_ref, mf_ref, y0, y1, p_ref, gple_ref, wpg_ref, wpu_ref):
    gates = mf_ref[...]
    h = h_ref[...] + (y0 * gates[:, 0:1] + y1 * gates[:, 1:2])
    hn = _rms(h, gple_ref[...]).astype(BF16)
    gate = jnp.dot(hn, wpg_ref[...], preferred_element_type=F32)
    gate = 1.0 / (1.0 + jnp.exp(-gate))
    up = jnp.dot(p_ref[...].astype(BF16), wpu_ref[...], preferred_element_type=F32)
    return h + gate * up


def _rope(x, cos, sin_signed):
    n, width = x.shape
    reps = width // LANES
    cos_t = jnp.concatenate([cos] * reps, axis=1)
    sin_t = jnp.concatenate([sin_signed] * reps, axis=1)
    lane = lax.broadcasted_iota(jnp.int32, (n, width), 1)
    first_half = (lane % HEAD_DIM) < (HEAD_DIM // 2)
    partner = jnp.where(first_half, pltpu.roll(x, width - HEAD_DIM // 2, axis=1), pltpu.roll(x, HEAD_DIM // 2, axis=1))
    return x * cos_t + partner * sin_t


def _combine0_kernel(pos0_ref, pos1_ref, h_ref, mf_ref, p_ref, cos_ref, sin_ref, gple_ref, wpg_ref, wpu_ref,
                     gkv_ref, wk_ref, wv_ref, gq_ref, wq_ref, ys_ref,
                     ho_ref, k_ref, v_ref, q_ref, buf0, buf1, sem):
    y0, y1 = _gather_two(pos0_ref, pos1_ref, ys_ref, buf0, buf1, sem)
    h = _moe_residual_and_ple(h_ref, mf_ref, y0, y1, p_ref, gple_ref, wpg_ref, wpu_ref)
    ho_ref[...] = h
    cos = cos_ref[...]
    sin = sin_ref[...]
    kvn = _rms(h, gkv_ref[...]).astype(BF16)
    k = jnp.dot(kvn, wk_ref[...], preferred_element_type=F32)
    k_ref[...] = _rope(k, cos, sin).astype(BF16)
    v_ref[...] = jnp.dot(kvn, wv_ref[...], preferred_element_type=F32).astype(BF16)
    qn = _rms(h, gq_ref[...]).astype(BF16)
    q = jnp.dot(qn, wq_ref[...], preferred_element_type=F32)
    q_ref[...] = (_rope(q, cos, sin) * (HEAD_DIM ** -0.5)).astype(BF16)


def _combine1_kernel(pos0_ref, pos1_ref, h_ref, mf_ref, p_ref, gple_ref, wpg_ref, wpu_ref, gfin_ref, ys_ref,
                     out_ref, buf0, buf1, sem):
    y0, y1 = _gather_two(pos0_ref, pos1_ref, ys_ref, buf0, buf1, sem)
    h = _moe_residual_and_ple(h_ref, mf_ref, y0, y1, p_ref, gple_ref, wpg_ref, wpu_ref)
    out_ref[...] = _rms(h, gfin_ref[...])


def _combine_specs():
    smem = lambda: pl.BlockSpec((TOK_TILE,), lambda i: (i,), memory_space=pltpu.SMEM)
    tok = lambda w: pl.BlockSpec((TOK_TILE, w), lambda i: (i, 0))
    full = lambda a, b: pl.BlockSpec((a, b), lambda i: (0, 0))
    scratch = [pltpu.VMEM((TOK_TILE * CHUNKS, LANES), F32), pltpu.VMEM((TOK_TILE * CHUNKS, LANES), F32),
               pltpu.SemaphoreType.DMA(())]
    return smem, tok, full, scratch


def _combine0(pos0, pos1, h, mf, p, cos_t, sin_t, g_ple, w_pg, w_pu, g_kv, w_k, w_v, g_q, w_q, ys_slab):
    smem, tok, full, scratch = _combine_specs()
    return pl.pallas_call(
        _combine0_kernel,
        grid=(N_TOK // TOK_TILE,),
        in_specs=[smem(), smem(), tok(D_MODEL), tok(LANES), tok(PLE_DIM), tok(LANES), tok(LANES),
                  full(1, D_MODEL), full(D_MODEL, D_MODEL), full(PLE_DIM, D_MODEL),
                  full(1, D_MODEL), full(D_MODEL, KV_DIM), full(D_MODEL, KV_DIM),
                  full(1, D_MODEL), full(D_MODEL, D_MODEL),
                  pl.BlockSpec(memory_space=pl.ANY)],
        out_specs=[tok(D_MODEL), tok(KV_DIM), tok(KV_DIM), tok(D_MODEL)],
        out_shape=[jax.ShapeDtypeStruct((N_TOK, D_MODEL), F32),
                   jax.ShapeDtypeStruct((N_TOK, KV_DIM), BF16),
                   jax.ShapeDtypeStruct((N_TOK, KV_DIM), BF16),
                   jax.ShapeDtypeStruct((N_TOK, D_MODEL), BF16)],
        scratch_shapes=scratch,
        compiler_params=pltpu.CompilerParams(dimension_semantics=("arbitrary",), vmem_limit_bytes=56 << 20),
        name="combine0",
    )(pos0, pos1, h, mf, p, cos_t, sin_t, g_ple, w_pg, w_pu, g_kv, w_k, w_v, g_q, w_q, ys_slab)


def _combine1(pos0, pos1, h, mf, p, g_ple, w_pg, w_pu, g_fin, ys_slab):
    smem, tok, full, scratch = _combine_specs()
    return pl.pallas_call(
        _combine1_kernel,
        grid=(N_TOK // TOK_TILE,),
        in_specs=[smem(), smem(), tok(D_MODEL), tok(LANES), tok(PLE_DIM),
                  full(1, D_MODEL), full(D_MODEL, D_MODEL), full(PLE_DIM, D_MODEL), full(1, D_MODEL),
                  pl.BlockSpec(memory_space=pl.ANY)],
        out_specs=tok(D_MODEL),
        out_shape=jax.ShapeDtypeStruct((N_TOK, D_MODEL), F32),
        scratch_shapes=scratch,
        compiler_params=pltpu.CompilerParams(dimension_semantics=("arbitrary",), vmem_limit_bytes=48 << 20),
        name="combine1",
    )(pos0, pos1, h, mf, p, g_ple, w_pg, w_pu, g_fin, ys_slab)


def _attn1_kernel(sink_ref, q_ref, kc_ref, kp_ref, vc_ref, vp_ref, h_ref, wo_ref, gffn_ref, wr_ref, br_ref,
                  ho_ref, xn_ref, mi_ref, mf_ref, cnt_ref, run_ref):
    i = pl.program_id(0)
    blocks_per_seq = SEQ // ATT_BLOCK

    @pl.when(i == 0)
    def _():
        run_ref[...] = jnp.zeros_like(run_ref)

    has_prev = (i % blocks_per_seq) != 0
    kcat = jnp.concatenate([kp_ref[...], kc_ref[...]], axis=0)
    vcat = jnp.concatenate([vp_ref[...], vc_ref[...]], axis=0)
    q = q_ref[...]
    qs = jnp.concatenate([q[:, g * KV_DIM:(g + 1) * KV_DIM] for g in range(Q_PER_KV)], axis=0)
    n_rows = Q_PER_KV * ATT_BLOCK
    row = lax.broadcasted_iota(jnp.int32, (n_rows, 2 * ATT_BLOCK), 0)
    col = lax.broadcasted_iota(jnp.int32, (n_rows, 2 * ATT_BLOCK), 1)
    rel = (row % ATT_BLOCK) + ATT_BLOCK - col
    keep = (rel >= 0) & (rel < WINDOW) & (has_prev | (col >= ATT_BLOCK))
    head_of_lane = lax.broadcasted_iota(jnp.int32, (2 * ATT_BLOCK, KV_DIM), 1) // HEAD_DIM
    grp_of_row = lax.broadcasted_iota(jnp.int32, (n_rows, 1), 0) // ATT_BLOCK
    o = jnp.zeros((n_rows, KV_DIM), F32)
    for kh in range(N_KV_HEADS):
        k_h = jnp.where(head_of_lane == kh, kcat, jnp.zeros_like(kcat))
        v_h = jnp.where(head_of_lane == kh, vcat, jnp.zeros_like(vcat))
        s = lax.dot_general(qs, k_h, (((1,), (1,)), ((), ())), preferred_element_type=F32)
        s = jnp.where(keep, s, NEG_INF)
        sink = jnp.zeros((n_rows, 1), F32)
        for g in range(Q_PER_KV):
            sink = jnp.where(grp_of_row == g, sink_ref[kh * Q_PER_KV + g], sink)
        m = jnp.maximum(jnp.max(s, axis=1, keepdims=True), sink)
        e = jnp.exp(s - m)
        denom = jnp.sum(e, axis=1, keepdims=True) + jnp.exp(sink - m)
        pr = (e / denom).astype(BF16)
        o = o + jnp.dot(pr, v_h, preferred_element_type=F32)
    o_cat = jnp.concatenate([o[g * ATT_BLOCK:(g + 1) * ATT_BLOCK, :] for g in range(Q_PER_KV)], axis=1)
    h = h_ref[...] + jnp.dot(o_cat.astype(BF16), wo_ref[...], preferred_element_type=F32)
    ho_ref[...] = h
    xn = _rms(h, gffn_ref[...])
    _store_slab(xn_ref, xn)
    _route_and_rank(xn, wr_ref, br_ref, run_ref, mi_ref, mf_ref, cnt_ref)


def _attn1(sinks, q, k, v, h, w_o, g_ffn, w_r, b_r):
    blocks_per_seq = SEQ // ATT_BLOCK
    cur = lambda w: pl.BlockSpec((ATT_BLOCK, w), lambda i: (i, 0))
    prev = lambda w: pl.BlockSpec((ATT_BLOCK, w), lambda i: (jnp.maximum(i - 1, 0), 0))
    full = lambda a, b: pl.BlockSpec((a, b), lambda i: (0, 0))
    return pl.pallas_call(
        _attn1_kernel,
        grid=(N_TOK // ATT_BLOCK,),
        in_specs=[pl.BlockSpec(memory_space=pltpu.SMEM),
                  cur(D_MODEL), cur(KV_DIM), prev(KV_DIM), cur(KV_DIM), prev(KV_DIM), cur(D_MODEL),
                  full(D_MODEL, D_MODEL), full(1, D_MODEL), full(D_MODEL, LANES), full(1, LANES)],
        out_specs=[cur(D_MODEL),
                   pl.BlockSpec((ATT_BLOCK * CHUNKS, LANES), lambda i: (i, 0)),
                   cur(LANES), cur(LANES),
                   pl.BlockSpec((SUBLANES, LANES), lambda i: (0, 0))],
        out_shape=[jax.ShapeDtypeStruct((N_TOK, D_MODEL), F32),
                   jax.ShapeDtypeStruct((N_TOK * CHUNKS, LANES), F32),
                   jax.ShapeDtypeStruct((N_TOK, LANES), jnp.int32),
                   jax.ShapeDtypeStruct((N_TOK, LANES), F32),
                   jax.ShapeDtypeStruct((SUBLANES, LANES), F32)],
        scratch_shapes=[pltpu.VMEM((SUBLANES, LANES), F32)],
        compiler_params=pltpu.CompilerParams(dimension_semantics=("arbitrary",), vmem_limit_bytes=48 << 20),
        name="attn1",
    )(sinks, q, k, k, v, v, h, w_o, g_ffn, w_r, b_r)


def _routing_tables(meta_i, counts_f):
    counts = counts_f[0, :N_EXPERTS].astype(jnp.int32)
    g_end = jnp.cumsum(counts)
    g_start = g_end - counts
    pos0 = g_start[meta_i[:, 0]] + meta_i[:, 2]
    pos1 = g_start[meta_i[:, 1]] + meta_i[:, 3]
    first_tile = g_start // EXPERT_TILE
    last_tile = jnp.where(counts > 0, (g_end - 1) // EXPERT_TILE, first_tile - 1)
    n_vis = last_tile - first_tile + 1
    v_end = jnp.cumsum(n_vis)
    v_start = v_end - n_vis
    total = v_end[-1]
    idx = jnp.arange(N_VISITS, dtype=jnp.int32)
    e_of = jnp.minimum(jnp.searchsorted(v_end, idx, side="right"), N_EXPERTS - 1).astype(jnp.int32)
    t_of = first_tile[e_of] + (idx - v_start[e_of])
    last = jnp.maximum(total - 1, 0)
    valid = idx < total
    visit_expert = jnp.where(valid, e_of, e_of[last]).astype(jnp.int32)
    visit_tile = jnp.where(valid, t_of, t_of[last]).astype(jnp.int32)
    return (pos0.astype(jnp.int32), pos1.astype(jnp.int32), visit_tile, visit_expert,
            g_start.astype(jnp.int32), g_end.astype(jnp.int32), total.reshape(1).astype(jnp.int32))


def _router_weights(w_grp, b_grp, w_exp, b_exp):
    w = jnp.zeros((D_MODEL, LANES), F32)
    w = w.at[:, :N_EXPERTS].set(w_exp).at[:, GROUP_LANE0:GROUP_LANE0 + N_GROUPS].set(w_grp)
    b = jnp.zeros((1, LANES), F32)
    b = b.at[0, :N_EXPERTS].set(b_exp).at[0, GROUP_LANE0:GROUP_LANE0 + N_GROUPS].set(b_grp)
    return w, b


def _rope_tables(positions):
    inv = ROPE_THETA ** (-jnp.arange(0, HEAD_DIM, 2, dtype=F32) / HEAD_DIM)
    ang = positions.astype(F32).reshape(N_TOK, 1) * inv
    cos = jnp.cos(ang)
    sin = jnp.sin(ang)
    cos_t = jnp.concatenate([cos, cos, cos, cos], axis=1)
    sin_t = jnp.concatenate([-sin, sin, -sin, sin], axis=1)
    return cos_t, sin_t


def _moe(meta_i, counts_f, xn_slab, w_gate, w_up, w_down):
    pos0, pos1, vt, ve, gs, ge, nv = _routing_tables(meta_i, counts_f)
    xs_slab = _dispatch(pos0, pos1, xn_slab)
    ys_slab = _experts(vt, ve, gs, ge, nv, xs_slab, w_gate, w_up, w_down)
    return pos0, pos1, ys_slab


def kernel(x, p, positions, g_mix, g_ffn, pool_w, pool_scale, g_kv, w_k, w_v, w_q, w_o, sinks, w_group_router, b_group_router, w_expert_router, b_expert_router, w_exp_gate, w_exp_up, w_exp_down, g_ple, w_ple_gate, w_ple_up, g_final):
    row = lambda a: a.reshape(1, -1)
    x2 = x.reshape(N_TOK, D_MODEL)
    p2 = p.reshape(2, N_TOK, PLE_DIM)
    cos_t, sin_t = _rope_tables(positions)
    w_q_perm = w_q[0].reshape(D_MODEL, N_KV_HEADS, Q_PER_KV, HEAD_DIM).transpose(0, 2, 1, 3).reshape(D_MODEL, D_MODEL)
    w_o_perm = w_o[0].reshape(N_KV_HEADS, Q_PER_KV, HEAD_DIM, D_MODEL).transpose(1, 0, 2, 3).reshape(D_MODEL, D_MODEL)

    wr0, br0 = _router_weights(w_group_router[0], b_group_router[0], w_expert_router[0], b_expert_router[0])
    h1, xn_slab, meta_i, meta_f, counts = _mixer0(
        x2, row(g_mix[0]), pool_w[0].astype(BF16), row(pool_scale[0]), row(g_ffn[0]), wr0, br0)
    pos0, pos1, ys_slab = _moe(meta_i, counts, xn_slab, w_exp_gate[0], w_exp_up[0], w_exp_down[0])
    h3, k, v, q = _combine0(
        pos0, pos1, h1, meta_f, p2[0], cos_t, sin_t, row(g_ple[0]), w_ple_gate[0].astype(BF16),
        w_ple_up[0].astype(BF16), row(g_kv), w_k.astype(BF16), w_v.astype(BF16), row(g_mix[1]),
        w_q_perm.astype(BF16), ys_slab)

    wr1, br1 = _router_weights(w_group_router[1], b_group_router[1], w_expert_router[1], b_expert_router[1])
    h4, xn_slab, meta_i, meta_f, counts = _attn1(
        sinks[0], q, k, v, h3, w_o_perm.astype(BF16), row(g_ffn[1]), wr1, br1)
    pos0, pos1, ys_slab = _moe(meta_i, counts, xn_slab, w_exp_gate[1], w_exp_up[1], w_exp_down[1])
    out = _combine1(pos0, pos1, h4, meta_f, p2[1], row(g_ple[1]), w_ple_gate[1].astype(BF16),
                    w_ple_up[1].astype(BF16), row(g_final), ys_slab)
    return out.reshape(BATCH, SEQ, D_MODEL)
```

```python
import functools

import jax
import jax.numpy as jnp
from jax import lax
from jax.experimental import pallas as pl
from jax.experimental.pallas import tpu as pltpu

D_MODEL = 1024
BATCH = 2
SEQ = 8192
N_TOK = BATCH * SEQ
POOL_WINDOWS = (2, 4, 8, 16)
POOL_GROUP = D_MODEL // len(POOL_WINDOWS)
POOL_HALO = 16
HEAD_DIM = 64
N_Q_HEADS = 16
N_KV_HEADS = 4
Q_PER_KV = 4
KV_DIM = N_KV_HEADS * HEAD_DIM
WINDOW = 128
ATT_BLOCK = 128
ROPE_THETA = 10000.0
N_GROUPS = 4
EXPERTS_PER_GROUP = 8
N_EXPERTS = 32
D_EXPERT = 512
PLE_DIM = 256
EPS = 1e-6

LANES = 128
SUBLANES = 8
CHUNKS = D_MODEL // LANES
N_ASSIGN = 2 * N_TOK

TOK_TILE = 256
DISPATCH_TILE = 512
EXPERT_TILE = 256
N_ROW_TILES = N_ASSIGN // EXPERT_TILE
N_VISITS = N_ROW_TILES + N_EXPERTS - 1
GROUP_LANE0 = N_EXPERTS

F32 = jnp.float32
BF16 = jnp.bfloat16
NEG_INF = float("-inf")


def _rms(x, g):
    return x * lax.rsqrt(jnp.mean(x * x, axis=-1, keepdims=True) + EPS) * g


def _store_slab(ref, val):
    n = val.shape[0]
    for c in range(CHUNKS):
        ref[pl.ds(c, n, stride=CHUNKS), :] = val[:, c * LANES:(c + 1) * LANES]


def _load_slab(ref, n):
    return jnp.concatenate([ref[pl.ds(c, n, stride=CHUNKS), :] for c in range(CHUNKS)], axis=1)


def _route_and_rank(xn, wr_ref, br_ref, run_ref, mi_ref, mf_ref, cnt_ref):
    n = xn.shape[0]
    logits = jnp.dot(xn, wr_ref[...], preferred_element_type=F32, precision=lax.Precision.HIGHEST) + br_ref[...]
    lane = lax.broadcasted_iota(jnp.int32, (n, LANES), 1).astype(F32)
    big = float(LANES)
    is_g = (lane >= GROUP_LANE0) & (lane < GROUP_LANE0 + N_GROUPS)
    gl = jnp.where(is_g, logits, NEG_INF)
    gmax = jnp.max(gl, axis=1, keepdims=True)
    gidx = jnp.min(jnp.where(gl == gmax, lane, big), axis=1, keepdims=True) - GROUP_LANE0
    pg = 1.0 / jnp.sum(jnp.exp(gl - gmax), axis=1, keepdims=True)
    lo = gidx * EXPERTS_PER_GROUP
    el = jnp.where((lane >= lo) & (lane < lo + EXPERTS_PER_GROUP), logits, NEG_INF)
    m1 = jnp.max(el, axis=1, keepdims=True)
    i1 = jnp.min(jnp.where(el == m1, lane, big), axis=1, keepdims=True)
    el2 = jnp.where(lane == i1, NEG_INF, el)
    m2 = jnp.max(el2, axis=1, keepdims=True)
    i2 = jnp.min(jnp.where(el2 == m2, lane, big), axis=1, keepdims=True)
    z = jnp.exp(m2 - m1)
    gate0 = pg / (1.0 + z)
    gate1 = pg * z / (1.0 + z)

    oh0 = jnp.where(lane == i1, 1.0, 0.0)
    oh1 = jnp.where(lane == i2, 1.0, 0.0)
    oh = oh0 + oh1
    r = lax.broadcasted_iota(jnp.int32, (n, n), 0)
    c = lax.broadcasted_iota(jnp.int32, (n, n), 1)
    lower = jnp.where(r > c, 1.0, 0.0).astype(BF16)
    prefix = jnp.dot(lower, oh.astype(BF16), preferred_element_type=F32)
    tot = prefix + run_ref[0:1, :]
    rank0 = jnp.sum(oh0 * tot, axis=1, keepdims=True)
    rank1 = jnp.sum(oh1 * tot, axis=1, keepdims=True)
    new_run = run_ref[0:1, :] + jnp.sum(oh, axis=0, keepdims=True)
    run_ref[0:1, :] = new_run
    cnt_ref[...] = jnp.broadcast_to(new_run, cnt_ref.shape)

    meta = jnp.where(lane == 0.0, i1, jnp.where(lane == 1.0, i2,
                     jnp.where(lane == 2.0, rank0, jnp.where(lane == 3.0, rank1, 0.0))))
    mi_ref[...] = meta.T[:SUBLANES, :].astype(jnp.int32)
    mf_ref[...] = jnp.where(lane == 0.0, gate0, jnp.where(lane == 1.0, gate1, 0.0))


def _mixer0_kernel(x_ref, halo_ref, gmix_ref, poolw_ref, pscale_ref, gffn_ref, wr_ref, br_ref,
                   h_ref, xn_ref, mi_ref, mf_ref, cnt_ref, run_ref):
    i = pl.program_id(0)
    tiles_per_seq = SEQ // TOK_TILE

    @pl.when(i == 0)
    def _():
        run_ref[...] = jnp.zeros_like(run_ref)

    seq_tile = i % tiles_per_seq
    x = x_ref[...]
    g = gmix_ref[...]
    hn = _rms(x, g)
    halo = jnp.where(seq_tile == 0, 0.0, _rms(halo_ref[...], g))
    full = jnp.concatenate([halo, hn], axis=0)
    t = (seq_tile * TOK_TILE + lax.broadcasted_iota(jnp.int32, (TOK_TILE, 1), 0) + 1).astype(F32)
    outs = []
    for gi, w in enumerate(POOL_WINDOWS):
        acc = full[:, gi * POOL_GROUP:(gi + 1) * POOL_GROUP]
        shift = 1
        while shift < w:
            acc = acc + pltpu.roll(acc, shift, axis=0)
            shift *= 2
        win = acc[POOL_HALO:, :]
        mean = win / jnp.minimum(t, float(w))
        dlt = (mean - hn[:, gi * POOL_GROUP:(gi + 1) * POOL_GROUP]).astype(BF16)
        outs.append(jnp.dot(dlt, poolw_ref[gi], preferred_element_type=F32))
    y = jnp.concatenate(outs, axis=1)
    h = x + y * pscale_ref[...]
    h_ref[...] = h
    xn = _rms(h, gffn_ref[...])
    _store_slab(xn_ref, xn)
    _route_and_rank(xn, wr_ref, br_ref, run_ref, mi_ref, mf_ref, cnt_ref)


def _mixer0(x, g_mix, pool_w, pool_scale, g_ffn, w_r, b_r):
    n_tiles = N_TOK // TOK_TILE
    halo_per_tile = TOK_TILE // POOL_HALO
    vec = lambda: pl.BlockSpec((1, D_MODEL), lambda i: (0, 0))
    return pl.pallas_call(
        _mixer0_kernel,
        grid=(n_tiles,),
        in_specs=[
            pl.BlockSpec((TOK_TILE, D_MODEL), lambda i: (i, 0)),
            pl.BlockSpec((POOL_HALO, D_MODEL), lambda i: (jnp.maximum(i * halo_per_tile - 1, 0), 0)),
            vec(),
            pl.BlockSpec((len(POOL_WINDOWS), POOL_GROUP, POOL_GROUP), lambda i: (0, 0, 0)),
            vec(), vec(),
            pl.BlockSpec((D_MODEL, LANES), lambda i: (0, 0)),
            pl.BlockSpec((1, LANES), lambda i: (0, 0)),
        ],
        out_specs=[
            pl.BlockSpec((TOK_TILE, D_MODEL), lambda i: (i, 0)),
            pl.BlockSpec((TOK_TILE * CHUNKS, LANES), lambda i: (i, 0)),
            pl.BlockSpec((SUBLANES, TOK_TILE), lambda i: (0, i)),
            pl.BlockSpec((TOK_TILE, LANES), lambda i: (i, 0)),
            pl.BlockSpec((SUBLANES, LANES), lambda i: (0, 0)),
        ],
        out_shape=[
            jax.ShapeDtypeStruct((N_TOK, D_MODEL), F32),
            jax.ShapeDtypeStruct((N_TOK * CHUNKS, LANES), F32),
            jax.ShapeDtypeStruct((SUBLANES, N_TOK), jnp.int32),
            jax.ShapeDtypeStruct((N_TOK, LANES), F32),
            jax.ShapeDtypeStruct((SUBLANES, LANES), F32),
        ],
        scratch_shapes=[pltpu.VMEM((SUBLANES, LANES), F32)],
        compiler_params=pltpu.CompilerParams(dimension_semantics=("arbitrary",), vmem_limit_bytes=48 << 20),
        name="mixer0",
    )(x, x, g_mix, pool_w, pool_scale, g_ffn, w_r, b_r)


def _row_copy(src, src_row, dst, dst_row, sem):
    return pltpu.make_async_copy(
        src.at[pl.ds(pl.multiple_of(src_row * CHUNKS, CHUNKS), CHUNKS), :],
        dst.at[pl.ds(pl.multiple_of(dst_row * CHUNKS, CHUNKS), CHUNKS), :],
        sem)


def _dispatch_kernel(pos0_ref, pos1_ref, src_ref, dst_ref, sem):
    unroll = 8

    def issue(jo, carry):
        for u in range(unroll):
            j = jo * unroll + u
            _row_copy(src_ref, j, dst_ref, pos0_ref[j], sem).start()
            _row_copy(src_ref, j, dst_ref, pos1_ref[j], sem).start()
        return carry

    lax.fori_loop(0, DISPATCH_TILE // unroll, issue, 0)

    def drain(jo, carry):
        for u in range(2 * unroll):
            _row_copy(src_ref, 0, dst_ref, 0, sem).wait()
        return carry

    lax.fori_loop(0, DISPATCH_TILE // unroll, drain, 0)


def _dispatch(pos0, pos1, xn_slab):
    smem = lambda: pl.BlockSpec((DISPATCH_TILE,), lambda i: (i,), memory_space=pltpu.SMEM)
    return pl.pallas_call(
        _dispatch_kernel,
        grid=(N_TOK // DISPATCH_TILE,),
        in_specs=[smem(), smem(), pl.BlockSpec((DISPATCH_TILE * CHUNKS, LANES), lambda i: (i, 0))],
        out_specs=pl.BlockSpec(memory_space=pl.ANY),
        out_shape=jax.ShapeDtypeStruct((N_ASSIGN * CHUNKS, LANES), F32),
        scratch_shapes=[pltpu.SemaphoreType.DMA(())],
        compiler_params=pltpu.CompilerParams(dimension_semantics=("arbitrary",), has_side_effects=True),
        name="dispatch",
    )(pos0, pos1, xn_slab)


def _experts_kernel(vt_ref, ve_ref, gs_ref, ge_ref, nv_ref, xs_ref, wg_ref, wu_ref, wd_ref, ys_ref,
                    wgu_s, wd_s):
    i = pl.program_id(0)
    prev = jnp.maximum(i - 1, 0)
    e = ve_ref[i]
    tile = vt_ref[i]
    valid = i < nv_ref[0]
    new_expert = (i == 0) | (e != ve_ref[prev])
    new_tile = (i == 0) | (tile != vt_ref[prev])

    @pl.when(valid & new_expert)
    def _():
        wgu_s[:, :D_EXPERT] = wg_ref[...].astype(BF16)
        wgu_s[:, D_EXPERT:] = wu_ref[...].astype(BF16)
        wd_s[...] = wd_ref[...].astype(BF16)

    @pl.when(valid)
    def _():
        x = _load_slab(xs_ref, EXPERT_TILE).astype(BF16)
        gu = jnp.dot(x, wgu_s[...], preferred_element_type=F32)
        gate = gu[:, :D_EXPERT]
        act = (gate / (1.0 + jnp.exp(-gate)) * gu[:, D_EXPERT:]).astype(BF16)
        y = jnp.dot(act, wd_s[...], preferred_element_type=F32)
        rows = tile * EXPERT_TILE + lax.broadcasted_iota(jnp.int32, (EXPERT_TILE, 1), 0)
        mine = (rows >= gs_ref[e]) & (rows < ge_ref[e])

        @pl.when(new_tile)
        def _():
            _store_slab(ys_ref, jnp.where(mine, y, 0.0))

        @pl.when(jnp.logical_not(new_tile))
        def _():
            _store_slab(ys_ref, jnp.where(mine, y, _load_slab(ys_ref, EXPERT_TILE)))


def _experts(layer, visit_tile, visit_expert, g_start, g_end, n_visits, xs_slab, w_gate, w_up, w_down):
    rows = EXPERT_TILE * CHUNKS
    grid_spec = pltpu.PrefetchScalarGridSpec(
        num_scalar_prefetch=5,
        grid=(N_VISITS,),
        in_specs=[
            pl.BlockSpec((rows, LANES), lambda i, vt, ve, gs, ge, nv: (vt[i], 0)),
            pl.BlockSpec((None, None, D_MODEL, D_EXPERT), lambda i, vt, ve, gs, ge, nv: (layer, ve[i], 0, 0)),
            pl.BlockSpec((None, None, D_MODEL, D_EXPERT), lambda i, vt, ve, gs, ge, nv: (layer, ve[i], 0, 0)),
            pl.BlockSpec((None, None, D_EXPERT, D_MODEL), lambda i, vt, ve, gs, ge, nv: (layer, ve[i], 0, 0)),
        ],
        out_specs=pl.BlockSpec((rows, LANES), lambda i, vt, ve, gs, ge, nv: (vt[i], 0)),
        scratch_shapes=[pltpu.VMEM((D_MODEL, 2 * D_EXPERT), BF16), pltpu.VMEM((D_EXPERT, D_MODEL), BF16)],
    )
    return pl.pallas_call(
        _experts_kernel,
        grid_spec=grid_spec,
        out_shape=jax.ShapeDtypeStruct((N_ASSIGN * CHUNKS, LANES), F32),
        compiler_params=pltpu.CompilerParams(dimension_semantics=("arbitrary",), vmem_limit_bytes=48 << 20),
        name="experts",
    )(visit_tile, visit_expert, g_start, g_end, n_visits, xs_slab, w_gate, w_up, w_down)


def _gather_two(pos0_ref, pos1_ref, ys_ref, buf0, buf1, sem):
    unroll = 8

    def issue(jo, carry):
        for u in range(unroll):
            j = jo * unroll + u
            _row_copy(ys_ref, pos0_ref[j], buf0, j, sem).start()
            _row_copy(ys_ref, pos1_ref[j], buf1, j, sem).start()
        return carry

    lax.fori_loop(0, TOK_TILE // unroll, issue, 0)

    def drain(jo, carry):
        for u in range(unroll):
            _row_copy(ys_ref, 0, buf0, 0, sem).wait()
            _row_copy(ys_ref, 0, buf1, 0, sem).wait()
        return carry

    lax.fori_loop(0, TOK_TILE // unroll, drain, 0)
    return _load_slab(buf0, TOK_TILE), _load_slab(buf1, TOK_TILE)


def _moe_residual_and_ple(h_ref, mf_ref, y0, y1, p_ref, gple_ref, wpg_ref, wpu_ref):
    gates = mf_ref[...]
    h = h_ref[...] + (y0 * gates[:, 0:1] + y1 * gates[:, 1:2])
    hn = _rms(h, gple_ref[...]).astype(BF16)
    gate = jnp.dot(hn, wpg_ref[...], preferred_element_type=F32)
    gate = 1.0 / (1.0 + jnp.exp(-gate))
    up = jnp.dot(p_ref[...].astype(BF16), wpu_ref[...], preferred_element_type=F32)
    return h + gate * up


def _rope(x, cos, sin_signed):
    n, width = x.shape
    reps = width // LANES
    cos_t = jnp.concatenate([cos] * reps, axis=1)
    sin_t = jnp.concatenate([sin_signed] * reps, axis=1)
    lane = lax.broadcasted_iota(jnp.int32, (n, width), 1)
    first_half = (lane % HEAD_DIM) < (HEAD_DIM // 2)
    partner = jnp.where(first_half, pltpu.roll(x, width - HEAD_DIM // 2, axis=1), pltpu.roll(x, HEAD_DIM // 2, axis=1))
    return x * cos_t + partner * sin_t


def _combine0_kernel(pos0_ref, pos1_ref, h_ref, mf_ref, p_ref, cos_ref, sin_ref, gple_ref, wpg_ref, wpu_ref,
                     gkv_ref, wk_ref, wv_ref, gq_ref, wq_ref, ys_ref,
                     ho_ref, k_ref, v_ref, q_ref, buf0, buf1, sem):
    y0, y1 = _gather_two(pos0_ref, pos1_ref, ys_ref, buf0, buf1, sem)
    h = _moe_residual_and_ple(h_ref, mf_ref, y0, y1, p_ref, gple_ref, wpg_ref, wpu_ref)
    ho_ref[...] = h
    cos = cos_ref[...]
    sin = sin_ref[...]
    kvn = _rms(h, gkv_ref[...]).astype(BF16)
    k = jnp.dot(kvn, wk_ref[...], preferred_element_type=F32)
    k_ref[...] = _rope(k, cos, sin).astype(BF16)
    v_ref[...] = jnp.dot(kvn, wv_ref[...], preferred_element_type=F32).astype(BF16)
    qn = _rms(h, gq_ref[...]).astype(BF16)
    q = jnp.dot(qn, wq_ref[...], preferred_element_type=F32)
    q_ref[...] = (_rope(q, cos, sin) * (HEAD_DIM ** -0.5)).astype(BF16)


def _combine1_kernel(pos0_ref, pos1_ref, h_ref, mf_ref, p_ref, gple_ref, wpg_ref, wpu_ref, gfin_ref, ys_ref,
                     out_ref, buf0, buf1, sem):
    y0, y1 = _gather_two(pos0_ref, pos1_ref, ys_ref, buf0, buf1, sem)
    h = _moe_residual_and_ple(h_ref, mf_ref, y0, y1, p_ref, gple_ref, wpg_ref, wpu_ref)
    out_ref[...] = _rms(h, gfin_ref[...])


def _combine_specs():
    smem = lambda: pl.BlockSpec((TOK_TILE,), lambda i: (i,), memory_space=pltpu.SMEM)
    tok = lambda w: pl.BlockSpec((TOK_TILE, w), lambda i: (i, 0))
    full = lambda a, b: pl.BlockSpec((a, b), lambda i: (0, 0))
    scratch = [pltpu.VMEM((TOK_TILE * CHUNKS, LANES), F32), pltpu.VMEM((TOK_TILE * CHUNKS, LANES), F32),
               pltpu.SemaphoreType.DMA(())]
    return smem, tok, full, scratch


def _combine0(pos0, pos1, h, mf, p, cos_t, sin_t, g_ple, w_pg, w_pu, g_kv, w_k, w_v, g_q, w_q, ys_slab):
    smem, tok, full, scratch = _combine_specs()
    return pl.pallas_call(
        _combine0_kernel,
        grid=(N_TOK // TOK_TILE,),
        in_specs=[smem(), smem(), tok(D_MODEL), tok(LANES), tok(PLE_DIM), tok(LANES), tok(LANES),
                  full(1, D_MODEL), full(D_MODEL, D_MODEL), full(PLE_DIM, D_MODEL),
                  full(1, D_MODEL), full(D_MODEL, KV_DIM), full(D_MODEL, KV_DIM),
                  full(1, D_MODEL), full(D_MODEL, D_MODEL),
                  pl.BlockSpec(memory_space=pl.ANY)],
        out_specs=[tok(D_MODEL), tok(KV_DIM), tok(KV_DIM), tok(D_MODEL)],
        out_shape=[jax.ShapeDtypeStruct((N_TOK, D_MODEL), F32),
                   jax.ShapeDtypeStruct((N_TOK, KV_DIM), BF16),
                   jax.ShapeDtypeStruct((N_TOK, KV_DIM), BF16),
                   jax.ShapeDtypeStruct((N_TOK, D_MODEL), BF16)],
        scratch_shapes=scratch,
        compiler_params=pltpu.CompilerParams(dimension_semantics=("arbitrary",), vmem_limit_bytes=56 << 20),
        name="combine0",
    )(pos0, pos1, h, mf, p, cos_t, sin_t, g_ple, w_pg, w_pu, g_kv, w_k, w_v, g_q, w_q, ys_slab)


def _combine1(pos0, pos1, h, mf, p, g_ple, w_pg, w_pu, g_fin, ys_slab):
    smem, tok, full, scratch = _combine_specs()
    return pl.pallas_call(
        _combine1_kernel,
        grid=(N_TOK // TOK_TILE,),
        in_specs=[smem(), smem(), tok(D_MODEL), tok(LANES), tok(PLE_DIM),
                  full(1, D_MODEL), full(D_MODEL, D_MODEL), full(PLE_DIM, D_MODEL), full(1, D_MODEL),
                  pl.BlockSpec(memory_space=pl.ANY)],
        out_specs=tok(D_MODEL),
        out_shape=jax.ShapeDtypeStruct((N_TOK, D_MODEL), F32),
        scratch_shapes=scratch,
        compiler_params=pltpu.CompilerParams(dimension_semantics=("arbitrary",), vmem_limit_bytes=48 << 20),
        name="combine1",
    )(pos0, pos1, h, mf, p, g_ple, w_pg, w_pu, g_fin, ys_slab)


def _attn1_kernel(sink_ref, q_ref, kc_ref, kp_ref, vc_ref, vp_ref, h_ref, wo_ref, gffn_ref, wr_ref, br_ref,
                  ho_ref, xn_ref, mi_ref, mf_ref, cnt_ref, run_ref):
    i = pl.program_id(0)
    blocks_per_seq = SEQ // ATT_BLOCK

    @pl.when(i == 0)
    def _():
        run_ref[...] = jnp.zeros_like(run_ref)

    has_prev = (i % blocks_per_seq) != 0
    kcat = jnp.concatenate([kp_ref[...], kc_ref[...]], axis=0)
    vcat = jnp.concatenate([vp_ref[...], vc_ref[...]], axis=0)
    q = q_ref[...]
    qs = jnp.concatenate([q[:, g * KV_DIM:(g + 1) * KV_DIM] for g in range(Q_PER_KV)], axis=0)
    n_rows = Q_PER_KV * ATT_BLOCK
    row = lax.broadcasted_iota(jnp.int32, (n_rows, 2 * ATT_BLOCK), 0)
    col = lax.broadcasted_iota(jnp.int32, (n_rows, 2 * ATT_BLOCK), 1)
    rel = (row % ATT_BLOCK) + ATT_BLOCK - col
    keep = (rel >= 0) & (rel < WINDOW) & (has_prev | (col >= ATT_BLOCK))
    head_of_lane = lax.broadcasted_iota(jnp.int32, (2 * ATT_BLOCK, KV_DIM), 1) // HEAD_DIM
    grp_of_row = lax.broadcasted_iota(jnp.int32, (n_rows, 1), 0) // ATT_BLOCK
    o = jnp.zeros((n_rows, KV_DIM), F32)
    for kh in range(N_KV_HEADS):
        k_h = jnp.where(head_of_lane == kh, kcat, jnp.zeros_like(kcat))
        v_h = jnp.where(head_of_lane == kh, vcat, jnp.zeros_like(vcat))
        s = lax.dot_general(qs, k_h, (((1,), (1,)), ((), ())), preferred_element_type=F32)
        s = jnp.where(keep, s, NEG_INF)
        sink = jnp.zeros((n_rows, 1), F32)
        for g in range(Q_PER_KV):
            sink = jnp.where(grp_of_row == g, sink_ref[kh * Q_PER_KV + g], sink)
        m = jnp.maximum(jnp.max(s, axis=1, keepdims=True), sink)
        e = jnp.exp(s - m)
        denom = jnp.sum(e, axis=1, keepdims=True) + jnp.exp(sink - m)
        pr = (e / denom).astype(BF16)
        o = o + jnp.dot(pr, v_h, preferred_element_type=F32)
    o_cat = jnp.concatenate([o[g * ATT_BLOCK:(g + 1) * ATT_BLOCK, :] for g in range(Q_PER_KV)], axis=1)
    h = h_ref[...] + jnp.dot(o_cat.astype(BF16), wo_ref[...], preferred_element_type=F32)
    ho_ref[...] = h
    xn = _rms(h, gffn_ref[...])
    _store_slab(xn_ref, xn)
    _route_and_rank(xn, wr_ref, br_ref, run_ref, mi_ref, mf_ref, cnt_ref)


def _attn1(sinks, q, k, v, h, w_o, g_ffn, w_r, b_r):
    blocks_per_seq = SEQ // ATT_BLOCK
    cur = lambda w: pl.BlockSpec((ATT_BLOCK, w), lambda i: (i, 0))
    prev = lambda w: pl.BlockSpec((ATT_BLOCK, w), lambda i: (jnp.maximum(i - 1, 0), 0))
    full = lambda a, b: pl.BlockSpec((a, b), lambda i: (0, 0))
    return pl.pallas_call(
        _attn1_kernel,
        grid=(N_TOK // ATT_BLOCK,),
        in_specs=[pl.BlockSpec(memory_space=pltpu.SMEM),
                  cur(D_MODEL), cur(KV_DIM), prev(KV_DIM), cur(KV_DIM), prev(KV_DIM), cur(D_MODEL),
                  full(D_MODEL, D_MODEL), full(1, D_MODEL), full(D_MODEL, LANES), full(1, LANES)],
        out_specs=[cur(D_MODEL),
                   pl.BlockSpec((ATT_BLOCK * CHUNKS, LANES), lambda i: (i, 0)),
                   pl.BlockSpec((SUBLANES, ATT_BLOCK), lambda i: (0, i)), cur(LANES),
                   pl.BlockSpec((SUBLANES, LANES), lambda i: (0, 0))],
        out_shape=[jax.ShapeDtypeStruct((N_TOK, D_MODEL), F32),
                   jax.ShapeDtypeStruct((N_TOK * CHUNKS, LANES), F32),
                   jax.ShapeDtypeStruct((SUBLANES, N_TOK), jnp.int32),
                   jax.ShapeDtypeStruct((N_TOK, LANES), F32),
                   jax.ShapeDtypeStruct((SUBLANES, LANES), F32)],
        scratch_shapes=[pltpu.VMEM((SUBLANES, LANES), F32)],
        compiler_params=pltpu.CompilerParams(dimension_semantics=("arbitrary",), vmem_limit_bytes=48 << 20),
        name="attn1",
    )(sinks, q, k, k, v, v, h, w_o, g_ffn, w_r, b_r)


def _routing_tables(meta_i, counts_f):
    counts = counts_f[0, :N_EXPERTS].astype(jnp.int32)
    g_end = jnp.cumsum(counts)
    g_start = g_end - counts
    pos0 = g_start[meta_i[0]] + meta_i[2]
    pos1 = g_start[meta_i[1]] + meta_i[3]
    first_tile = g_start // EXPERT_TILE
    last_tile = jnp.where(counts > 0, (g_end - 1) // EXPERT_TILE, first_tile - 1)
    n_vis = last_tile - first_tile + 1
    v_end = jnp.cumsum(n_vis)
    v_start = v_end - n_vis
    total = v_end[-1]
    idx = jnp.arange(N_VISITS, dtype=jnp.int32)
    e_of = jnp.minimum(jnp.searchsorted(v_end, idx, side="right"), N_EXPERTS - 1).astype(jnp.int32)
    t_of = first_tile[e_of] + (idx - v_start[e_of])
    last = jnp.maximum(total - 1, 0)
    valid = idx < total
    visit_expert = jnp.where(valid, e_of, e_of[last]).astype(jnp.int32)
    visit_tile = jnp.where(valid, t_of, t_of[last]).astype(jnp.int32)
    return (pos0.astype(jnp.int32), pos1.astype(jnp.int32), visit_tile, visit_expert,
            g_start.astype(jnp.int32), g_end.astype(jnp.int32), total.reshape(1).astype(jnp.int32))


def _router_weights(w_grp, b_grp, w_exp, b_exp):
    w = jnp.zeros((D_MODEL, LANES), F32)
    w = w.at[:, :N_EXPERTS].set(w_exp).at[:, GROUP_LANE0:GROUP_LANE0 + N_GROUPS].set(w_grp)
    b = jnp.zeros((1, LANES), F32)
    b = b.at[0, :N_EXPERTS].set(b_exp).at[0, GROUP_LANE0:GROUP_LANE0 + N_GROUPS].set(b_grp)
    return w, b


def _rope_tables(positions):
    inv = ROPE_THETA ** (-jnp.arange(0, HEAD_DIM, 2, dtype=F32) / HEAD_DIM)
    ang = positions.astype(F32).reshape(N_TOK, 1) * inv
    cos = jnp.cos(ang)
    sin = jnp.sin(ang)
    cos_t = jnp.concatenate([cos, cos, cos, cos], axis=1)
    sin_t = jnp.concatenate([-sin, sin, -sin, sin], axis=1)
    return cos_t, sin_t


def _moe(layer, meta_i, counts_f, xn_slab, w_gate, w_up, w_down):
    pos0, pos1, vt, ve, gs, ge, nv = _routing_tables(meta_i, counts_f)
    xs_slab = _dispatch(pos0, pos1, xn_slab)
    ys_slab = _experts(layer, vt, ve, gs, ge, nv, xs_slab, w_gate, w_up, w_down)
    return pos0, pos1, ys_slab


def kernel(x, p, positions, g_mix, g_ffn, pool_w, pool_scale, g_kv, w_k, w_v, w_q, w_o, sinks, w_group_router, b_group_router, w_expert_router, b_expert_router, w_exp_gate, w_exp_up, w_exp_down, g_ple, w_ple_gate, w_ple_up, g_final):
    row = lambda a: a.reshape(1, -1)
    x2 = x.reshape(N_TOK, D_MODEL)
    p2 = p.reshape(2, N_TOK, PLE_DIM)
    cos_t, sin_t = _rope_tables(positions)
    w_q_perm = w_q[0].reshape(D_MODEL, N_KV_HEADS, Q_PER_KV, HEAD_DIM).transpose(0, 2, 1, 3).reshape(D_MODEL, D_MODEL)
    w_o_perm = w_o[0].reshape(N_KV_HEADS, Q_PER_KV, HEAD_DIM, D_MODEL).transpose(1, 0, 2, 3).reshape(D_MODEL, D_MODEL)

    wr0, br0 = _router_weights(w_group_router[0], b_group_router[0], w_expert_router[0], b_expert_router[0])
    h1, xn_slab, meta_i, meta_f, counts = _mixer0(
        x2, row(g_mix[0]), pool_w[0].astype(BF16), row(pool_scale[0]), row(g_ffn[0]), wr0, br0)
    pos0, pos1, ys_slab = _moe(0, meta_i, counts, xn_slab, w_exp_gate, w_exp_up, w_exp_down)
    h3, k, v, q = _combine0(
        pos0, pos1, h1, meta_f, p2[0], cos_t, sin_t, row(g_ple[0]), w_ple_gate[0].astype(BF16),
        w_ple_up[0].astype(BF16), row(g_kv), w_k.astype(BF16), w_v.astype(BF16), row(g_mix[1]),
        w_q_perm.astype(BF16), ys_slab)

    wr1, br1 = _router_weights(w_group_router[1], b_group_router[1], w_expert_router[1], b_expert_router[1])
    h4, xn_slab, meta_i, meta_f, counts = _attn1(
        sinks[0], q, k, v, h3, w_o_perm.astype(BF16), row(g_ffn[1]), wr1, br1)
    pos0, pos1, ys_slab = _moe(1, meta_i, counts, xn_slab, w_exp_gate, w_exp_up, w_exp_down)
    out = _combine1(pos0, pos1, h4, meta_f, p2[1], row(g_ple[1]), w_ple_gate[1].astype(BF16),
                    w_ple_up[1].astype(BF16), row(g_final), ys_slab)
    return out.reshape(BATCH, SEQ, D_MODEL)
```

```python
import functools

import jax
import jax.numpy as jnp
from jax import lax
from jax.experimental import pallas as pl
from jax.experimental.pallas import tpu as pltpu

D_MODEL = 1024
BATCH = 2
SEQ = 8192
N_TOK = BATCH * SEQ
POOL_WINDOWS = (2, 4, 8, 16)
POOL_GROUP = D_MODEL // len(POOL_WINDOWS)
POOL_HALO = 16
HEAD_DIM = 64
N_Q_HEADS = 16
N_KV_HEADS = 4
Q_PER_KV = 4
KV_DIM = N_KV_HEADS * HEAD_DIM
WINDOW = 128
ATT_BLOCK = 128
ROPE_THETA = 10000.0
N_GROUPS = 4
EXPERTS_PER_GROUP = 8
N_EXPERTS = 32
D_EXPERT = 512
PLE_DIM = 256
EPS = 1e-6

LANES = 128
SUBLANES = 8
CHUNKS = D_MODEL // LANES
N_ASSIGN = 2 * N_TOK

TOK_TILE = 256
DISPATCH_TILE = 512
EXPERT_TILE = 256
N_ROW_TILES = N_ASSIGN // EXPERT_TILE
N_VISITS = N_ROW_TILES + N_EXPERTS - 1
GROUP_LANE0 = N_EXPERTS

F32 = jnp.float32
BF16 = jnp.bfloat16
NEG_INF = float("-inf")


def _rms(x, g):
    return x * lax.rsqrt(jnp.mean(x * x, axis=-1, keepdims=True) + EPS) * g


def _store_slab(ref, val):
    n = val.shape[0]
    for c in range(CHUNKS):
        ref[pl.ds(c, n, stride=CHUNKS), :] = val[:, c * LANES:(c + 1) * LANES]


def _load_slab(ref, n):
    return jnp.concatenate([ref[pl.ds(c, n, stride=CHUNKS), :] for c in range(CHUNKS)], axis=1)


def _route_and_rank(xn, wr_ref, br_ref, run_ref, mi_ref, mf_ref, cnt_ref):
    n = xn.shape[0]
    x_hi = xn.astype(BF16)
    x_lo = (xn - x_hi.astype(F32)).astype(BF16)
    hh_hl = jnp.dot(x_hi, wr_ref[...], preferred_element_type=F32)
    lh = jnp.dot(x_lo, wr_ref[:, :LANES], preferred_element_type=F32)
    logits = hh_hl[:, :LANES] + (hh_hl[:, LANES:] + lh) + br_ref[...]
    lane = lax.broadcasted_iota(jnp.int32, (n, LANES), 1).astype(F32)
    big = float(LANES)
    is_g = (lane >= GROUP_LANE0) & (lane < GROUP_LANE0 + N_GROUPS)
    gl = jnp.where(is_g, logits, NEG_INF)
    gmax = jnp.max(gl, axis=1, keepdims=True)
    gidx = jnp.min(jnp.where(gl == gmax, lane, big), axis=1, keepdims=True) - GROUP_LANE0
    pg = 1.0 / jnp.sum(jnp.exp(gl - gmax), axis=1, keepdims=True)
    lo = gidx * EXPERTS_PER_GROUP
    el = jnp.where((lane >= lo) & (lane < lo + EXPERTS_PER_GROUP), logits, NEG_INF)
    m1 = jnp.max(el, axis=1, keepdims=True)
    i1 = jnp.min(jnp.where(el == m1, lane, big), axis=1, keepdims=True)
    el2 = jnp.where(lane == i1, NEG_INF, el)
    m2 = jnp.max(el2, axis=1, keepdims=True)
    i2 = jnp.min(jnp.where(el2 == m2, lane, big), axis=1, keepdims=True)
    z = jnp.exp(m2 - m1)
    gate0 = pg / (1.0 + z)
    gate1 = pg * z / (1.0 + z)

    oh0 = jnp.where(lane == i1, 1.0, 0.0)
    oh1 = jnp.where(lane == i2, 1.0, 0.0)
    oh = oh0 + oh1
    r = lax.broadcasted_iota(jnp.int32, (n, n), 0)
    c = lax.broadcasted_iota(jnp.int32, (n, n), 1)
    lower = jnp.where(r > c, 1.0, 0.0).astype(BF16)
    prefix = jnp.dot(lower, oh.astype(BF16), preferred_element_type=F32)
    tot = prefix + run_ref[0:1, :]
    rank0 = jnp.sum(oh0 * tot, axis=1, keepdims=True)
    rank1 = jnp.sum(oh1 * tot, axis=1, keepdims=True)
    new_run = run_ref[0:1, :] + jnp.sum(oh, axis=0, keepdims=True)
    run_ref[0:1, :] = new_run
    cnt_ref[...] = jnp.broadcast_to(new_run, cnt_ref.shape)

    meta = jnp.where(lane == 0.0, i1, jnp.where(lane == 1.0, i2,
                     jnp.where(lane == 2.0, rank0, jnp.where(lane == 3.0, rank1, 0.0))))
    mi_ref[...] = meta.T[:SUBLANES, :].astype(jnp.int32)
    mf_ref[...] = jnp.where(lane == 0.0, gate0, jnp.where(lane == 1.0, gate1, 0.0))


def _mixer0_kernel(x_ref, halo_ref, gmix_ref, poolw_ref, pscale_ref, gffn_ref, wr_ref, br_ref,
                   h_ref, xn_ref, mi_ref, mf_ref, cnt_ref, run_ref):
    i = pl.program_id(0)
    tiles_per_seq = SEQ // TOK_TILE

    @pl.when(i == 0)
    def _():
        run_ref[...] = jnp.zeros_like(run_ref)

    seq_tile = i % tiles_per_seq
    x = x_ref[...]
    g = gmix_ref[...]
    hn = _rms(x, g)
    halo = jnp.where(seq_tile == 0, 0.0, _rms(halo_ref[...], g))
    full = jnp.concatenate([halo, hn], axis=0)
    t = (seq_tile * TOK_TILE + lax.broadcasted_iota(jnp.int32, (TOK_TILE, 1), 0) + 1).astype(F32)
    outs = []
    for gi, w in enumerate(POOL_WINDOWS):
        acc = full[:, gi * POOL_GROUP:(gi + 1) * POOL_GROUP]
        shift = 1
        while shift < w:
            acc = acc + pltpu.roll(acc, shift, axis=0)
            shift *= 2
        win = acc[POOL_HALO:, :]
        mean = win / jnp.minimum(t, float(w))
        dlt = (mean - hn[:, gi * POOL_GROUP:(gi + 1) * POOL_GROUP]).astype(BF16)
        outs.append(jnp.dot(dlt, poolw_ref[gi], preferred_element_type=F32))
    y = jnp.concatenate(outs, axis=1)
    h = x + y * pscale_ref[...]
    h_ref[...] = h
    xn = _rms(h, gffn_ref[...])
    _store_slab(xn_ref, xn)
    _route_and_rank(xn, wr_ref, br_ref, run_ref, mi_ref, mf_ref, cnt_ref)


def _mixer0(x, g_mix, pool_w, pool_scale, g_ffn, w_r, b_r):
    n_tiles = N_TOK // TOK_TILE
    halo_per_tile = TOK_TILE // POOL_HALO
    vec = lambda: pl.BlockSpec((1, D_MODEL), lambda i: (0, 0))
    return pl.pallas_call(
        _mixer0_kernel,
        grid=(n_tiles,),
        in_specs=[
            pl.BlockSpec((TOK_TILE, D_MODEL), lambda i: (i, 0)),
            pl.BlockSpec((POOL_HALO, D_MODEL), lambda i: (jnp.maximum(i * halo_per_tile - 1, 0), 0)),
            vec(),
            pl.BlockSpec((len(POOL_WINDOWS), POOL_GROUP, POOL_GROUP), lambda i: (0, 0, 0)),
            vec(), vec(),
            pl.BlockSpec((D_MODEL, 2 * LANES), lambda i: (0, 0)),
            pl.BlockSpec((1, LANES), lambda i: (0, 0)),
        ],
        out_specs=[
            pl.BlockSpec((TOK_TILE, D_MODEL), lambda i: (i, 0)),
            pl.BlockSpec((TOK_TILE * CHUNKS, LANES), lambda i: (i, 0)),
            pl.BlockSpec((SUBLANES, TOK_TILE), lambda i: (0, i)),
            pl.BlockSpec((TOK_TILE, LANES), lambda i: (i, 0)),
            pl.BlockSpec((SUBLANES, LANES), lambda i: (0, 0)),
        ],
        out_shape=[
            jax.ShapeDtypeStruct((N_TOK, D_MODEL), F32),
            jax.ShapeDtypeStruct((N_TOK * CHUNKS, LANES), F32),
            jax.ShapeDtypeStruct((SUBLANES, N_TOK), jnp.int32),
            jax.ShapeDtypeStruct((N_TOK, LANES), F32),
            jax.ShapeDtypeStruct((SUBLANES, LANES), F32),
        ],
        scratch_shapes=[pltpu.VMEM((SUBLANES, LANES), F32)],
        compiler_params=pltpu.CompilerParams(dimension_semantics=("arbitrary",), vmem_limit_bytes=48 << 20),
        name="mixer0",
    )(x, x, g_mix, pool_w, pool_scale, g_ffn, w_r, b_r)


def _row_copy(src, src_row, dst, dst_row, sem):
    return pltpu.make_async_copy(
        src.at[pl.ds(pl.multiple_of(src_row * CHUNKS, CHUNKS), CHUNKS), :],
        dst.at[pl.ds(pl.multiple_of(dst_row * CHUNKS, CHUNKS), CHUNKS), :],
        sem)


def _dispatch_kernel(pos0_ref, pos1_ref, src_ref, dst_ref, sem):
    unroll = 8

    def issue(jo, carry):
        for u in range(unroll):
            j = jo * unroll + u
            _row_copy(src_ref, j, dst_ref, pos0_ref[j], sem).start()
            _row_copy(src_ref, j, dst_ref, pos1_ref[j], sem).start()
        return carry

    lax.fori_loop(0, DISPATCH_TILE // unroll, issue, 0)

    def drain(jo, carry):
        for u in range(2 * unroll):
            _row_copy(src_ref, 0, dst_ref, 0, sem).wait()
        return carry

    lax.fori_loop(0, DISPATCH_TILE // unroll, drain, 0)


def _dispatch(pos0, pos1, xn_slab):
    smem = lambda: pl.BlockSpec((DISPATCH_TILE,), lambda i: (i,), memory_space=pltpu.SMEM)
    return pl.pallas_call(
        _dispatch_kernel,
        grid=(N_TOK // DISPATCH_TILE,),
        in_specs=[smem(), smem(), pl.BlockSpec((DISPATCH_TILE * CHUNKS, LANES), lambda i: (i, 0))],
        out_specs=pl.BlockSpec(memory_space=pl.ANY),
        out_shape=jax.ShapeDtypeStruct((N_ASSIGN * CHUNKS, LANES), F32),
        scratch_shapes=[pltpu.SemaphoreType.DMA(())],
        compiler_params=pltpu.CompilerParams(dimension_semantics=("arbitrary",), has_side_effects=True),
        name="dispatch",
    )(pos0, pos1, xn_slab)


def _experts_kernel(vt_ref, ve_ref, gs_ref, ge_ref, nv_ref, xs_ref, wg_ref, wu_ref, wd_ref, ys_ref,
                    wgu_s, wd_s):
    i = pl.program_id(0)
    prev = jnp.maximum(i - 1, 0)
    e = ve_ref[i]
    tile = vt_ref[i]
    valid = i < nv_ref[0]
    new_expert = (i == 0) | (e != ve_ref[prev])
    new_tile = (i == 0) | (tile != vt_ref[prev])

    @pl.when(valid & new_expert)
    def _():
        wgu_s[:, :D_EXPERT] = wg_ref[...].astype(BF16)
        wgu_s[:, D_EXPERT:] = wu_ref[...].astype(BF16)
        wd_s[...] = wd_ref[...].astype(BF16)

    @pl.when(valid)
    def _():
        x = _load_slab(xs_ref, EXPERT_TILE).astype(BF16)
        gu = jnp.dot(x, wgu_s[...], preferred_element_type=F32)
        gate = gu[:, :D_EXPERT]
        act = (gate / (1.0 + jnp.exp(-gate)) * gu[:, D_EXPERT:]).astype(BF16)
        y = jnp.dot(act, wd_s[...], preferred_element_type=F32)
        rows = tile * EXPERT_TILE + lax.broadcasted_iota(jnp.int32, (EXPERT_TILE, 1), 0)
        mine = (rows >= gs_ref[e]) & (rows < ge_ref[e])

        @pl.when(new_tile)
        def _():
            _store_slab(ys_ref, jnp.where(mine, y, 0.0))

        @pl.when(jnp.logical_not(new_tile))
        def _():
            _store_slab(ys_ref, jnp.where(mine, y, _load_slab(ys_ref, EXPERT_TILE)))


def _experts(layer, visit_tile, visit_expert, g_start, g_end, n_visits, xs_slab, w_gate, w_up, w_down):
    rows = EXPERT_TILE * CHUNKS
    grid_spec = pltpu.PrefetchScalarGridSpec(
        num_scalar_prefetch=5,
        grid=(N_VISITS,),
        in_specs=[
            pl.BlockSpec((rows, LANES), lambda i, vt, ve, gs, ge, nv: (vt[i], 0)),
            pl.BlockSpec((None, None, D_MODEL, D_EXPERT), lambda i, vt, ve, gs, ge, nv: (layer, ve[i], 0, 0)),
            pl.BlockSpec((None, None, D_MODEL, D_EXPERT), lambda i, vt, ve, gs, ge, nv: (layer, ve[i], 0, 0)),
            pl.BlockSpec((None, None, D_EXPERT, D_MODEL), lambda i, vt, ve, gs, ge, nv: (layer, ve[i], 0, 0)),
        ],
        out_specs=pl.BlockSpec((rows, LANES), lambda i, vt, ve, gs, ge, nv: (vt[i], 0)),
        scratch_shapes=[pltpu.VMEM((D_MODEL, 2 * D_EXPERT), BF16), pltpu.VMEM((D_EXPERT, D_MODEL), BF16)],
    )
    return pl.pallas_call(
        _experts_kernel,
        grid_spec=grid_spec,
        out_shape=jax.ShapeDtypeStruct((N_ASSIGN * CHUNKS, LANES), F32),
        compiler_params=pltpu.CompilerParams(dimension_semantics=("arbitrary",), vmem_limit_bytes=48 << 20),
        name="experts",
    )(visit_tile, visit_expert, g_start, g_end, n_visits, xs_slab, w_gate, w_up, w_down)


def _with_expert_rows(pos0_ref, pos1_ref, npos0_ref, npos1_ref, ys_ref, bufs, sem, body):
    i = pl.program_id(0)

    def copies(p0_ref, p1_ref, s):
        for j in range(TOK_TILE):
            yield _row_copy(ys_ref, p0_ref[j], bufs[s][0], j, sem.at[s])
            yield _row_copy(ys_ref, p1_ref[j], bufs[s][1], j, sem.at[s])

    @pl.when(i == 0)
    def _():
        for c in copies(pos0_ref, pos1_ref, 0):
            c.start()

    for s in range(2):
        @pl.when(i % 2 == s)
        def _():
            for c in copies(npos0_ref, npos1_ref, 1 - s):
                c.start()
            for c in copies(pos0_ref, pos1_ref, s):
                c.wait()
            body(_load_slab(bufs[s][0], TOK_TILE), _load_slab(bufs[s][1], TOK_TILE))

            @pl.when(i == pl.num_programs(0) - 1)
            def _():
                for c in copies(npos0_ref, npos1_ref, 1 - s):
                    c.wait()


def _moe_residual_and_ple(h_ref, mf_ref, y0, y1, p_ref, gple_ref, wpg_ref, wpu_ref):
    gates = mf_ref[...]
    h = h_ref[...] + (y0 * gates[:, 0:1] + y1 * gates[:, 1:2])
    hn = _rms(h, gple_ref[...]).astype(BF16)
    gate = jnp.dot(hn, wpg_ref[...], preferred_element_type=F32)
    gate = 1.0 / (1.0 + jnp.exp(-gate))
    up = jnp.dot(p_ref[...].astype(BF16), wpu_ref[...], preferred_element_type=F32)
    return h + gate * up


def _rope(x, cos, sin_signed):
    n, width = x.shape
    reps = width // LANES
    cos_t = jnp.concatenate([cos] * reps, axis=1)
    sin_t = jnp.concatenate([sin_signed] * reps, axis=1)
    lane = lax.broadcasted_iota(jnp.int32, (n, width), 1)
    first_half = (lane % HEAD_DIM) < (HEAD_DIM // 2)
    partner = jnp.where(first_half, pltpu.roll(x, width - HEAD_DIM // 2, axis=1), pltpu.roll(x, HEAD_DIM // 2, axis=1))
    return x * cos_t + partner * sin_t


def _combine0_kernel(pos0_ref, pos1_ref, npos0_ref, npos1_ref, h_ref, mf_ref, p_ref, cos_ref, sin_ref, gple_ref, wpg_ref, wpu_ref,
                     gkv_ref, wk_ref, wv_ref, gq_ref, wq_ref, ys_ref,
                     ho_ref, k_ref, v_ref, q_ref, b00, b01, b10, b11, sem):
    def body(y0, y1):
        h = _moe_residual_and_ple(h_ref, mf_ref, y0, y1, p_ref, gple_ref, wpg_ref, wpu_ref)
        ho_ref[...] = h
        cos = cos_ref[...]
        sin = sin_ref[...]
        kvn = _rms(h, gkv_ref[...]).astype(BF16)
        k = jnp.dot(kvn, wk_ref[...], preferred_element_type=F32)
        k_ref[...] = _rope(k, cos, sin).astype(BF16)
        v_ref[...] = jnp.dot(kvn, wv_ref[...], preferred_element_type=F32).astype(BF16)
        qn = _rms(h, gq_ref[...]).astype(BF16)
        q = jnp.dot(qn, wq_ref[...], preferred_element_type=F32)
        q_ref[...] = (_rope(q, cos, sin) * (HEAD_DIM ** -0.5)).astype(BF16)

    _with_expert_rows(pos0_ref, pos1_ref, npos0_ref, npos1_ref, ys_ref, ((b00, b01), (b10, b11)), sem, body)


def _combine1_kernel(pos0_ref, pos1_ref, npos0_ref, npos1_ref, h_ref, mf_ref, p_ref, gple_ref, wpg_ref, wpu_ref, gfin_ref, ys_ref,
                     out_ref, b00, b01, b10, b11, sem):
    def body(y0, y1):
        h = _moe_residual_and_ple(h_ref, mf_ref, y0, y1, p_ref, gple_ref, wpg_ref, wpu_ref)
        out_ref[...] = _rms(h, gfin_ref[...])

    _with_expert_rows(pos0_ref, pos1_ref, npos0_ref, npos1_ref, ys_ref, ((b00, b01), (b10, b11)), sem, body)


def _combine_specs():
    last = N_TOK // TOK_TILE - 1
    smem = lambda: pl.BlockSpec((TOK_TILE,), lambda i: (i,), memory_space=pltpu.SMEM)
    smem_next = lambda: pl.BlockSpec((TOK_TILE,), lambda i: (jnp.minimum(i + 1, last),), memory_space=pltpu.SMEM)
    tok = lambda w: pl.BlockSpec((TOK_TILE, w), lambda i: (i, 0))
    full = lambda a, b: pl.BlockSpec((a, b), lambda i: (0, 0))
    scratch = [pltpu.VMEM((TOK_TILE * CHUNKS, LANES), F32) for _ in range(4)] + [pltpu.SemaphoreType.DMA((2,))]
    ple = lambda layer: pl.BlockSpec((None, TOK_TILE, PLE_DIM), lambda i: (layer, i, 0))
    return smem, smem_next, tok, ple, full, scratch


def _combine0(pos0, pos1, h, mf, p, cos_t, sin_t, g_ple, w_pg, w_pu, g_kv, w_k, w_v, g_q, w_q, ys_slab):
    smem, smem_next, tok, ple, full, scratch = _combine_specs()
    return pl.pallas_call(
        _combine0_kernel,
        grid=(N_TOK // TOK_TILE,),
        in_specs=[smem(), smem(), smem_next(), smem_next(), tok(D_MODEL), tok(LANES), ple(0), tok(LANES), tok(LANES),
                  full(1, D_MODEL), full(D_MODEL, D_MODEL), full(PLE_DIM, D_MODEL),
                  full(1, D_MODEL), full(D_MODEL, KV_DIM), full(D_MODEL, KV_DIM),
                  full(1, D_MODEL), full(D_MODEL, D_MODEL),
                  pl.BlockSpec(memory_space=pl.ANY)],
        out_specs=[tok(D_MODEL), tok(KV_DIM), tok(KV_DIM), tok(D_MODEL)],
        out_shape=[jax.ShapeDtypeStruct((N_TOK, D_MODEL), F32),
                   jax.ShapeDtypeStruct((N_TOK, KV_DIM), BF16),
                   jax.ShapeDtypeStruct((N_TOK, KV_DIM), BF16),
                   jax.ShapeDtypeStruct((N_TOK, D_MODEL), BF16)],
        scratch_shapes=scratch,
        compiler_params=pltpu.CompilerParams(dimension_semantics=("arbitrary",), vmem_limit_bytes=56 << 20),
        name="combine0",
    )(pos0, pos1, pos0, pos1, h, mf, p, cos_t, sin_t, g_ple, w_pg, w_pu, g_kv, w_k, w_v, g_q, w_q, ys_slab)


def _combine1(pos0, pos1, h, mf, p, g_ple, w_pg, w_pu, g_fin, ys_slab):
    smem, smem_next, tok, ple, full, scratch = _combine_specs()
    return pl.pallas_call(
        _combine1_kernel,
        grid=(N_TOK // TOK_TILE,),
        in_specs=[smem(), smem(), smem_next(), smem_next(), tok(D_MODEL), tok(LANES), ple(1),
                  full(1, D_MODEL), full(D_MODEL, D_MODEL), full(PLE_DIM, D_MODEL), full(1, D_MODEL),
                  pl.BlockSpec(memory_space=pl.ANY)],
        out_specs=tok(D_MODEL),
        out_shape=jax.ShapeDtypeStruct((N_TOK, D_MODEL), F32),
        scratch_shapes=scratch,
        compiler_params=pltpu.CompilerParams(dimension_semantics=("arbitrary",), vmem_limit_bytes=48 << 20),
        name="combine1",
    )(pos0, pos1, pos0, pos1, h, mf, p, g_ple, w_pg, w_pu, g_fin, ys_slab)


def _attn1_kernel(sink_ref, q_ref, kc_ref, kp_ref, vc_ref, vp_ref, h_ref, wo_ref, gffn_ref, wr_ref, br_ref,
                  ho_ref, xn_ref, mi_ref, mf_ref, cnt_ref, run_ref):
    i = pl.program_id(0)
    blocks_per_seq = SEQ // ATT_BLOCK

    @pl.when(i == 0)
    def _():
        run_ref[...] = jnp.zeros_like(run_ref)

    has_prev = (i % blocks_per_seq) != 0
    kcat = jnp.concatenate([kp_ref[...], kc_ref[...]], axis=0)
    vcat = jnp.concatenate([vp_ref[...], vc_ref[...]], axis=0)
    q = q_ref[...]
    qs = jnp.concatenate([q[:, g * KV_DIM:(g + 1) * KV_DIM] for g in range(Q_PER_KV)], axis=0)
    n_rows = Q_PER_KV * ATT_BLOCK
    row = lax.broadcasted_iota(jnp.int32, (n_rows, 2 * ATT_BLOCK), 0)
    col = lax.broadcasted_iota(jnp.int32, (n_rows, 2 * ATT_BLOCK), 1)
    rel = (row % ATT_BLOCK) + ATT_BLOCK - col
    keep = (rel >= 0) & (rel < WINDOW) & (has_prev | (col >= ATT_BLOCK))
    head_of_lane = lax.broadcasted_iota(jnp.int32, (2 * ATT_BLOCK, KV_DIM), 1) // HEAD_DIM
    grp_of_row = lax.broadcasted_iota(jnp.int32, (n_rows, 1), 0) // ATT_BLOCK
    o = jnp.zeros((n_rows, KV_DIM), F32)
    for kh in range(N_KV_HEADS):
        k_h = jnp.where(head_of_lane == kh, kcat, jnp.zeros_like(kcat))
        v_h = jnp.where(head_of_lane == kh, vcat, jnp.zeros_like(vcat))
        s = lax.dot_general(qs, k_h, (((1,), (1,)), ((), ())), preferred_element_type=F32)
        s = jnp.where(keep, s, NEG_INF)
        sink = jnp.zeros((n_rows, 1), F32)
        for g in range(Q_PER_KV):
            sink = jnp.where(grp_of_row == g, sink_ref[kh * Q_PER_KV + g], sink)
        m = jnp.maximum(jnp.max(s, axis=1, keepdims=True), sink)
        e = jnp.exp(s - m)
        denom = jnp.sum(e, axis=1, keepdims=True) + jnp.exp(sink - m)
        pr = (e / denom).astype(BF16)
        o = o + jnp.dot(pr, v_h, preferred_element_type=F32)
    o_cat = jnp.concatenate([o[g * ATT_BLOCK:(g + 1) * ATT_BLOCK, :] for g in range(Q_PER_KV)], axis=1)
    h = h_ref[...] + jnp.dot(o_cat.astype(BF16), wo_ref[...], preferred_element_type=F32)
    ho_ref[...] = h
    xn = _rms(h, gffn_ref[...])
    _store_slab(xn_ref, xn)
    _route_and_rank(xn, wr_ref, br_ref, run_ref, mi_ref, mf_ref, cnt_ref)


def _attn1(sinks, q, k, v, h, w_o, g_ffn, w_r, b_r):
    blocks_per_seq = SEQ // ATT_BLOCK
    cur = lambda w: pl.BlockSpec((ATT_BLOCK, w), lambda i: (i, 0))
    prev = lambda w: pl.BlockSpec((ATT_BLOCK, w), lambda i: (jnp.maximum(i - 1, 0), 0))
    full = lambda a, b: pl.BlockSpec((a, b), lambda i: (0, 0))
    return pl.pallas_call(
        _attn1_kernel,
        grid=(N_TOK // ATT_BLOCK,),
        in_specs=[pl.BlockSpec(memory_space=pltpu.SMEM),
                  cur(D_MODEL), cur(KV_DIM), prev(KV_DIM), cur(KV_DIM), prev(KV_DIM), cur(D_MODEL),
                  full(D_MODEL, D_MODEL), full(1, D_MODEL), full(D_MODEL, 2 * LANES), full(1, LANES)],
        out_specs=[cur(D_MODEL),
                   pl.BlockSpec((ATT_BLOCK * CHUNKS, LANES), lambda i: (i, 0)),
                   pl.BlockSpec((SUBLANES, ATT_BLOCK), lambda i: (0, i)), cur(LANES),
                   pl.BlockSpec((SUBLANES, LANES), lambda i: (0, 0))],
        out_shape=[jax.ShapeDtypeStruct((N_TOK, D_MODEL), F32),
                   jax.ShapeDtypeStruct((N_TOK * CHUNKS, LANES), F32),
                   jax.ShapeDtypeStruct((SUBLANES, N_TOK), jnp.int32),
                   jax.ShapeDtypeStruct((N_TOK, LANES), F32),
                   jax.ShapeDtypeStruct((SUBLANES, LANES), F32)],
        scratch_shapes=[pltpu.VMEM((SUBLANES, LANES), F32)],
        compiler_params=pltpu.CompilerParams(dimension_semantics=("arbitrary",), vmem_limit_bytes=48 << 20),
        name="attn1",
    )(sinks, q, k, k, v, v, h, w_o, g_ffn, w_r, b_r)


def _routing_tables(meta_i, counts_f):
    counts = counts_f[0, :N_EXPERTS].astype(jnp.int32)
    g_end = jnp.cumsum(counts)
    g_start = g_end - counts
    pos0 = g_start[meta_i[0]] + meta_i[2]
    pos1 = g_start[meta_i[1]] + meta_i[3]
    first_tile = g_start // EXPERT_TILE
    last_tile = jnp.where(counts > 0, (g_end - 1) // EXPERT_TILE, first_tile - 1)
    n_vis = last_tile - first_tile + 1
    v_end = jnp.cumsum(n_vis)
    v_start = v_end - n_vis
    total = v_end[-1]
    idx = jnp.arange(N_VISITS, dtype=jnp.int32)
    e_of = jnp.minimum(jnp.searchsorted(v_end, idx, side="right"), N_EXPERTS - 1).astype(jnp.int32)
    t_of = first_tile[e_of] + (idx - v_start[e_of])
    last = jnp.maximum(total - 1, 0)
    valid = idx < total
    visit_expert = jnp.where(valid, e_of, e_of[last]).astype(jnp.int32)
    visit_tile = jnp.where(valid, t_of, t_of[last]).astype(jnp.int32)
    return (pos0.astype(jnp.int32), pos1.astype(jnp.int32), visit_tile, visit_expert,
            g_start.astype(jnp.int32), g_end.astype(jnp.int32), total.reshape(1).astype(jnp.int32))


def _router_weights(w_grp, b_grp, w_exp, b_exp):
    w = jnp.zeros((D_MODEL, LANES), F32)
    w = w.at[:, :N_EXPERTS].set(w_exp).at[:, GROUP_LANE0:GROUP_LANE0 + N_GROUPS].set(w_grp)
    b = jnp.zeros((1, LANES), F32)
    b = b.at[0, :N_EXPERTS].set(b_exp).at[0, GROUP_LANE0:GROUP_LANE0 + N_GROUPS].set(b_grp)
    w_hi = w.astype(BF16)
    w_lo = (w - w_hi.astype(F32)).astype(BF16)
    return jnp.concatenate([w_hi, w_lo], axis=1), b


def _rope_tables(positions):
    inv = ROPE_THETA ** (-jnp.arange(0, HEAD_DIM, 2, dtype=F32) / HEAD_DIM)
    ang = positions.astype(F32).reshape(N_TOK, 1) * inv
    cos = jnp.cos(ang)
    sin = jnp.sin(ang)
    cos_t = jnp.concatenate([cos, cos, cos, cos], axis=1)
    sin_t = jnp.concatenate([-sin, sin, -sin, sin], axis=1)
    return cos_t, sin_t


def _moe(layer, meta_i, counts_f, xn_slab, w_gate, w_up, w_down):
    pos0, pos1, vt, ve, gs, ge, nv = _routing_tables(meta_i, counts_f)
    xs_slab = _dispatch(pos0, pos1, xn_slab)
    ys_slab = _experts(layer, vt, ve, gs, ge, nv, xs_slab, w_gate, w_up, w_down)
    return pos0, pos1, ys_slab


def kernel(x, p, positions, g_mix, g_ffn, pool_w, pool_scale, g_kv, w_k, w_v, w_q, w_o, sinks, w_group_router, b_group_router, w_expert_router, b_expert_router, w_exp_gate, w_exp_up, w_exp_down, g_ple, w_ple_gate, w_ple_up, g_final):
    row = lambda a: a.reshape(1, -1)
    x2 = x.reshape(N_TOK, D_MODEL)
    p2 = p.reshape(2, N_TOK, PLE_DIM)
    cos_t, sin_t = _rope_tables(positions)
    w_q_perm = w_q[0].reshape(D_MODEL, N_KV_HEADS, Q_PER_KV, HEAD_DIM).transpose(0, 2, 1, 3).reshape(D_MODEL, D_MODEL)
    w_o_perm = w_o[0].reshape(N_KV_HEADS, Q_PER_KV, HEAD_DIM, D_MODEL).transpose(1, 0, 2, 3).reshape(D_MODEL, D_MODEL)

    wr0, br0 = _router_weights(w_group_router[0], b_group_router[0], w_expert_router[0], b_expert_router[0])
    h1, xn_slab, meta_i, meta_f, counts = _mixer0(
        x2, row(g_mix[0]), pool_w[0].astype(BF16), row(pool_scale[0]), row(g_ffn[0]), wr0, br0)
    pos0, pos1, ys_slab = _moe(0, meta_i, counts, xn_slab, w_exp_gate, w_exp_up, w_exp_down)
    h3, k, v, q = _combine0(
        pos0, pos1, h1, meta_f, p2, cos_t, sin_t, row(g_ple[0]), w_ple_gate[0].astype(BF16),
        w_ple_up[0].astype(BF16), row(g_kv), w_k.astype(BF16), w_v.astype(BF16), row(g_mix[1]),
        w_q_perm.astype(BF16), ys_slab)

    wr1, br1 = _router_weights(w_group_router[1], b_group_router[1], w_expert_router[1], b_expert_router[1])
    h4, xn_slab, meta_i, meta_f, counts = _attn1(
        sinks[0], q, k, v, h3, w_o_perm.astype(BF16), row(g_ffn[1]), wr1, br1)
    pos0, pos1, ys_slab = _moe(1, meta_i, counts, xn_slab, w_exp_gate, w_exp_up, w_exp_down)
    out = _combine1(pos0, pos1, h4, meta_f, p2, row(g_ple[1]), w_ple_gate[1].astype(BF16),
                    w_ple_up[1].astype(BF16), row(g_final), ys_slab)
    return out.reshape(BATCH, SEQ, D_MODEL)
```

```python
import functools

import jax
import jax.numpy as jnp
from jax import lax
from jax.experimental import pallas as pl
from jax.experimental.pallas import tpu as pltpu

D_MODEL = 1024
BATCH = 2
SEQ = 8192
N_TOK = BATCH * SEQ
POOL_WINDOWS = (2, 4, 8, 16)
POOL_GROUP = D_MODEL // len(POOL_WINDOWS)
POOL_HALO = 16
HEAD_DIM = 64
N_Q_HEADS = 16
N_KV_HEADS = 4
Q_PER_KV = 4
KV_DIM = N_KV_HEADS * HEAD_DIM
WINDOW = 128
ATT_BLOCK = 128
ROPE_THETA = 10000.0
N_GROUPS = 4
EXPERTS_PER_GROUP = 8
N_EXPERTS = 32
D_EXPERT = 512
PLE_DIM = 256
EPS = 1e-6

LANES = 128
SUBLANES = 8
CHUNKS = D_MODEL // LANES
N_ASSIGN = 2 * N_TOK

TOK_TILE = 256
DISPATCH_TILE = 512
EXPERT_TILE = 256
N_ROW_TILES = N_ASSIGN // EXPERT_TILE
N_VISITS = N_ROW_TILES + N_EXPERTS - 1
GROUP_LANE0 = N_EXPERTS

F32 = jnp.float32
BF16 = jnp.bfloat16
NEG_INF = float("-inf")


def _rms(x, g):
    return x * lax.rsqrt(jnp.mean(x * x, axis=-1, keepdims=True) + EPS) * g


def _store_slab(ref, val):
    n = val.shape[0]
    for c in range(CHUNKS):
        ref[pl.ds(c, n, stride=CHUNKS), :] = val[:, c * LANES:(c + 1) * LANES]


def _load_slab(ref, n):
    return jnp.concatenate([ref[pl.ds(c, n, stride=CHUNKS), :] for c in range(CHUNKS)], axis=1)


def _route_and_rank(xn, wr_ref, br_ref, run_ref, mi_ref, mf_ref, cnt_ref):
    n = xn.shape[0]
    x_hi = xn.astype(BF16)
    x_lo = (xn - x_hi.astype(F32)).astype(BF16)
    hh_hl = jnp.dot(x_hi, wr_ref[...], preferred_element_type=F32)
    lh = jnp.dot(x_lo, wr_ref[:, :LANES], preferred_element_type=F32)
    logits = hh_hl[:, :LANES] + (hh_hl[:, LANES:] + lh) + br_ref[...]
    lane = lax.broadcasted_iota(jnp.int32, (n, LANES), 1).astype(F32)
    big = float(LANES)
    is_g = (lane >= GROUP_LANE0) & (lane < GROUP_LANE0 + N_GROUPS)
    gl = jnp.where(is_g, logits, NEG_INF)
    gmax = jnp.max(gl, axis=1, keepdims=True)
    gidx = jnp.min(jnp.where(gl == gmax, lane, big), axis=1, keepdims=True) - GROUP_LANE0
    pg = 1.0 / jnp.sum(jnp.exp(gl - gmax), axis=1, keepdims=True)
    lo = gidx * EXPERTS_PER_GROUP
    el = jnp.where((lane >= lo) & (lane < lo + EXPERTS_PER_GROUP), logits, NEG_INF)
    m1 = jnp.max(el, axis=1, keepdims=True)
    i1 = jnp.min(jnp.where(el == m1, lane, big), axis=1, keepdims=True)
    el2 = jnp.where(lane == i1, NEG_INF, el)
    m2 = jnp.max(el2, axis=1, keepdims=True)
    i2 = jnp.min(jnp.where(el2 == m2, lane, big), axis=1, keepdims=True)
    z = jnp.exp(m2 - m1)
    gate0 = pg / (1.0 + z)
    gate1 = pg * z / (1.0 + z)

    oh0 = jnp.where(lane == i1, 1.0, 0.0)
    oh1 = jnp.where(lane == i2, 1.0, 0.0)
    oh = oh0 + oh1
    r = lax.broadcasted_iota(jnp.int32, (n, n), 0)
    c = lax.broadcasted_iota(jnp.int32, (n, n), 1)
    lower = jnp.where(r > c, 1.0, 0.0).astype(BF16)
    prefix = jnp.dot(lower, oh.astype(BF16), preferred_element_type=F32)
    tot = prefix + run_ref[0:1, :]
    rank0 = jnp.sum(oh0 * tot, axis=1, keepdims=True)
    rank1 = jnp.sum(oh1 * tot, axis=1, keepdims=True)
    new_run = run_ref[0:1, :] + jnp.sum(oh, axis=0, keepdims=True)
    run_ref[0:1, :] = new_run
    cnt_ref[...] = jnp.broadcast_to(new_run, cnt_ref.shape)

    meta = jnp.where(lane == 0.0, i1, jnp.where(lane == 1.0, i2,
                     jnp.where(lane == 2.0, rank0, jnp.where(lane == 3.0, rank1, 0.0))))
    mi_ref[...] = meta.T[:SUBLANES, :].astype(jnp.int32)
    mf_ref[...] = jnp.where(lane == 0.0, gate0, jnp.where(lane == 1.0, gate1, 0.0))


def _mixer0_kernel(x_ref, halo_ref, gmix_ref, poolw_ref, pscale_ref, gffn_ref, wr_ref, br_ref,
                   h_ref, xn_ref, mi_ref, mf_ref, cnt_ref, run_ref):
    i = pl.program_id(0)
    tiles_per_seq = SEQ // TOK_TILE

    @pl.when(i == 0)
    def _():
        run_ref[...] = jnp.zeros_like(run_ref)

    seq_tile = i % tiles_per_seq
    x = x_ref[...]
    g = gmix_ref[...]
    hn = _rms(x, g)
    halo = jnp.where(seq_tile == 0, 0.0, _rms(halo_ref[...], g))
    full = jnp.concatenate([halo, hn], axis=0)
    t = (seq_tile * TOK_TILE + lax.broadcasted_iota(jnp.int32, (TOK_TILE, 1), 0) + 1).astype(F32)
    outs = []
    for gi, w in enumerate(POOL_WINDOWS):
        acc = full[:, gi * POOL_GROUP:(gi + 1) * POOL_GROUP]
        shift = 1
        while shift < w:
            acc = acc + pltpu.roll(acc, shift, axis=0)
            shift *= 2
        win = acc[POOL_HALO:, :]
        mean = win / jnp.minimum(t, float(w))
        dlt = (mean - hn[:, gi * POOL_GROUP:(gi + 1) * POOL_GROUP]).astype(BF16)
        outs.append(jnp.dot(dlt, poolw_ref[gi], preferred_element_type=F32))
    y = jnp.concatenate(outs, axis=1)
    h = x + y * pscale_ref[...]
    h_ref[...] = h
    xn = _rms(h, gffn_ref[...])
    _store_slab(xn_ref, xn)
    _route_and_rank(xn, wr_ref, br_ref, run_ref, mi_ref, mf_ref, cnt_ref)


def _mixer0(x, g_mix, pool_w, pool_scale, g_ffn, w_r, b_r):
    n_tiles = N_TOK // TOK_TILE
    halo_per_tile = TOK_TILE // POOL_HALO
    vec = lambda: pl.BlockSpec((1, D_MODEL), lambda i: (0, 0))
    return pl.pallas_call(
        _mixer0_kernel,
        grid=(n_tiles,),
        in_specs=[
            pl.BlockSpec((TOK_TILE, D_MODEL), lambda i: (i, 0)),
            pl.BlockSpec((POOL_HALO, D_MODEL), lambda i: (jnp.maximum(i * halo_per_tile - 1, 0), 0)),
            vec(),
            pl.BlockSpec((len(POOL_WINDOWS), POOL_GROUP, POOL_GROUP), lambda i: (0, 0, 0)),
            vec(), vec(),
            pl.BlockSpec((D_MODEL, 2 * LANES), lambda i: (0, 0)),
            pl.BlockSpec((1, LANES), lambda i: (0, 0)),
        ],
        out_specs=[
            pl.BlockSpec((TOK_TILE, D_MODEL), lambda i: (i, 0)),
            pl.BlockSpec((TOK_TILE * CHUNKS, LANES), lambda i: (i, 0)),
            pl.BlockSpec((SUBLANES, TOK_TILE), lambda i: (0, i)),
            pl.BlockSpec((TOK_TILE, LANES), lambda i: (i, 0)),
            pl.BlockSpec((SUBLANES, LANES), lambda i: (0, 0)),
        ],
        out_shape=[
            jax.ShapeDtypeStruct((N_TOK, D_MODEL), F32),
            jax.ShapeDtypeStruct((N_TOK * CHUNKS, LANES), F32),
            jax.ShapeDtypeStruct((SUBLANES, N_TOK), jnp.int32),
            jax.ShapeDtypeStruct((N_TOK, LANES), F32),
            jax.ShapeDtypeStruct((SUBLANES, LANES), F32),
        ],
        scratch_shapes=[pltpu.VMEM((SUBLANES, LANES), F32)],
        compiler_params=pltpu.CompilerParams(dimension_semantics=("arbitrary",), vmem_limit_bytes=48 << 20),
        name="mixer0",
    )(x, x, g_mix, pool_w, pool_scale, g_ffn, w_r, b_r)


def _row_copy(src, src_row, dst, dst_row, sem):
    return pltpu.make_async_copy(
        src.at[pl.ds(pl.multiple_of(src_row * CHUNKS, CHUNKS), CHUNKS), :],
        dst.at[pl.ds(pl.multiple_of(dst_row * CHUNKS, CHUNKS), CHUNKS), :],
        sem)


def _dispatch_kernel(pos0_ref, pos1_ref, src_ref, dst_ref, sem):
    unroll = 8

    def issue(jo, carry):
        for u in range(unroll):
            j = jo * unroll + u
            _row_copy(src_ref, j, dst_ref, pos0_ref[j], sem).start(priority=0)
            _row_copy(src_ref, j, dst_ref, pos1_ref[j], sem).start(priority=1)
        return carry

    lax.fori_loop(0, DISPATCH_TILE // unroll, issue, 0)

    def drain(jo, carry):
        for u in range(2 * unroll):
            _row_copy(src_ref, 0, dst_ref, 0, sem).wait()
        return carry

    lax.fori_loop(0, DISPATCH_TILE // unroll, drain, 0)


def _dispatch(pos0, pos1, xn_slab):
    smem = lambda: pl.BlockSpec((DISPATCH_TILE,), lambda i: (i,), memory_space=pltpu.SMEM)
    return pl.pallas_call(
        _dispatch_kernel,
        grid=(N_TOK // DISPATCH_TILE,),
        in_specs=[smem(), smem(), pl.BlockSpec((DISPATCH_TILE * CHUNKS, LANES), lambda i: (i, 0))],
        out_specs=pl.BlockSpec(memory_space=pl.ANY),
        out_shape=jax.ShapeDtypeStruct((N_ASSIGN * CHUNKS, LANES), F32),
        scratch_shapes=[pltpu.SemaphoreType.DMA(())],
        compiler_params=pltpu.CompilerParams(dimension_semantics=("arbitrary",), has_side_effects=True),
        name="dispatch",
    )(pos0, pos1, xn_slab)


def _experts_kernel(vt_ref, ve_ref, gs_ref, ge_ref, nv_ref, xs_ref, wg_ref, wu_ref, wd_ref, ys_ref,
                    wgu_s, wd_s):
    i = pl.program_id(0)
    prev = jnp.maximum(i - 1, 0)
    e = ve_ref[i]
    tile = vt_ref[i]
    valid = i < nv_ref[0]
    new_expert = (i == 0) | (e != ve_ref[prev])
    new_tile = (i == 0) | (tile != vt_ref[prev])

    @pl.when(valid & new_expert)
    def _():
        wgu_s[:, :D_EXPERT] = wg_ref[...].astype(BF16)
        wgu_s[:, D_EXPERT:] = wu_ref[...].astype(BF16)
        wd_s[...] = wd_ref[...].astype(BF16)

    @pl.when(valid)
    def _():
        x = _load_slab(xs_ref, EXPERT_TILE).astype(BF16)
        gu = jnp.dot(x, wgu_s[...], preferred_element_type=F32)
        gate = gu[:, :D_EXPERT]
        act = (gate / (1.0 + jnp.exp(-gate)) * gu[:, D_EXPERT:]).astype(BF16)
        y = jnp.dot(act, wd_s[...], preferred_element_type=F32)
        rows = tile * EXPERT_TILE + lax.broadcasted_iota(jnp.int32, (EXPERT_TILE, 1), 0)
        mine = (rows >= gs_ref[e]) & (rows < ge_ref[e])

        @pl.when(new_tile)
        def _():
            _store_slab(ys_ref, jnp.where(mine, y, 0.0))

        @pl.when(jnp.logical_not(new_tile))
        def _():
            _store_slab(ys_ref, jnp.where(mine, y, _load_slab(ys_ref, EXPERT_TILE)))


def _experts(layer, visit_tile, visit_expert, g_start, g_end, n_visits, xs_slab, w_gate, w_up, w_down):
    rows = EXPERT_TILE * CHUNKS
    grid_spec = pltpu.PrefetchScalarGridSpec(
        num_scalar_prefetch=5,
        grid=(N_VISITS,),
        in_specs=[
            pl.BlockSpec((rows, LANES), lambda i, vt, ve, gs, ge, nv: (vt[i], 0)),
            pl.BlockSpec((None, None, D_MODEL, D_EXPERT), lambda i, vt, ve, gs, ge, nv: (layer, ve[i], 0, 0)),
            pl.BlockSpec((None, None, D_MODEL, D_EXPERT), lambda i, vt, ve, gs, ge, nv: (layer, ve[i], 0, 0)),
            pl.BlockSpec((None, None, D_EXPERT, D_MODEL), lambda i, vt, ve, gs, ge, nv: (layer, ve[i], 0, 0)),
        ],
        out_specs=pl.BlockSpec((rows, LANES), lambda i, vt, ve, gs, ge, nv: (vt[i], 0)),
        scratch_shapes=[pltpu.VMEM((D_MODEL, 2 * D_EXPERT), BF16), pltpu.VMEM((D_EXPERT, D_MODEL), BF16)],
    )
    return pl.pallas_call(
        _experts_kernel,
        grid_spec=grid_spec,
        out_shape=jax.ShapeDtypeStruct((N_ASSIGN * CHUNKS, LANES), F32),
        compiler_params=pltpu.CompilerParams(dimension_semantics=("arbitrary",), vmem_limit_bytes=48 << 20),
        name="experts",
    )(visit_tile, visit_expert, g_start, g_end, n_visits, xs_slab, w_gate, w_up, w_down)


def _with_expert_rows(pos0_ref, pos1_ref, npos0_ref, npos1_ref, ys_ref, bufs, sem, body):
    i = pl.program_id(0)

    def copies(p0_ref, p1_ref, s):
        for j in range(TOK_TILE):
            yield _row_copy(ys_ref, p0_ref[j], bufs[s][0], j, sem.at[s])
            yield _row_copy(ys_ref, p1_ref[j], bufs[s][1], j, sem.at[s])

    @pl.when(i == 0)
    def _():
        for c in copies(pos0_ref, pos1_ref, 0):
            c.start()

    for s in range(2):
        @pl.when(i % 2 == s)
        def _():
            for c in copies(npos0_ref, npos1_ref, 1 - s):
                c.start()
            for c in copies(pos0_ref, pos1_ref, s):
                c.wait()
            body(_load_slab(bufs[s][0], TOK_TILE), _load_slab(bufs[s][1], TOK_TILE))

            @pl.when(i == pl.num_programs(0) - 1)
            def _():
                for c in copies(npos0_ref, npos1_ref, 1 - s):
                    c.wait()


def _moe_residual_and_ple(h_ref, mf_ref, y0, y1, p_ref, gple_ref, wpg_ref, wpu_ref):
    gates = mf_ref[...]
    h = h_ref[...] + (y0 * gates[:, 0:1] + y1 * gates[:, 1:2])
    hn = _rms(h, gple_ref[...]).astype(BF16)
    gate = jnp.dot(hn, wpg_ref[...], preferred_element_type=F32)
    gate = 1.0 / (1.0 + jnp.exp(-gate))
    up = jnp.dot(p_ref[...].astype(BF16), wpu_ref[...], preferred_element_type=F32)
    return h + gate * up


def _rope(x, cos, sin_signed):
    n, width = x.shape
    reps = width // LANES
    cos_t = jnp.concatenate([cos] * reps, axis=1)
    sin_t = jnp.concatenate([sin_signed] * reps, axis=1)
    lane = lax.broadcasted_iota(jnp.int32, (n, width), 1)
    first_half = (lane % HEAD_DIM) < (HEAD_DIM // 2)
    partner = jnp.where(first_half, pltpu.roll(x, width - HEAD_DIM // 2, axis=1), pltpu.roll(x, HEAD_DIM // 2, axis=1))
    return x * cos_t + partner * sin_t


def _combine0_kernel(pos0_ref, pos1_ref, npos0_ref, npos1_ref, h_ref, mf_ref, p_ref, cos_ref, sin_ref, gple_ref, wpg_ref, wpu_ref,
                     gkv_ref, wk_ref, wv_ref, gq_ref, wq_ref, ys_ref,
                     ho_ref, k_ref, v_ref, q_ref, b00, b01, b10, b11, sem):
    def body(y0, y1):
        h = _moe_residual_and_ple(h_ref, mf_ref, y0, y1, p_ref, gple_ref, wpg_ref, wpu_ref)
        ho_ref[...] = h
        cos = cos_ref[...]
        sin = sin_ref[...]
        kvn = _rms(h, gkv_ref[...]).astype(BF16)
        k = jnp.dot(kvn, wk_ref[...], preferred_element_type=F32)
        k_ref[...] = _rope(k, cos, sin).astype(BF16)
        v_ref[...] = jnp.dot(kvn, wv_ref[...], preferred_element_type=F32).astype(BF16)
        qn = _rms(h, gq_ref[...]).astype(BF16)
        q = jnp.dot(qn, wq_ref[...], preferred_element_type=F32)
        q_ref[...] = (_rope(q, cos, sin) * (HEAD_DIM ** -0.5)).astype(BF16)

    _with_expert_rows(pos0_ref, pos1_ref, npos0_ref, npos1_ref, ys_ref, ((b00, b01), (b10, b11)), sem, body)


def _combine1_kernel(pos0_ref, pos1_ref, npos0_ref, npos1_ref, h_ref, mf_ref, p_ref, gple_ref, wpg_ref, wpu_ref, gfin_ref, ys_ref,
                     out_ref, b00, b01, b10, b11, sem):
    def body(y0, y1):
        h = _moe_residual_and_ple(h_ref, mf_ref, y0, y1, p_ref, gple_ref, wpg_ref, wpu_ref)
        out_ref[...] = _rms(h, gfin_ref[...])

    _with_expert_rows(pos0_ref, pos1_ref, npos0_ref, npos1_ref, ys_ref, ((b00, b01), (b10, b11)), sem, body)


def _combine_specs():
    last = N_TOK // TOK_TILE - 1
    smem = lambda: pl.BlockSpec((TOK_TILE,), lambda i: (i,), memory_space=pltpu.SMEM)
    smem_next = lambda: pl.BlockSpec((TOK_TILE,), lambda i: (jnp.minimum(i + 1, last),), memory_space=pltpu.SMEM)
    tok = lambda w: pl.BlockSpec((TOK_TILE, w), lambda i: (i, 0))
    full = lambda a, b: pl.BlockSpec((a, b), lambda i: (0, 0))
    scratch = [pltpu.VMEM((TOK_TILE * CHUNKS, LANES), F32) for _ in range(4)] + [pltpu.SemaphoreType.DMA((2,))]
    ple = lambda layer: pl.BlockSpec((None, TOK_TILE, PLE_DIM), lambda i: (layer, i, 0))
    return smem, smem_next, tok, ple, full, scratch


def _combine0(pos0, pos1, h, mf, p, cos_t, sin_t, g_ple, w_pg, w_pu, g_kv, w_k, w_v, g_q, w_q, ys_slab):
    smem, smem_next, tok, ple, full, scratch = _combine_specs()
    return pl.pallas_call(
        _combine0_kernel,
        grid=(N_TOK // TOK_TILE,),
        in_specs=[smem(), smem(), smem_next(), smem_next(), tok(D_MODEL), tok(LANES), ple(0), tok(LANES), tok(LANES),
                  full(1, D_MODEL), full(D_MODEL, D_MODEL), full(PLE_DIM, D_MODEL),
                  full(1, D_MODEL), full(D_MODEL, KV_DIM), full(D_MODEL, KV_DIM),
                  full(1, D_MODEL), full(D_MODEL, D_MODEL),
                  pl.BlockSpec(memory_space=pl.ANY)],
        out_specs=[tok(D_MODEL), tok(KV_DIM), tok(KV_DIM), tok(D_MODEL)],
        out_shape=[jax.ShapeDtypeStruct((N_TOK, D_MODEL), F32),
                   jax.ShapeDtypeStruct((N_TOK, KV_DIM), BF16),
                   jax.ShapeDtypeStruct((N_TOK, KV_DIM), BF16),
                   jax.ShapeDtypeStruct((N_TOK, D_MODEL), BF16)],
        scratch_shapes=scratch,
        compiler_params=pltpu.CompilerParams(dimension_semantics=("arbitrary",), vmem_limit_bytes=56 << 20),
        name="combine0",
    )(pos0, pos1, pos0, pos1, h, mf, p, cos_t, sin_t, g_ple, w_pg, w_pu, g_kv, w_k, w_v, g_q, w_q, ys_slab)


def _combine1(pos0, pos1, h, mf, p, g_ple, w_pg, w_pu, g_fin, ys_slab):
    smem, smem_next, tok, ple, full, scratch = _combine_specs()
    return pl.pallas_call(
        _combine1_kernel,
        grid=(N_TOK // TOK_TILE,),
        in_specs=[smem(), smem(), smem_next(), smem_next(), tok(D_MODEL), tok(LANES), ple(1),
                  full(1, D_MODEL), full(D_MODEL, D_MODEL), full(PLE_DIM, D_MODEL), full(1, D_MODEL),
                  pl.BlockSpec(memory_space=pl.ANY)],
        out_specs=tok(D_MODEL),
        out_shape=jax.ShapeDtypeStruct((N_TOK, D_MODEL), F32),
        scratch_shapes=scratch,
        compiler_params=pltpu.CompilerParams(dimension_semantics=("arbitrary",), vmem_limit_bytes=48 << 20),
        name="combine1",
    )(pos0, pos1, pos0, pos1, h, mf, p, g_ple, w_pg, w_pu, g_fin, ys_slab)


def _attn1_kernel(sink_ref, q_ref, kc_ref, kp_ref, vc_ref, vp_ref, h_ref, wo_ref, gffn_ref, wr_ref, br_ref,
                  ho_ref, xn_ref, mi_ref, mf_ref, cnt_ref, run_ref):
    i = pl.program_id(0)
    blocks_per_seq = SEQ // ATT_BLOCK

    @pl.when(i == 0)
    def _():
        run_ref[...] = jnp.zeros_like(run_ref)

    has_prev = (i % blocks_per_seq) != 0
    kcat = jnp.concatenate([kp_ref[...], kc_ref[...]], axis=0)
    vcat = jnp.concatenate([vp_ref[...], vc_ref[...]], axis=0)
    q = q_ref[...]
    qs = jnp.concatenate([q[:, g * KV_DIM:(g + 1) * KV_DIM] for g in range(Q_PER_KV)], axis=0)
    n_rows = Q_PER_KV * ATT_BLOCK
    row = lax.broadcasted_iota(jnp.int32, (n_rows, 2 * ATT_BLOCK), 0)
    col = lax.broadcasted_iota(jnp.int32, (n_rows, 2 * ATT_BLOCK), 1)
    rel = (row % ATT_BLOCK) + ATT_BLOCK - col
    keep = (rel >= 0) & (rel < WINDOW) & (has_prev | (col >= ATT_BLOCK))
    head_of_lane = lax.broadcasted_iota(jnp.int32, (2 * ATT_BLOCK, KV_DIM), 1) // HEAD_DIM
    grp_of_row = lax.broadcasted_iota(jnp.int32, (n_rows, 1), 0) // ATT_BLOCK
    o = jnp.zeros((n_rows, KV_DIM), F32)
    for kh in range(N_KV_HEADS):
        k_h = jnp.where(head_of_lane == kh, kcat, jnp.zeros_like(kcat))
        v_h = jnp.where(head_of_lane == kh, vcat, jnp.zeros_like(vcat))
        s = lax.dot_general(qs, k_h, (((1,), (1,)), ((), ())), preferred_element_type=F32)
        s = jnp.where(keep, s, NEG_INF)
        sink = jnp.zeros((n_rows, 1), F32)
        for g in range(Q_PER_KV):
            sink = jnp.where(grp_of_row == g, sink_ref[kh * Q_PER_KV + g], sink)
        m = jnp.maximum(jnp.max(s, axis=1, keepdims=True), sink)
        e = jnp.exp(s - m)
        denom = jnp.sum(e, axis=1, keepdims=True) + jnp.exp(sink - m)
        pr = (e / denom).astype(BF16)
        o = o + jnp.dot(pr, v_h, preferred_element_type=F32)
    o_cat = jnp.concatenate([o[g * ATT_BLOCK:(g + 1) * ATT_BLOCK, :] for g in range(Q_PER_KV)], axis=1)
    h = h_ref[...] + jnp.dot(o_cat.astype(BF16), wo_ref[...], preferred_element_type=F32)
    ho_ref[...] = h
    xn = _rms(h, gffn_ref[...])
    _store_slab(xn_ref, xn)
    _route_and_rank(xn, wr_ref, br_ref, run_ref, mi_ref, mf_ref, cnt_ref)


def _attn1(sinks, q, k, v, h, w_o, g_ffn, w_r, b_r):
    blocks_per_seq = SEQ // ATT_BLOCK
    cur = lambda w: pl.BlockSpec((ATT_BLOCK, w), lambda i: (i, 0))
    prev = lambda w: pl.BlockSpec((ATT_BLOCK, w), lambda i: (jnp.maximum(i - 1, 0), 0))
    full = lambda a, b: pl.BlockSpec((a, b), lambda i: (0, 0))
    return pl.pallas_call(
        _attn1_kernel,
        grid=(N_TOK // ATT_BLOCK,),
        in_specs=[pl.BlockSpec(memory_space=pltpu.SMEM),
                  cur(D_MODEL), cur(KV_DIM), prev(KV_DIM), cur(KV_DIM), prev(KV_DIM), cur(D_MODEL),
                  full(D_MODEL, D_MODEL), full(1, D_MODEL), full(D_MODEL, 2 * LANES), full(1, LANES)],
        out_specs=[cur(D_MODEL),
                   pl.BlockSpec((ATT_BLOCK * CHUNKS, LANES), lambda i: (i, 0)),
                   pl.BlockSpec((SUBLANES, ATT_BLOCK), lambda i: (0, i)), cur(LANES),
                   pl.BlockSpec((SUBLANES, LANES), lambda i: (0, 0))],
        out_shape=[jax.ShapeDtypeStruct((N_TOK, D_MODEL), F32),
                   jax.ShapeDtypeStruct((N_TOK * CHUNKS, LANES), F32),
                   jax.ShapeDtypeStruct((SUBLANES, N_TOK), jnp.int32),
                   jax.ShapeDtypeStruct((N_TOK, LANES), F32),
                   jax.ShapeDtypeStruct((SUBLANES, LANES), F32)],
        scratch_shapes=[pltpu.VMEM((SUBLANES, LANES), F32)],
        compiler_params=pltpu.CompilerParams(dimension_semantics=("arbitrary",), vmem_limit_bytes=48 << 20),
        name="attn1",
    )(sinks, q, k, k, v, v, h, w_o, g_ffn, w_r, b_r)


def _plan_kernel(cnt_ref, meta_ref, pos_ref, vt_ref, ve_ref, gs_ref, ge_ref, nv_ref):
    def per_expert(e, carry):
        off, v = carry
        c = cnt_ref[e]
        gs_ref[e] = off
        ge_ref[e] = off + c
        first = off // EXPERT_TILE
        n_vis = jnp.where(c > 0, (off + c - 1) // EXPERT_TILE - first + 1, 0)

        def visit(j, _):
            vt_ref[v + j] = first + j
            ve_ref[v + j] = e
            return 0

        lax.fori_loop(0, n_vis, visit, 0)
        return off + c, v + n_vis

    _, total = lax.fori_loop(0, N_EXPERTS, per_expert, (jnp.int32(0), jnp.int32(0)))
    nv_ref[0] = total

    def pad(j, _):
        vt_ref[j] = vt_ref[total - 1]
        ve_ref[j] = ve_ref[total - 1]
        return 0

    lax.fori_loop(total, N_VISITS, pad, 0)

    meta = meta_ref[...]
    start = jnp.zeros_like(meta)
    for e in range(N_EXPERTS):
        start = jnp.where(meta == e, gs_ref[e], start)
    pos_ref[...] = start + pltpu.roll(meta, SUBLANES - 2, axis=0)


def _routing_tables(meta_i, counts_f):
    counts = counts_f[0, :N_EXPERTS].astype(jnp.int32)
    smem = lambda: pl.BlockSpec(memory_space=pltpu.SMEM)
    i32 = lambda n: jax.ShapeDtypeStruct((n,), jnp.int32)
    pos, vt, ve, gs, ge, nv = pl.pallas_call(
        _plan_kernel,
        in_specs=[smem(), pl.BlockSpec(memory_space=pltpu.VMEM)],
        out_specs=[pl.BlockSpec(memory_space=pltpu.VMEM), smem(), smem(), smem(), smem(), smem()],
        out_shape=[jax.ShapeDtypeStruct((SUBLANES, N_TOK), jnp.int32),
                   i32(N_VISITS), i32(N_VISITS), i32(N_EXPERTS), i32(N_EXPERTS), i32(1)],
        name="plan",
    )(counts, meta_i)
    return pos[0], pos[1], vt, ve, gs, ge, nv


def _router_weights(w_grp, b_grp, w_exp, b_exp):
    w = jnp.zeros((D_MODEL, LANES), F32)
    w = w.at[:, :N_EXPERTS].set(w_exp).at[:, GROUP_LANE0:GROUP_LANE0 + N_GROUPS].set(w_grp)
    b = jnp.zeros((1, LANES), F32)
    b = b.at[0, :N_EXPERTS].set(b_exp).at[0, GROUP_LANE0:GROUP_LANE0 + N_GROUPS].set(b_grp)
    w_hi = w.astype(BF16)
    w_lo = (w - w_hi.astype(F32)).astype(BF16)
    return jnp.concatenate([w_hi, w_lo], axis=1), b


def _rope_tables(positions):
    inv = ROPE_THETA ** (-jnp.arange(0, HEAD_DIM, 2, dtype=F32) / HEAD_DIM)
    ang = positions.astype(F32).reshape(N_TOK, 1) * inv
    cos = jnp.cos(ang)
    sin = jnp.sin(ang)
    cos_t = jnp.concatenate([cos, cos, cos, cos], axis=1)
    sin_t = jnp.concatenate([-sin, sin, -sin, sin], axis=1)
    return cos_t, sin_t


def _moe(layer, meta_i, counts_f, xn_slab, w_gate, w_up, w_down):
    pos0, pos1, vt, ve, gs, ge, nv = _routing_tables(meta_i, counts_f)
    xs_slab = _dispatch(pos0, pos1, xn_slab)
    ys_slab = _experts(layer, vt, ve, gs, ge, nv, xs_slab, w_gate, w_up, w_down)
    return pos0, pos1, ys_slab


def kernel(x, p, positions, g_mix, g_ffn, pool_w, pool_scale, g_kv, w_k, w_v, w_q, w_o, sinks, w_group_router, b_group_router, w_expert_router, b_expert_router, w_exp_gate, w_exp_up, w_exp_down, g_ple, w_ple_gate, w_ple_up, g_final):
    row = lambda a: a.reshape(1, -1)
    x2 = x.reshape(N_TOK, D_MODEL)
    p2 = p.reshape(2, N_TOK, PLE_DIM)
    cos_t, sin_t = _rope_tables(positions)
    w_q_perm = w_q[0].reshape(D_MODEL, N_KV_HEADS, Q_PER_KV, HEAD_DIM).transpose(0, 2, 1, 3).reshape(D_MODEL, D_MODEL)
    w_o_perm = w_o[0].reshape(N_KV_HEADS, Q_PER_KV, HEAD_DIM, D_MODEL).transpose(1, 0, 2, 3).reshape(D_MODEL, D_MODEL)

    wr0, br0 = _router_weights(w_group_router[0], b_group_router[0], w_expert_router[0], b_expert_router[0])
    h1, xn_slab, meta_i, meta_f, counts = _mixer0(
        x2, row(g_mix[0]), pool_w[0].astype(BF16), row(pool_scale[0]), row(g_ffn[0]), wr0, br0)
    pos0, pos1, ys_slab = _moe(0, meta_i, counts, xn_slab, w_exp_gate, w_exp_up, w_exp_down)
    h3, k, v, q = _combine0(
        pos0, pos1, h1, meta_f, p2, cos_t, sin_t, row(g_ple[0]), w_ple_gate[0].astype(BF16),
        w_ple_up[0].astype(BF16), row(g_kv), w_k.astype(BF16), w_v.astype(BF16), row(g_mix[1]),
        w_q_perm.astype(BF16), ys_slab)

    wr1, br1 = _router_weights(w_group_router[1], b_group_router[1], w_expert_router[1], b_expert_router[1])
    h4, xn_slab, meta_i, meta_f, counts = _attn1(
        sinks[0], q, k, v, h3, w_o_perm.astype(BF16), row(g_ffn[1]), wr1, br1)
    pos0, pos1, ys_slab = _moe(1, meta_i, counts, xn_slab, w_exp_gate, w_exp_up, w_exp_down)
    out = _combine1(pos0, pos1, h4, meta_f, p2, row(g_ple[1]), w_ple_gate[1].astype(BF16),
                    w_ple_up[1].astype(BF16), row(g_final), ys_slab)
    return out.reshape(BATCH, SEQ, D_MODEL)
```

```python
import functools

import jax
import jax.numpy as jnp
from jax import lax
from jax.experimental import pallas as pl
from jax.experimental.pallas import tpu as pltpu

D_MODEL = 1024
BATCH = 2
SEQ = 8192
N_TOK = BATCH * SEQ
POOL_WINDOWS = (2, 4, 8, 16)
POOL_GROUP = D_MODEL // len(POOL_WINDOWS)
POOL_HALO = 16
HEAD_DIM = 64
N_Q_HEADS = 16
N_KV_HEADS = 4
Q_PER_KV = 4
KV_DIM = N_KV_HEADS * HEAD_DIM
WINDOW = 128
ATT_BLOCK = 128
ROPE_THETA = 10000.0
N_GROUPS = 4
EXPERTS_PER_GROUP = 8
N_EXPERTS = 32
D_EXPERT = 512
PLE_DIM = 256
EPS = 1e-6

LANES = 128
SUBLANES = 8
CHUNKS = D_MODEL // LANES
N_ASSIGN = 2 * N_TOK

TOK_TILE = 256
MIX_TILE = 512
ATT_TILE = 512
DISPATCH_TILE = 512
EXPERT_TILE = 256
N_ROW_TILES = N_ASSIGN // EXPERT_TILE
N_VISITS = N_ROW_TILES + N_EXPERTS - 1
GROUP_LANE0 = N_EXPERTS

F32 = jnp.float32
BF16 = jnp.bfloat16
NEG_INF = float("-inf")


def _rms(x, g):
    return x * lax.rsqrt(jnp.mean(x * x, axis=-1, keepdims=True) + EPS) * g


def _store_slab(ref, val):
    n = val.shape[0]
    for c in range(CHUNKS):
        ref[pl.ds(c, n, stride=CHUNKS), :] = val[:, c * LANES:(c + 1) * LANES]


def _load_slab(ref, n):
    return jnp.concatenate([ref[pl.ds(c, n, stride=CHUNKS), :] for c in range(CHUNKS)], axis=1)


def _route_and_rank(xn, wr_ref, br_ref, run_ref, mi_ref, mf_ref, cnt_ref):
    n = xn.shape[0]
    x_hi = xn.astype(BF16)
    x_lo = (xn - x_hi.astype(F32)).astype(BF16)
    hh_hl = jnp.dot(x_hi, wr_ref[...], preferred_element_type=F32)
    lh = jnp.dot(x_lo, wr_ref[:, :LANES], preferred_element_type=F32)
    logits = hh_hl[:, :LANES] + (hh_hl[:, LANES:] + lh) + br_ref[...]
    lane = lax.broadcasted_iota(jnp.int32, (n, LANES), 1).astype(F32)
    big = float(LANES)
    is_g = (lane >= GROUP_LANE0) & (lane < GROUP_LANE0 + N_GROUPS)
    gl = jnp.where(is_g, logits, NEG_INF)
    gmax = jnp.max(gl, axis=1, keepdims=True)
    gidx = jnp.min(jnp.where(gl == gmax, lane, big), axis=1, keepdims=True) - GROUP_LANE0
    pg = 1.0 / jnp.sum(jnp.exp(gl - gmax), axis=1, keepdims=True)
    lo = gidx * EXPERTS_PER_GROUP
    el = jnp.where((lane >= lo) & (lane < lo + EXPERTS_PER_GROUP), logits, NEG_INF)
    m1 = jnp.max(el, axis=1, keepdims=True)
    i1 = jnp.min(jnp.where(el == m1, lane, big), axis=1, keepdims=True)
    el2 = jnp.where(lane == i1, NEG_INF, el)
    m2 = jnp.max(el2, axis=1, keepdims=True)
    i2 = jnp.min(jnp.where(el2 == m2, lane, big), axis=1, keepdims=True)
    z = jnp.exp(m2 - m1)
    gate0 = pg / (1.0 + z)
    gate1 = pg * z / (1.0 + z)

    oh0 = jnp.where(lane == i1, 1.0, 0.0)
    oh1 = jnp.where(lane == i2, 1.0, 0.0)
    oh = oh0 + oh1
    r = lax.broadcasted_iota(jnp.int32, (n, n), 0)
    c = lax.broadcasted_iota(jnp.int32, (n, n), 1)
    lower = jnp.where(r > c, 1.0, 0.0).astype(BF16)
    prefix = jnp.dot(lower, oh.astype(BF16), preferred_element_type=F32)
    tot = prefix + run_ref[0:1, :]
    rank0 = jnp.sum(oh0 * tot, axis=1, keepdims=True)
    rank1 = jnp.sum(oh1 * tot, axis=1, keepdims=True)
    new_run = run_ref[0:1, :] + jnp.sum(oh, axis=0, keepdims=True)
    run_ref[0:1, :] = new_run
    cnt_ref[...] = jnp.broadcast_to(new_run, cnt_ref.shape)

    meta = jnp.where(lane == 0.0, i1, jnp.where(lane == 1.0, i2,
                     jnp.where(lane == 2.0, rank0, jnp.where(lane == 3.0, rank1, 0.0))))
    mi_ref[...] = meta.T[:SUBLANES, :].astype(jnp.int32)
    mf_ref[...] = jnp.where(lane == 0.0, gate0, jnp.where(lane == 1.0, gate1, 0.0))


def _mixer0_kernel(x_ref, halo_ref, gmix_ref, poolw_ref, pscale_ref, gffn_ref, wr_ref, br_ref,
                   h_ref, xn_ref, mi_ref, mf_ref, cnt_ref, run_ref):
    i = pl.program_id(0)
    tiles_per_seq = SEQ // MIX_TILE

    @pl.when(i == 0)
    def _():
        run_ref[...] = jnp.zeros_like(run_ref)

    seq_tile = i % tiles_per_seq
    x = x_ref[...]
    g = gmix_ref[...]
    hn = _rms(x, g)
    halo = jnp.where(seq_tile == 0, 0.0, _rms(halo_ref[...], g))
    full = jnp.concatenate([halo, hn], axis=0)
    t = (seq_tile * MIX_TILE + lax.broadcasted_iota(jnp.int32, (MIX_TILE, 1), 0) + 1).astype(F32)
    outs = []
    for gi, w in enumerate(POOL_WINDOWS):
        acc = full[:, gi * POOL_GROUP:(gi + 1) * POOL_GROUP]
        shift = 1
        while shift < w:
            acc = acc + pltpu.roll(acc, shift, axis=0)
            shift *= 2
        win = acc[POOL_HALO:, :]
        mean = win / jnp.minimum(t, float(w))
        dlt = (mean - hn[:, gi * POOL_GROUP:(gi + 1) * POOL_GROUP]).astype(BF16)
        outs.append(jnp.dot(dlt, poolw_ref[gi], preferred_element_type=F32))
    y = jnp.concatenate(outs, axis=1)
    h = x + y * pscale_ref[...]
    h_ref[...] = h
    xn = _rms(h, gffn_ref[...])
    _store_slab(xn_ref, xn)
    _route_and_rank(xn, wr_ref, br_ref, run_ref, mi_ref, mf_ref, cnt_ref)


def _mixer0(x, g_mix, pool_w, pool_scale, g_ffn, w_r, b_r):
    n_tiles = N_TOK // MIX_TILE
    halo_per_tile = MIX_TILE // POOL_HALO
    vec = lambda: pl.BlockSpec((1, D_MODEL), lambda i: (0, 0))
    return pl.pallas_call(
        _mixer0_kernel,
        grid=(n_tiles,),
        in_specs=[
            pl.BlockSpec((MIX_TILE, D_MODEL), lambda i: (i, 0)),
            pl.BlockSpec((POOL_HALO, D_MODEL), lambda i: (jnp.maximum(i * halo_per_tile - 1, 0), 0)),
            vec(),
            pl.BlockSpec((len(POOL_WINDOWS), POOL_GROUP, POOL_GROUP), lambda i: (0, 0, 0)),
            vec(), vec(),
            pl.BlockSpec((D_MODEL, 2 * LANES), lambda i: (0, 0)),
            pl.BlockSpec((1, LANES), lambda i: (0, 0)),
        ],
        out_specs=[
            pl.BlockSpec((MIX_TILE, D_MODEL), lambda i: (i, 0)),
            pl.BlockSpec((MIX_TILE * CHUNKS, LANES), lambda i: (i, 0)),
            pl.BlockSpec((SUBLANES, MIX_TILE), lambda i: (0, i)),
            pl.BlockSpec((MIX_TILE, LANES), lambda i: (i, 0)),
            pl.BlockSpec((SUBLANES, LANES), lambda i: (0, 0)),
        ],
        out_shape=[
            jax.ShapeDtypeStruct((N_TOK, D_MODEL), F32),
            jax.ShapeDtypeStruct((N_TOK * CHUNKS, LANES), F32),
            jax.ShapeDtypeStruct((SUBLANES, N_TOK), jnp.int32),
            jax.ShapeDtypeStruct((N_TOK, LANES), F32),
            jax.ShapeDtypeStruct((SUBLANES, LANES), F32),
        ],
        scratch_shapes=[pltpu.VMEM((SUBLANES, LANES), F32)],
        compiler_params=pltpu.CompilerParams(dimension_semantics=("arbitrary",), vmem_limit_bytes=48 << 20),
        name="mixer0",
    )(x, x, g_mix, pool_w, pool_scale, g_ffn, w_r, b_r)


def _row_copy(src, src_row, dst, dst_row, sem):
    return pltpu.make_async_copy(
        src.at[pl.ds(pl.multiple_of(src_row * CHUNKS, CHUNKS), CHUNKS), :],
        dst.at[pl.ds(pl.multiple_of(dst_row * CHUNKS, CHUNKS), CHUNKS), :],
        sem)


def _dispatch_kernel(pos0_ref, pos1_ref, src_ref, dst_ref, sem):
    unroll = 8

    def issue(jo, carry):
        for u in range(unroll):
            j = jo * unroll + u
            _row_copy(src_ref, j, dst_ref, pos0_ref[j], sem).start(priority=0)
            _row_copy(src_ref, j, dst_ref, pos1_ref[j], sem).start(priority=1)
        return carry

    lax.fori_loop(0, DISPATCH_TILE // unroll, issue, 0)

    def drain(jo, carry):
        for u in range(2 * unroll):
            _row_copy(src_ref, 0, dst_ref, 0, sem).wait()
        return carry

    lax.fori_loop(0, DISPATCH_TILE // unroll, drain, 0)


def _dispatch(pos0, pos1, xn_slab):
    smem = lambda: pl.BlockSpec((DISPATCH_TILE,), lambda i: (i,), memory_space=pltpu.SMEM)
    return pl.pallas_call(
        _dispatch_kernel,
        grid=(N_TOK // DISPATCH_TILE,),
        in_specs=[smem(), smem(), pl.BlockSpec((DISPATCH_TILE * CHUNKS, LANES), lambda i: (i, 0))],
        out_specs=pl.BlockSpec(memory_space=pl.ANY),
        out_shape=jax.ShapeDtypeStruct((N_ASSIGN * CHUNKS, LANES), F32),
        scratch_shapes=[pltpu.SemaphoreType.DMA(())],
        compiler_params=pltpu.CompilerParams(dimension_semantics=("arbitrary",), has_side_effects=True),
        name="dispatch",
    )(pos0, pos1, xn_slab)


def _experts_kernel(vt_ref, ve_ref, gs_ref, ge_ref, nv_ref, xs_ref, wg_ref, wu_ref, wd_ref, ys_ref,
                    wgu_s, wd_s):
    i = pl.program_id(0)
    prev = jnp.maximum(i - 1, 0)
    e = ve_ref[i]
    tile = vt_ref[i]
    valid = i < nv_ref[0]
    new_expert = (i == 0) | (e != ve_ref[prev])
    new_tile = (i == 0) | (tile != vt_ref[prev])

    @pl.when(valid & new_expert)
    def _():
        wgu_s[:, :D_EXPERT] = wg_ref[...].astype(BF16)
        wgu_s[:, D_EXPERT:] = wu_ref[...].astype(BF16)
        wd_s[...] = wd_ref[...].astype(BF16)

    @pl.when(valid)
    def _():
        x = _load_slab(xs_ref, EXPERT_TILE).astype(BF16)
        gu = jnp.dot(x, wgu_s[...], preferred_element_type=F32)
        gate = gu[:, :D_EXPERT]
        act = (gate / (1.0 + jnp.exp(-gate)) * gu[:, D_EXPERT:]).astype(BF16)
        y = jnp.dot(act, wd_s[...], preferred_element_type=F32)
        rows = tile * EXPERT_TILE + lax.broadcasted_iota(jnp.int32, (EXPERT_TILE, 1), 0)
        mine = (rows >= gs_ref[e]) & (rows < ge_ref[e])

        @pl.when(new_tile)
        def _():
            _store_slab(ys_ref, jnp.where(mine, y, 0.0))

        @pl.when(jnp.logical_not(new_tile))
        def _():
            _store_slab(ys_ref, jnp.where(mine, y, _load_slab(ys_ref, EXPERT_TILE)))


def _experts(layer, visit_tile, visit_expert, g_start, g_end, n_visits, xs_slab, w_gate, w_up, w_down):
    rows = EXPERT_TILE * CHUNKS
    grid_spec = pltpu.PrefetchScalarGridSpec(
        num_scalar_prefetch=5,
        grid=(N_VISITS,),
        in_specs=[
            pl.BlockSpec((rows, LANES), lambda i, vt, ve, gs, ge, nv: (vt[i], 0)),
            pl.BlockSpec((None, None, D_MODEL, D_EXPERT), lambda i, vt, ve, gs, ge, nv: (layer, ve[i], 0, 0)),
            pl.BlockSpec((None, None, D_MODEL, D_EXPERT), lambda i, vt, ve, gs, ge, nv: (layer, ve[i], 0, 0)),
            pl.BlockSpec((None, None, D_EXPERT, D_MODEL), lambda i, vt, ve, gs, ge, nv: (layer, ve[i], 0, 0)),
        ],
        out_specs=pl.BlockSpec((rows, LANES), lambda i, vt, ve, gs, ge, nv: (vt[i], 0)),
        scratch_shapes=[pltpu.VMEM((D_MODEL, 2 * D_EXPERT), BF16), pltpu.VMEM((D_EXPERT, D_MODEL), BF16)],
    )
    return pl.pallas_call(
        _experts_kernel,
        grid_spec=grid_spec,
        out_shape=jax.ShapeDtypeStruct((N_ASSIGN * CHUNKS, LANES), F32),
        compiler_params=pltpu.CompilerParams(dimension_semantics=("arbitrary",), vmem_limit_bytes=48 << 20),
        name="experts",
    )(visit_tile, visit_expert, g_start, g_end, n_visits, xs_slab, w_gate, w_up, w_down)


def _with_expert_rows(pos0_ref, pos1_ref, npos0_ref, npos1_ref, ys_ref, bufs, sem, body):
    i = pl.program_id(0)

    def copies(p0_ref, p1_ref, s):
        for j in range(TOK_TILE):
            yield _row_copy(ys_ref, p0_ref[j], bufs[s][0], j, sem.at[s])
            yield _row_copy(ys_ref, p1_ref[j], bufs[s][1], j, sem.at[s])

    @pl.when(i == 0)
    def _():
        for c in copies(pos0_ref, pos1_ref, 0):
            c.start()

    for s in range(2):
        @pl.when(i % 2 == s)
        def _():
            for c in copies(npos0_ref, npos1_ref, 1 - s):
                c.start()
            for c in copies(pos0_ref, pos1_ref, s):
                c.wait()
            body(_load_slab(bufs[s][0], TOK_TILE), _load_slab(bufs[s][1], TOK_TILE))

            @pl.when(i == pl.num_programs(0) - 1)
            def _():
                for c in copies(npos0_ref, npos1_ref, 1 - s):
                    c.wait()


def _moe_residual_and_ple(h_ref, mf_ref, y0, y1, p_ref, gple_ref, wpg_ref, wpu_ref):
    gates = mf_ref[...]
    h = h_ref[...] + (y0 * gates[:, 0:1] + y1 * gates[:, 1:2])
    hn = _rms(h, gple_ref[...]).astype(BF16)
    gate = jnp.dot(hn, wpg_ref[...], preferred_element_type=F32)
    gate = 1.0 / (1.0 + jnp.exp(-gate))
    up = jnp.dot(p_ref[...].astype(BF16), wpu_ref[...], preferred_element_type=F32)
    return h + gate * up


def _rope(x, cos, sin_signed):
    n, width = x.shape
    reps = width // LANES
    cos_t = jnp.concatenate([cos] * reps, axis=1)
    sin_t = jnp.concatenate([sin_signed] * reps, axis=1)
    lane = lax.broadcasted_iota(jnp.int32, (n, width), 1)
    first_half = (lane % HEAD_DIM) < (HEAD_DIM // 2)
    partner = jnp.where(first_half, pltpu.roll(x, width - HEAD_DIM // 2, axis=1), pltpu.roll(x, HEAD_DIM // 2, axis=1))
    return x * cos_t + partner * sin_t


def _combine0_kernel(pos0_ref, pos1_ref, npos0_ref, npos1_ref, h_ref, mf_ref, p_ref, cos_ref, sin_ref, gple_ref, wpg_ref, wpu_ref,
                     gkv_ref, wk_ref, wv_ref, gq_ref, wq_ref, ys_ref,
                     ho_ref, k_ref, v_ref, q_ref, b00, b01, b10, b11, sem):
    def body(y0, y1):
        h = _moe_residual_and_ple(h_ref, mf_ref, y0, y1, p_ref, gple_ref, wpg_ref, wpu_ref)
        ho_ref[...] = h
        cos = cos_ref[...]
        sin = sin_ref[...]
        kvn = _rms(h, gkv_ref[...]).astype(BF16)
        k = jnp.dot(kvn, wk_ref[...], preferred_element_type=F32)
        k_ref[...] = _rope(k, cos, sin).astype(BF16)
        v_ref[...] = jnp.dot(kvn, wv_ref[...], preferred_element_type=F32).astype(BF16)
        qn = _rms(h, gq_ref[...]).astype(BF16)
        q = jnp.dot(qn, wq_ref[...], preferred_element_type=F32)
        q_ref[...] = (_rope(q, cos, sin) * (HEAD_DIM ** -0.5)).astype(BF16)

    _with_expert_rows(pos0_ref, pos1_ref, npos0_ref, npos1_ref, ys_ref, ((b00, b01), (b10, b11)), sem, body)


def _combine1_kernel(pos0_ref, pos1_ref, npos0_ref, npos1_ref, h_ref, mf_ref, p_ref, gple_ref, wpg_ref, wpu_ref, gfin_ref, ys_ref,
                     out_ref, b00, b01, b10, b11, sem):
    def body(y0, y1):
        h = _moe_residual_and_ple(h_ref, mf_ref, y0, y1, p_ref, gple_ref, wpg_ref, wpu_ref)
        out_ref[...] = _rms(h, gfin_ref[...])

    _with_expert_rows(pos0_ref, pos1_ref, npos0_ref, npos1_ref, ys_ref, ((b00, b01), (b10, b11)), sem, body)


def _combine_specs():
    last = N_TOK // TOK_TILE - 1
    smem = lambda: pl.BlockSpec((TOK_TILE,), lambda i: (i,), memory_space=pltpu.SMEM)
    smem_next = lambda: pl.BlockSpec((TOK_TILE,), lambda i: (jnp.minimum(i + 1, last),), memory_space=pltpu.SMEM)
    tok = lambda w: pl.BlockSpec((TOK_TILE, w), lambda i: (i, 0))
    full = lambda a, b: pl.BlockSpec((a, b), lambda i: (0, 0))
    scratch = [pltpu.VMEM((TOK_TILE * CHUNKS, LANES), F32) for _ in range(4)] + [pltpu.SemaphoreType.DMA((2,))]
    ple = lambda layer: pl.BlockSpec((None, TOK_TILE, PLE_DIM), lambda i: (layer, i, 0))
    return smem, smem_next, tok, ple, full, scratch


def _combine0(pos0, pos1, h, mf, p, cos_t, sin_t, g_ple, w_pg, w_pu, g_kv, w_k, w_v, g_q, w_q, ys_slab):
    smem, smem_next, tok, ple, full, scratch = _combine_specs()
    return pl.pallas_call(
        _combine0_kernel,
        grid=(N_TOK // TOK_TILE,),
        in_specs=[smem(), smem(), smem_next(), smem_next(), tok(D_MODEL), tok(LANES), ple(0), tok(LANES), tok(LANES),
                  full(1, D_MODEL), full(D_MODEL, D_MODEL), full(PLE_DIM, D_MODEL),
                  full(1, D_MODEL), full(D_MODEL, KV_DIM), full(D_MODEL, KV_DIM),
                  full(1, D_MODEL), full(D_MODEL, D_MODEL),
                  pl.BlockSpec(memory_space=pl.ANY)],
        out_specs=[tok(D_MODEL), tok(KV_DIM), tok(KV_DIM), tok(D_MODEL)],
        out_shape=[jax.ShapeDtypeStruct((N_TOK, D_MODEL), F32),
                   jax.ShapeDtypeStruct((N_TOK, KV_DIM), BF16),
                   jax.ShapeDtypeStruct((N_TOK, KV_DIM), BF16),
                   jax.ShapeDtypeStruct((N_TOK, D_MODEL), BF16)],
        scratch_shapes=scratch,
        compiler_params=pltpu.CompilerParams(dimension_semantics=("arbitrary",), vmem_limit_bytes=56 << 20),
        name="combine0",
    )(pos0, pos1, pos0, pos1, h, mf, p, cos_t, sin_t, g_ple, w_pg, w_pu, g_kv, w_k, w_v, g_q, w_q, ys_slab)


def _combine1(pos0, pos1, h, mf, p, g_ple, w_pg, w_pu, g_fin, ys_slab):
    smem, smem_next, tok, ple, full, scratch = _combine_specs()
    return pl.pallas_call(
        _combine1_kernel,
        grid=(N_TOK // TOK_TILE,),
        in_specs=[smem(), smem(), smem_next(), smem_next(), tok(D_MODEL), tok(LANES), ple(1),
                  full(1, D_MODEL), full(D_MODEL, D_MODEL), full(PLE_DIM, D_MODEL), full(1, D_MODEL),
                  pl.BlockSpec(memory_space=pl.ANY)],
        out_specs=tok(D_MODEL),
        out_shape=jax.ShapeDtypeStruct((N_TOK, D_MODEL), F32),
        scratch_shapes=scratch,
        compiler_params=pltpu.CompilerParams(dimension_semantics=("arbitrary",), vmem_limit_bytes=48 << 20),
        name="combine1",
    )(pos0, pos1, pos0, pos1, h, mf, p, g_ple, w_pg, w_pu, g_fin, ys_slab)


def _attn1_kernel(sink_ref, q_ref, kc_ref, kp_ref, vc_ref, vp_ref, h_ref, wo_ref, gffn_ref, wr_ref, br_ref,
                  ho_ref, xn_ref, mi_ref, mf_ref, cnt_ref, run_ref, k_s, v_s, o_s):
    i = pl.program_id(0)
    steps_per_seq = SEQ // ATT_TILE

    @pl.when(i == 0)
    def _():
        run_ref[...] = jnp.zeros_like(run_ref)

    k_s[:ATT_BLOCK, :] = kp_ref[...]
    k_s[ATT_BLOCK:, :] = kc_ref[...]
    v_s[:ATT_BLOCK, :] = vp_ref[...]
    v_s[ATT_BLOCK:, :] = vc_ref[...]
    seq_start = (i % steps_per_seq) == 0

    n_rows = Q_PER_KV * ATT_BLOCK
    row = lax.broadcasted_iota(jnp.int32, (n_rows, 2 * ATT_BLOCK), 0)
    col = lax.broadcasted_iota(jnp.int32, (n_rows, 2 * ATT_BLOCK), 1)
    rel = (row % ATT_BLOCK) + ATT_BLOCK - col
    band = (rel >= 0) & (rel < WINDOW)
    head_of_lane = lax.broadcasted_iota(jnp.int32, (2 * ATT_BLOCK, KV_DIM), 1) // HEAD_DIM
    grp_of_row = lax.broadcasted_iota(jnp.int32, (n_rows, 1), 0) // ATT_BLOCK

    def block(b, carry):
        r0 = pl.multiple_of(b * ATT_BLOCK, ATT_BLOCK)
        has_prev = jnp.logical_not(seq_start & (b == 0))
        keep = band & (has_prev | (col >= ATT_BLOCK))
        kcat = k_s[pl.ds(r0, 2 * ATT_BLOCK), :]
        vcat = v_s[pl.ds(r0, 2 * ATT_BLOCK), :]
        q = q_ref[pl.ds(r0, ATT_BLOCK), :]
        qs = jnp.concatenate([q[:, g * KV_DIM:(g + 1) * KV_DIM] for g in range(Q_PER_KV)], axis=0)
        o = jnp.zeros((n_rows, KV_DIM), F32)
        for kh in range(N_KV_HEADS):
            k_h = jnp.where(head_of_lane == kh, kcat, jnp.zeros_like(kcat))
            v_h = jnp.where(head_of_lane == kh, vcat, jnp.zeros_like(vcat))
            s = lax.dot_general(qs, k_h, (((1,), (1,)), ((), ())), preferred_element_type=F32)
            s = jnp.where(keep, s, NEG_INF)
            sink = jnp.zeros((n_rows, 1), F32)
            for g in range(Q_PER_KV):
                sink = jnp.where(grp_of_row == g, sink_ref[kh * Q_PER_KV + g], sink)
            m = jnp.maximum(jnp.max(s, axis=1, keepdims=True), sink)
            e = jnp.exp(s - m)
            denom = jnp.sum(e, axis=1, keepdims=True) + jnp.exp(sink - m)
            pr = (e / denom).astype(BF16)
            o = o + jnp.dot(pr, v_h, preferred_element_type=F32)
        o_cat = jnp.concatenate([o[g * ATT_BLOCK:(g + 1) * ATT_BLOCK, :] for g in range(Q_PER_KV)], axis=1)
        o_s[pl.ds(r0, ATT_BLOCK), :] = o_cat.astype(BF16)
        return carry

    lax.fori_loop(0, ATT_TILE // ATT_BLOCK, block, 0)
    h = h_ref[...] + jnp.dot(o_s[...], wo_ref[...], preferred_element_type=F32)
    ho_ref[...] = h
    xn = _rms(h, gffn_ref[...])
    _store_slab(xn_ref, xn)
    _route_and_rank(xn, wr_ref, br_ref, run_ref, mi_ref, mf_ref, cnt_ref)


def _attn1(sinks, q, k, v, h, w_o, g_ffn, w_r, b_r):
    blocks_per_step = ATT_TILE // ATT_BLOCK
    cur = lambda w: pl.BlockSpec((ATT_TILE, w), lambda i: (i, 0))
    prev = lambda w: pl.BlockSpec((ATT_BLOCK, w), lambda i: (jnp.maximum(i * blocks_per_step - 1, 0), 0))
    full = lambda a, b: pl.BlockSpec((a, b), lambda i: (0, 0))
    return pl.pallas_call(
        _attn1_kernel,
        grid=(N_TOK // ATT_TILE,),
        in_specs=[pl.BlockSpec(memory_space=pltpu.SMEM),
                  cur(D_MODEL), cur(KV_DIM), prev(KV_DIM), cur(KV_DIM), prev(KV_DIM), cur(D_MODEL),
                  full(D_MODEL, D_MODEL), full(1, D_MODEL), full(D_MODEL, 2 * LANES), full(1, LANES)],
        out_specs=[cur(D_MODEL),
                   pl.BlockSpec((ATT_TILE * CHUNKS, LANES), lambda i: (i, 0)),
                   pl.BlockSpec((SUBLANES, ATT_TILE), lambda i: (0, i)), cur(LANES),
                   pl.BlockSpec((SUBLANES, LANES), lambda i: (0, 0))],
        out_shape=[jax.ShapeDtypeStruct((N_TOK, D_MODEL), F32),
                   jax.ShapeDtypeStruct((N_TOK * CHUNKS, LANES), F32),
                   jax.ShapeDtypeStruct((SUBLANES, N_TOK), jnp.int32),
                   jax.ShapeDtypeStruct((N_TOK, LANES), F32),
                   jax.ShapeDtypeStruct((SUBLANES, LANES), F32)],
        scratch_shapes=[pltpu.VMEM((SUBLANES, LANES), F32),
                        pltpu.VMEM((ATT_TILE + ATT_BLOCK, KV_DIM), BF16),
                        pltpu.VMEM((ATT_TILE + ATT_BLOCK, KV_DIM), BF16),
                        pltpu.VMEM((ATT_TILE, D_MODEL), BF16)],
        compiler_params=pltpu.CompilerParams(dimension_semantics=("arbitrary",), vmem_limit_bytes=48 << 20),
        name="attn1",
    )(sinks, q, k, k, v, v, h, w_o, g_ffn, w_r, b_r)


def _plan_kernel(cnt_ref, meta_ref, pos_ref, vt_ref, ve_ref, gs_ref, ge_ref, nv_ref):
    def per_expert(e, carry):
        off, v = carry
        c = cnt_ref[e]
        gs_ref[e] = off
        ge_ref[e] = off + c
        first = off // EXPERT_TILE
        n_vis = jnp.where(c > 0, (off + c - 1) // EXPERT_TILE - first + 1, 0)

        def visit(j, _):
            vt_ref[v + j] = first + j
            ve_ref[v + j] = e
            return 0

        lax.fori_loop(0, n_vis, visit, 0)
        return off + c, v + n_vis

    _, total = lax.fori_loop(0, N_EXPERTS, per_expert, (jnp.int32(0), jnp.int32(0)))
    nv_ref[0] = total

    def pad(j, _):
        vt_ref[j] = vt_ref[total - 1]
        ve_ref[j] = ve_ref[total - 1]
        return 0

    lax.fori_loop(total, N_VISITS, pad, 0)

    meta = meta_ref[...]
    start = jnp.zeros_like(meta)
    for e in range(N_EXPERTS):
        start = jnp.where(meta == e, gs_ref[e], start)
    pos_ref[...] = start + pltpu.roll(meta, SUBLANES - 2, axis=0)


def _routing_tables(meta_i, counts_f):
    counts = counts_f[0, :N_EXPERTS].astype(jnp.int32)
    smem = lambda: pl.BlockSpec(memory_space=pltpu.SMEM)
    i32 = lambda n: jax.ShapeDtypeStruct((n,), jnp.int32)
    pos, vt, ve, gs, ge, nv = pl.pallas_call(
        _plan_kernel,
        in_specs=[smem(), pl.BlockSpec(memory_space=pltpu.VMEM)],
        out_specs=[pl.BlockSpec(memory_space=pltpu.VMEM), smem(), smem(), smem(), smem(), smem()],
        out_shape=[jax.ShapeDtypeStruct((SUBLANES, N_TOK), jnp.int32),
                   i32(N_VISITS), i32(N_VISITS), i32(N_EXPERTS), i32(N_EXPERTS), i32(1)],
        name="plan",
    )(counts, meta_i)
    return pos[0], pos[1], vt, ve, gs, ge, nv


def _router_weights(w_grp, b_grp, w_exp, b_exp):
    w = jnp.zeros((D_MODEL, LANES), F32)
    w = w.at[:, :N_EXPERTS].set(w_exp).at[:, GROUP_LANE0:GROUP_LANE0 + N_GROUPS].set(w_grp)
    b = jnp.zeros((1, LANES), F32)
    b = b.at[0, :N_EXPERTS].set(b_exp).at[0, GROUP_LANE0:GROUP_LANE0 + N_GROUPS].set(b_grp)
    w_hi = w.astype(BF16)
    w_lo = (w - w_hi.astype(F32)).astype(BF16)
    return jnp.concatenate([w_hi, w_lo], axis=1), b


def _rope_tables(positions):
    inv = ROPE_THETA ** (-jnp.arange(0, HEAD_DIM, 2, dtype=F32) / HEAD_DIM)
    ang = positions.astype(F32).reshape(N_TOK, 1) * inv
    cos = jnp.cos(ang)
    sin = jnp.sin(ang)
    cos_t = jnp.concatenate([cos, cos, cos, cos], axis=1)
    sin_t = jnp.concatenate([-sin, sin, -sin, sin], axis=1)
    return cos_t, sin_t


def _moe(layer, meta_i, counts_f, xn_slab, w_gate, w_up, w_down):
    pos0, pos1, vt, ve, gs, ge, nv = _routing_tables(meta_i, counts_f)
    xs_slab = _dispatch(pos0, pos1, xn_slab)
    ys_slab = _experts(layer, vt, ve, gs, ge, nv, xs_slab, w_gate, w_up, w_down)
    return pos0, pos1, ys_slab


def kernel(x, p, positions, g_mix, g_ffn, pool_w, pool_scale, g_kv, w_k, w_v, w_q, w_o, sinks, w_group_router, b_group_router, w_expert_router, b_expert_router, w_exp_gate, w_exp_up, w_exp_down, g_ple, w_ple_gate, w_ple_up, g_final):
    row = lambda a: a.reshape(1, -1)
    x2 = x.reshape(N_TOK, D_MODEL)
    p2 = p.reshape(2, N_TOK, PLE_DIM)
    cos_t, sin_t = _rope_tables(positions)
    w_q_perm = w_q[0].reshape(D_MODEL, N_KV_HEADS, Q_PER_KV, HEAD_DIM).transpose(0, 2, 1, 3).reshape(D_MODEL, D_MODEL)
    w_o_perm = w_o[0].reshape(N_KV_HEADS, Q_PER_KV, HEAD_DIM, D_MODEL).transpose(1, 0, 2, 3).reshape(D_MODEL, D_MODEL)

    wr0, br0 = _router_weights(w_group_router[0], b_group_router[0], w_expert_router[0], b_expert_router[0])
    h1, xn_slab, meta_i, meta_f, counts = _mixer0(
        x2, row(g_mix[0]), pool_w[0].astype(BF16), row(pool_scale[0]), row(g_ffn[0]), wr0, br0)
    pos0, pos1, ys_slab = _moe(0, meta_i, counts, xn_slab, w_exp_gate, w_exp_up, w_exp_down)
    h3, k, v, q = _combine0(
        pos0, pos1, h1, meta_f, p2, cos_t, sin_t, row(g_ple[0]), w_ple_gate[0].astype(BF16),
        w_ple_up[0].astype(BF16), row(g_kv), w_k.astype(BF16), w_v.astype(BF16), row(g_mix[1]),
        w_q_perm.astype(BF16), ys_slab)

    wr1, br1 = _router_weights(w_group_router[1], b_group_router[1], w_expert_router[1], b_expert_router[1])
    h4, xn_slab, meta_i, meta_f, counts = _attn1(
        sinks[0], q, k, v, h3, w_o_perm.astype(BF16), row(g_ffn[1]), wr1, br1)
    pos0, pos1, ys_slab = _moe(1, meta_i, counts, xn_slab, w_exp_gate, w_exp_up, w_exp_down)
    out = _combine1(pos0, pos1, h4, meta_f, p2, row(g_ple[1]), w_ple_gate[1].astype(BF16),
                    w_ple_up[1].astype(BF16), row(g_final), ys_slab)
    return out.reshape(BATCH, SEQ, D_MODEL)
```

```python
import functools

import jax
import jax.numpy as jnp
from jax import lax
from jax.experimental import pallas as pl
from jax.experimental.pallas import tpu as pltpu

D_MODEL = 1024
BATCH = 2
SEQ = 8192
N_TOK = BATCH * SEQ
POOL_WINDOWS = (2, 4, 8, 16)
POOL_GROUP = D_MODEL // len(POOL_WINDOWS)
POOL_HALO = 16
HEAD_DIM = 64
N_Q_HEADS = 16
N_KV_HEADS = 4
Q_PER_KV = 4
KV_DIM = N_KV_HEADS * HEAD_DIM
WINDOW = 128
ATT_BLOCK = 128
ROPE_THETA = 10000.0
N_GROUPS = 4
EXPERTS_PER_GROUP = 8
N_EXPERTS = 32
D_EXPERT = 512
PLE_DIM = 256
EPS = 1e-6

LANES = 128
SUBLANES = 8
CHUNKS = D_MODEL // LANES
N_ASSIGN = 2 * N_TOK

TOK_TILE = 256
MIX_TILE = 512
ATT_TILE = 512
DISPATCH_TILE = 512
EXPERT_TILE = 256
N_ROW_TILES = N_ASSIGN // EXPERT_TILE
N_VISITS = N_ROW_TILES + N_EXPERTS - 1
GROUP_LANE0 = N_EXPERTS

F32 = jnp.float32
BF16 = jnp.bfloat16
NEG_INF = float("-inf")


def _rms(x, g):
    return x * lax.rsqrt(jnp.mean(x * x, axis=-1, keepdims=True) + EPS) * g


def _store_slab(ref, val):
    n = val.shape[0]
    for c in range(CHUNKS):
        ref[pl.ds(c, n, stride=CHUNKS), :] = val[:, c * LANES:(c + 1) * LANES]


def _load_slab(ref, n):
    return jnp.concatenate([ref[pl.ds(c, n, stride=CHUNKS), :] for c in range(CHUNKS)], axis=1)


def _route_and_rank(xn, wr_ref, br_ref, run_ref, mi_ref, mf_ref, cnt_ref):
    n = xn.shape[0]
    x_hi = xn.astype(BF16)
    x_lo = (xn - x_hi.astype(F32)).astype(BF16)
    hh_hl = jnp.dot(x_hi, wr_ref[...], preferred_element_type=F32)
    lh = jnp.dot(x_lo, wr_ref[:, :LANES], preferred_element_type=F32)
    logits = hh_hl[:, :LANES] + (hh_hl[:, LANES:] + lh) + br_ref[...]
    lane = lax.broadcasted_iota(jnp.int32, (n, LANES), 1).astype(F32)
    big = float(LANES)
    is_g = (lane >= GROUP_LANE0) & (lane < GROUP_LANE0 + N_GROUPS)
    gl = jnp.where(is_g, logits, NEG_INF)
    gmax = jnp.max(gl, axis=1, keepdims=True)
    gidx = jnp.min(jnp.where(gl == gmax, lane, big), axis=1, keepdims=True) - GROUP_LANE0
    pg = 1.0 / jnp.sum(jnp.exp(gl - gmax), axis=1, keepdims=True)
    lo = gidx * EXPERTS_PER_GROUP
    el = jnp.where((lane >= lo) & (lane < lo + EXPERTS_PER_GROUP), logits, NEG_INF)
    m1 = jnp.max(el, axis=1, keepdims=True)
    i1 = jnp.min(jnp.where(el == m1, lane, big), axis=1, keepdims=True)
    el2 = jnp.where(lane == i1, NEG_INF, el)
    m2 = jnp.max(el2, axis=1, keepdims=True)
    i2 = jnp.min(jnp.where(el2 == m2, lane, big), axis=1, keepdims=True)
    z = jnp.exp(m2 - m1)
    gate0 = pg / (1.0 + z)
    gate1 = pg * z / (1.0 + z)

    oh0 = jnp.where(lane == i1, 1.0, 0.0)
    oh1 = jnp.where(lane == i2, 1.0, 0.0)
    oh = oh0 + oh1
    r = lax.broadcasted_iota(jnp.int32, (n, n), 0)
    c = lax.broadcasted_iota(jnp.int32, (n, n), 1)
    lower = jnp.where(r > c, 1.0, 0.0).astype(BF16)
    prefix = jnp.dot(lower, oh.astype(BF16), preferred_element_type=F32)
    tot = prefix + run_ref[0:1, :]
    rank0 = jnp.sum(oh0 * tot, axis=1, keepdims=True)
    rank1 = jnp.sum(oh1 * tot, axis=1, keepdims=True)
    new_run = run_ref[0:1, :] + jnp.sum(oh, axis=0, keepdims=True)
    run_ref[0:1, :] = new_run
    cnt_ref[...] = jnp.broadcast_to(new_run, cnt_ref.shape)

    meta = jnp.where(lane == 0.0, i1, jnp.where(lane == 1.0, i2,
                     jnp.where(lane == 2.0, rank0, jnp.where(lane == 3.0, rank1, 0.0))))
    mi_ref[...] = meta.T[:SUBLANES, :].astype(jnp.int32)
    mf_ref[...] = jnp.where(lane == 0.0, gate0, jnp.where(lane == 1.0, gate1, 0.0))


def _mixer0_kernel(x_ref, halo_ref, gmix_ref, poolw_ref, pscale_ref, gffn_ref, wr_ref, br_ref,
                   h_ref, xn_ref, mi_ref, mf_ref, cnt_ref, run_ref):
    i = pl.program_id(0)
    tiles_per_seq = SEQ // MIX_TILE

    @pl.when(i == 0)
    def _():
        run_ref[...] = jnp.zeros_like(run_ref)

    seq_tile = i % tiles_per_seq
    x = x_ref[...]
    g = gmix_ref[...]
    hn = _rms(x, g)
    halo = jnp.where(seq_tile == 0, 0.0, _rms(halo_ref[...], g))
    full = jnp.concatenate([halo, hn], axis=0)
    t = (seq_tile * MIX_TILE + lax.broadcasted_iota(jnp.int32, (MIX_TILE, 1), 0) + 1).astype(F32)
    outs = []
    for gi, w in enumerate(POOL_WINDOWS):
        acc = full[:, gi * POOL_GROUP:(gi + 1) * POOL_GROUP]
        shift = 1
        while shift < w:
            acc = acc + pltpu.roll(acc, shift, axis=0)
            shift *= 2
        win = acc[POOL_HALO:, :]
        mean = win / jnp.minimum(t, float(w))
        dlt = (mean - hn[:, gi * POOL_GROUP:(gi + 1) * POOL_GROUP]).astype(BF16)
        outs.append(jnp.dot(dlt, poolw_ref[gi], preferred_element_type=F32))
    y = jnp.concatenate(outs, axis=1)
    h = x + y * pscale_ref[...]
    h_ref[...] = h
    xn = _rms(h, gffn_ref[...])
    _store_slab(xn_ref, xn)
    _route_and_rank(xn, wr_ref, br_ref, run_ref, mi_ref, mf_ref, cnt_ref)


def _mixer0(x, g_mix, pool_w, pool_scale, g_ffn, w_r, b_r):
    n_tiles = N_TOK // MIX_TILE
    halo_per_tile = MIX_TILE // POOL_HALO
    vec = lambda: pl.BlockSpec((1, D_MODEL), lambda i: (0, 0))
    return pl.pallas_call(
        _mixer0_kernel,
        grid=(n_tiles,),
        in_specs=[
            pl.BlockSpec((MIX_TILE, D_MODEL), lambda i: (i, 0)),
            pl.BlockSpec((POOL_HALO, D_MODEL), lambda i: (jnp.maximum(i * halo_per_tile - 1, 0), 0)),
            vec(),
            pl.BlockSpec((len(POOL_WINDOWS), POOL_GROUP, POOL_GROUP), lambda i: (0, 0, 0)),
            vec(), vec(),
            pl.BlockSpec((D_MODEL, 2 * LANES), lambda i: (0, 0)),
            pl.BlockSpec((1, LANES), lambda i: (0, 0)),
        ],
        out_specs=[
            pl.BlockSpec((MIX_TILE, D_MODEL), lambda i: (i, 0)),
            pl.BlockSpec((MIX_TILE * CHUNKS, LANES), lambda i: (i, 0)),
            pl.BlockSpec((SUBLANES, MIX_TILE), lambda i: (0, i)),
            pl.BlockSpec((MIX_TILE, LANES), lambda i: (i, 0)),
            pl.BlockSpec((SUBLANES, LANES), lambda i: (0, 0)),
        ],
        out_shape=[
            jax.ShapeDtypeStruct((N_TOK, D_MODEL), F32),
            jax.ShapeDtypeStruct((N_TOK * CHUNKS, LANES), F32),
            jax.ShapeDtypeStruct((SUBLANES, N_TOK), jnp.int32),
            jax.ShapeDtypeStruct((N_TOK, LANES), F32),
            jax.ShapeDtypeStruct((SUBLANES, LANES), F32),
        ],
        scratch_shapes=[pltpu.VMEM((SUBLANES, LANES), F32)],
        compiler_params=pltpu.CompilerParams(dimension_semantics=("arbitrary",), vmem_limit_bytes=48 << 20),
        name="mixer0",
    )(x, x, g_mix, pool_w, pool_scale, g_ffn, w_r, b_r)


def _row_copy(src, src_row, dst, dst_row, sem):
    return pltpu.make_async_copy(
        src.at[pl.ds(pl.multiple_of(src_row * CHUNKS, CHUNKS), CHUNKS), :],
        dst.at[pl.ds(pl.multiple_of(dst_row * CHUNKS, CHUNKS), CHUNKS), :],
        sem)


def _dispatch_kernel(pos0_ref, pos1_ref, src_ref, dst_ref, sem):
    unroll = 8

    def issue(jo, carry):
        for u in range(unroll):
            j = jo * unroll + u
            _row_copy(src_ref, j, dst_ref, pos0_ref[j], sem).start(priority=0)
            _row_copy(src_ref, j, dst_ref, pos1_ref[j], sem).start(priority=1)
        return carry

    lax.fori_loop(0, DISPATCH_TILE // unroll, issue, 0)

    def drain(jo, carry):
        for u in range(2 * unroll):
            _row_copy(src_ref, 0, dst_ref, 0, sem).wait()
        return carry

    lax.fori_loop(0, DISPATCH_TILE // unroll, drain, 0)


def _dispatch(pos0, pos1, xn_slab):
    smem = lambda: pl.BlockSpec((DISPATCH_TILE,), lambda i: (i,), memory_space=pltpu.SMEM)
    return pl.pallas_call(
        _dispatch_kernel,
        grid=(N_TOK // DISPATCH_TILE,),
        in_specs=[smem(), smem(), pl.BlockSpec((DISPATCH_TILE * CHUNKS, LANES), lambda i: (i, 0))],
        out_specs=pl.BlockSpec(memory_space=pl.ANY),
        out_shape=jax.ShapeDtypeStruct((N_ASSIGN * CHUNKS, LANES), F32),
        scratch_shapes=[pltpu.SemaphoreType.DMA(())],
        compiler_params=pltpu.CompilerParams(dimension_semantics=("arbitrary",), has_side_effects=True),
        name="dispatch",
    )(pos0, pos1, xn_slab)


X_SLOTS = 3
Y_SLOTS = 2


def _experts_kernel(layer, vt_ref, ve_ref, gs_ref, ge_ref, nv_ref, nx_ref, xs_hbm, wg_hbm, wu_hbm, wd_hbm, ys_hbm,
                    xb, yb, wg_st, wu_st, wd_st, wgu_s, wd_s, xsem, ysem, wsem):
    i = pl.program_id(0)
    rows = EXPERT_TILE * CHUNKS
    n_visits = nv_ref[0]
    prev = jnp.maximum(i - 1, 0)
    e = ve_ref[i]
    tile = vt_ref[i]
    valid = i < n_visits
    new_expert = (i == 0) | (e != ve_ref[prev])
    new_tile = (i == 0) | (tile != vt_ref[prev])

    def x_copy(t):
        return pltpu.make_async_copy(xs_hbm.at[pl.ds(pl.multiple_of(t * rows, rows), rows), :],
                                     xb.at[t % X_SLOTS], xsem.at[t % X_SLOTS])

    def y_copy(t):
        return pltpu.make_async_copy(yb.at[t % Y_SLOTS],
                                     ys_hbm.at[pl.ds(pl.multiple_of(t * rows, rows), rows), :], ysem.at[t % Y_SLOTS])

    def w_copies(ex):
        return (pltpu.make_async_copy(wg_hbm.at[layer, ex], wg_st, wsem.at[0]),
                pltpu.make_async_copy(wu_hbm.at[layer, ex], wu_st, wsem.at[1]),
                pltpu.make_async_copy(wd_hbm.at[layer, ex], wd_st, wsem.at[2]))

    @pl.when(i == 0)
    def _():
        x_copy(0).start()
        x_copy(1).start()
        for c in w_copies(e):
            c.start()

    @pl.when(valid & new_tile)
    def _():
        x_copy(tile).wait()

        @pl.when(tile + 2 < N_ROW_TILES)
        def _():
            x_copy(tile + 2).start()

        @pl.when(tile >= 2)
        def _():
            y_copy(tile - 2).wait()

        @pl.when(tile >= 1)
        def _():
            y_copy(tile - 1).start()

    @pl.when(valid & new_expert)
    def _():
        for c in w_copies(e):
            c.wait()
        wgu_s[:, :D_EXPERT] = wg_st[...].astype(BF16)
        wgu_s[:, D_EXPERT:] = wu_st[...].astype(BF16)
        wd_s[...] = wd_st[...].astype(BF16)
        nxt = nx_ref[e]

        @pl.when(nxt != e)
        def _():
            for c in w_copies(nxt):
                c.start()

    @pl.when(valid)
    def _():
        xs = tile % X_SLOTS
        ys = tile % Y_SLOTS
        x = jnp.concatenate([xb[xs, pl.ds(c, EXPERT_TILE, stride=CHUNKS), :] for c in range(CHUNKS)], axis=1)
        gu = jnp.dot(x.astype(BF16), wgu_s[...], preferred_element_type=F32)
        gate = gu[:, :D_EXPERT]
        act = (gate / (1.0 + jnp.exp(-gate)) * gu[:, D_EXPERT:]).astype(BF16)
        y = jnp.dot(act, wd_s[...], preferred_element_type=F32)
        r = tile * EXPERT_TILE + lax.broadcasted_iota(jnp.int32, (EXPERT_TILE, 1), 0)
        mine = (r >= gs_ref[e]) & (r < ge_ref[e])

        @pl.when(new_tile)
        def _():
            for c in range(CHUNKS):
                yb[ys, pl.ds(c, EXPERT_TILE, stride=CHUNKS), :] = jnp.where(mine, y[:, c * LANES:(c + 1) * LANES], 0.0)

        @pl.when(jnp.logical_not(new_tile))
        def _():
            for c in range(CHUNKS):
                old = yb[ys, pl.ds(c, EXPERT_TILE, stride=CHUNKS), :]
                yb[ys, pl.ds(c, EXPERT_TILE, stride=CHUNKS), :] = jnp.where(mine, y[:, c * LANES:(c + 1) * LANES], old)

        @pl.when(i == n_visits - 1)
        def _():
            y_copy(tile).start()
            y_copy(tile).wait()

            @pl.when(tile >= 1)
            def _():
                y_copy(tile - 1).wait()


def _experts(layer, visit_tile, visit_expert, g_start, g_end, n_visits, next_expert, xs_slab, w_gate, w_up, w_down):
    rows = EXPERT_TILE * CHUNKS
    any_spec = lambda: pl.BlockSpec(memory_space=pl.ANY)
    grid_spec = pltpu.PrefetchScalarGridSpec(
        num_scalar_prefetch=6,
        grid=(N_VISITS,),
        in_specs=[any_spec(), any_spec(), any_spec(), any_spec()],
        out_specs=any_spec(),
        scratch_shapes=[pltpu.VMEM((X_SLOTS, rows, LANES), F32), pltpu.VMEM((Y_SLOTS, rows, LANES), F32),
                        pltpu.VMEM((D_MODEL, D_EXPERT), F32), pltpu.VMEM((D_MODEL, D_EXPERT), F32),
                        pltpu.VMEM((D_EXPERT, D_MODEL), F32),
                        pltpu.VMEM((D_MODEL, 2 * D_EXPERT), BF16), pltpu.VMEM((D_EXPERT, D_MODEL), BF16),
                        pltpu.SemaphoreType.DMA((X_SLOTS,)), pltpu.SemaphoreType.DMA((Y_SLOTS,)),
                        pltpu.SemaphoreType.DMA((3,))],
    )
    return pl.pallas_call(
        functools.partial(_experts_kernel, layer),
        grid_spec=grid_spec,
        out_shape=jax.ShapeDtypeStruct((N_ASSIGN * CHUNKS, LANES), F32),
        compiler_params=pltpu.CompilerParams(dimension_semantics=("arbitrary",), vmem_limit_bytes=48 << 20,
                                             has_side_effects=True),
        name="experts",
    )(visit_tile, visit_expert, g_start, g_end, n_visits, next_expert, xs_slab, w_gate, w_up, w_down)


def _with_expert_rows(pos0_ref, pos1_ref, npos0_ref, npos1_ref, ys_ref, bufs, sem, body):
    i = pl.program_id(0)

    def copies(p0_ref, p1_ref, s):
        for j in range(TOK_TILE):
            yield _row_copy(ys_ref, p0_ref[j], bufs[s][0], j, sem.at[s])
            yield _row_copy(ys_ref, p1_ref[j], bufs[s][1], j, sem.at[s])

    @pl.when(i == 0)
    def _():
        for c in copies(pos0_ref, pos1_ref, 0):
            c.start()

    for s in range(2):
        @pl.when(i % 2 == s)
        def _():
            for c in copies(npos0_ref, npos1_ref, 1 - s):
                c.start()
            for c in copies(pos0_ref, pos1_ref, s):
                c.wait()
            body(_load_slab(bufs[s][0], TOK_TILE), _load_slab(bufs[s][1], TOK_TILE))

            @pl.when(i == pl.num_programs(0) - 1)
            def _():
                for c in copies(npos0_ref, npos1_ref, 1 - s):
                    c.wait()


def _moe_residual_and_ple(h_ref, mf_ref, y0, y1, p_ref, gple_ref, wpg_ref, wpu_ref):
    gates = mf_ref[...]
    h = h_ref[...] + (y0 * gates[:, 0:1] + y1 * gates[:, 1:2])
    hn = _rms(h, gple_ref[...]).astype(BF16)
    gate = jnp.dot(hn, wpg_ref[...], preferred_element_type=F32)
    gate = 1.0 / (1.0 + jnp.exp(-gate))
    up = jnp.dot(p_ref[...].astype(BF16), wpu_ref[...], preferred_element_type=F32)
    return h + gate * up


def _rope(x, cos, sin_signed):
    n, width = x.shape
    reps = width // LANES
    cos_t = jnp.concatenate([cos] * reps, axis=1)
    sin_t = jnp.concatenate([sin_signed] * reps, axis=1)
    lane = lax.broadcasted_iota(jnp.int32, (n, width), 1)
    first_half = (lane % HEAD_DIM) < (HEAD_DIM // 2)
    partner = jnp.where(first_half, pltpu.roll(x, width - HEAD_DIM // 2, axis=1), pltpu.roll(x, HEAD_DIM // 2, axis=1))
    return x * cos_t + partner * sin_t


def _combine0_kernel(pos0_ref, pos1_ref, npos0_ref, npos1_ref, h_ref, mf_ref, p_ref, cos_ref, sin_ref, gple_ref, wpg_ref, wpu_ref,
                     gkv_ref, wk_ref, wv_ref, gq_ref, wq_ref, ys_ref,
                     ho_ref, k_ref, v_ref, q_ref, b00, b01, b10, b11, sem):
    def body(y0, y1):
        h = _moe_residual_and_ple(h_ref, mf_ref, y0, y1, p_ref, gple_ref, wpg_ref, wpu_ref)
        ho_ref[...] = h
        cos = cos_ref[...]
        sin = sin_ref[...]
        kvn = _rms(h, gkv_ref[...]).astype(BF16)
        k = jnp.dot(kvn, wk_ref[...], preferred_element_type=F32)
        k_ref[...] = _rope(k, cos, sin).astype(BF16)
        v_ref[...] = jnp.dot(kvn, wv_ref[...], preferred_element_type=F32).astype(BF16)
        qn = _rms(h, gq_ref[...]).astype(BF16)
        q = jnp.dot(qn, wq_ref[...], preferred_element_type=F32)
        q_ref[...] = (_rope(q, cos, sin) * (HEAD_DIM ** -0.5)).astype(BF16)

    _with_expert_rows(pos0_ref, pos1_ref, npos0_ref, npos1_ref, ys_ref, ((b00, b01), (b10, b11)), sem, body)


def _combine1_kernel(pos0_ref, pos1_ref, npos0_ref, npos1_ref, h_ref, mf_ref, p_ref, gple_ref, wpg_ref, wpu_ref, gfin_ref, ys_ref,
                     out_ref, b00, b01, b10, b11, sem):
    def body(y0, y1):
        h = _moe_residual_and_ple(h_ref, mf_ref, y0, y1, p_ref, gple_ref, wpg_ref, wpu_ref)
        out_ref[...] = _rms(h, gfin_ref[...])

    _with_expert_rows(pos0_ref, pos1_ref, npos0_ref, npos1_ref, ys_ref, ((b00, b01), (b10, b11)), sem, body)


def _combine_specs():
    last = N_TOK // TOK_TILE - 1
    smem = lambda: pl.BlockSpec((TOK_TILE,), lambda i: (i,), memory_space=pltpu.SMEM)
    smem_next = lambda: pl.BlockSpec((TOK_TILE,), lambda i: (jnp.minimum(i + 1, last),), memory_space=pltpu.SMEM)
    tok = lambda w: pl.BlockSpec((TOK_TILE, w), lambda i: (i, 0))
    full = lambda a, b: pl.BlockSpec((a, b), lambda i: (0, 0))
    scratch = [pltpu.VMEM((TOK_TILE * CHUNKS, LANES), F32) for _ in range(4)] + [pltpu.SemaphoreType.DMA((2,))]
    ple = lambda layer: pl.BlockSpec((None, TOK_TILE, PLE_DIM), lambda i: (layer, i, 0))
    return smem, smem_next, tok, ple, full, scratch


def _combine0(pos0, pos1, h, mf, p, cos_t, sin_t, g_ple, w_pg, w_pu, g_kv, w_k, w_v, g_q, w_q, ys_slab):
    smem, smem_next, tok, ple, full, scratch = _combine_specs()
    return pl.pallas_call(
        _combine0_kernel,
        grid=(N_TOK // TOK_TILE,),
        in_specs=[smem(), smem(), smem_next(), smem_next(), tok(D_MODEL), tok(LANES), ple(0), tok(LANES), tok(LANES),
                  full(1, D_MODEL), full(D_MODEL, D_MODEL), full(PLE_DIM, D_MODEL),
                  full(1, D_MODEL), full(D_MODEL, KV_DIM), full(D_MODEL, KV_DIM),
                  full(1, D_MODEL), full(D_MODEL, D_MODEL),
                  pl.BlockSpec(memory_space=pl.ANY)],
        out_specs=[tok(D_MODEL), tok(KV_DIM), tok(KV_DIM), tok(D_MODEL)],
        out_shape=[jax.ShapeDtypeStruct((N_TOK, D_MODEL), F32),
                   jax.ShapeDtypeStruct((N_TOK, KV_DIM), BF16),
                   jax.ShapeDtypeStruct((N_TOK, KV_DIM), BF16),
                   jax.ShapeDtypeStruct((N_TOK, D_MODEL), BF16)],
        scratch_shapes=scratch,
        compiler_params=pltpu.CompilerParams(dimension_semantics=("arbitrary",), vmem_limit_bytes=56 << 20),
        name="combine0",
    )(pos0, pos1, pos0, pos1, h, mf, p, cos_t, sin_t, g_ple, w_pg, w_pu, g_kv, w_k, w_v, g_q, w_q, ys_slab)


def _combine1(pos0, pos1, h, mf, p, g_ple, w_pg, w_pu, g_fin, ys_slab):
    smem, smem_next, tok, ple, full, scratch = _combine_specs()
    return pl.pallas_call(
        _combine1_kernel,
        grid=(N_TOK // TOK_TILE,),
        in_specs=[smem(), smem(), smem_next(), smem_next(), tok(D_MODEL), tok(LANES), ple(1),
                  full(1, D_MODEL), full(D_MODEL, D_MODEL), full(PLE_DIM, D_MODEL), full(1, D_MODEL),
                  pl.BlockSpec(memory_space=pl.ANY)],
        out_specs=tok(D_MODEL),
        out_shape=jax.ShapeDtypeStruct((N_TOK, D_MODEL), F32),
        scratch_shapes=scratch,
        compiler_params=pltpu.CompilerParams(dimension_semantics=("arbitrary",), vmem_limit_bytes=48 << 20),
        name="combine1",
    )(pos0, pos1, pos0, pos1, h, mf, p, g_ple, w_pg, w_pu, g_fin, ys_slab)


def _attn1_kernel(sink_ref, q_ref, kc_ref, kp_ref, vc_ref, vp_ref, h_ref, wo_ref, gffn_ref, wr_ref, br_ref,
                  ho_ref, xn_ref, mi_ref, mf_ref, cnt_ref, run_ref, k_s, v_s, o_s):
    i = pl.program_id(0)
    steps_per_seq = SEQ // ATT_TILE

    @pl.when(i == 0)
    def _():
        run_ref[...] = jnp.zeros_like(run_ref)

    k_s[:ATT_BLOCK, :] = kp_ref[...]
    k_s[ATT_BLOCK:, :] = kc_ref[...]
    v_s[:ATT_BLOCK, :] = vp_ref[...]
    v_s[ATT_BLOCK:, :] = vc_ref[...]
    seq_start = (i % steps_per_seq) == 0

    n_rows = Q_PER_KV * ATT_BLOCK
    row = lax.broadcasted_iota(jnp.int32, (n_rows, 2 * ATT_BLOCK), 0)
    col = lax.broadcasted_iota(jnp.int32, (n_rows, 2 * ATT_BLOCK), 1)
    rel = (row % ATT_BLOCK) + ATT_BLOCK - col
    band = (rel >= 0) & (rel < WINDOW)
    head_of_lane = lax.broadcasted_iota(jnp.int32, (2 * ATT_BLOCK, KV_DIM), 1) // HEAD_DIM
    grp_of_row = lax.broadcasted_iota(jnp.int32, (n_rows, 1), 0) // ATT_BLOCK

    def block(b, carry):
        r0 = pl.multiple_of(b * ATT_BLOCK, ATT_BLOCK)
        has_prev = jnp.logical_not(seq_start & (b == 0))
        keep = band & (has_prev | (col >= ATT_BLOCK))
        kcat = k_s[pl.ds(r0, 2 * ATT_BLOCK), :]
        vcat = v_s[pl.ds(r0, 2 * ATT_BLOCK), :]
        q = q_ref[pl.ds(r0, ATT_BLOCK), :]
        qs = jnp.concatenate([q[:, g * KV_DIM:(g + 1) * KV_DIM] for g in range(Q_PER_KV)], axis=0)
        o = jnp.zeros((n_rows, KV_DIM), F32)
        for kh in range(N_KV_HEADS):
            k_h = jnp.where(head_of_lane == kh, kcat, jnp.zeros_like(kcat))
            v_h = jnp.where(head_of_lane == kh, vcat, jnp.zeros_like(vcat))
            s = lax.dot_general(qs, k_h, (((1,), (1,)), ((), ())), preferred_element_type=F32)
            s = jnp.where(keep, s, NEG_INF)
            sink = jnp.zeros((n_rows, 1), F32)
            for g in range(Q_PER_KV):
                sink = jnp.where(grp_of_row == g, sink_ref[kh * Q_PER_KV + g], sink)
            m = jnp.maximum(jnp.max(s, axis=1, keepdims=True), sink)
            e = jnp.exp(s - m)
            denom = jnp.sum(e, axis=1, keepdims=True) + jnp.exp(sink - m)
            pr = (e / denom).astype(BF16)
            o = o + jnp.dot(pr, v_h, preferred_element_type=F32)
        o_cat = jnp.concatenate([o[g * ATT_BLOCK:(g + 1) * ATT_BLOCK, :] for g in range(Q_PER_KV)], axis=1)
        o_s[pl.ds(r0, ATT_BLOCK), :] = o_cat.astype(BF16)
        return carry

    lax.fori_loop(0, ATT_TILE // ATT_BLOCK, block, 0)
    h = h_ref[...] + jnp.dot(o_s[...], wo_ref[...], preferred_element_type=F32)
    ho_ref[...] = h
    xn = _rms(h, gffn_ref[...])
    _store_slab(xn_ref, xn)
    _route_and_rank(xn, wr_ref, br_ref, run_ref, mi_ref, mf_ref, cnt_ref)


def _attn1(sinks, q, k, v, h, w_o, g_ffn, w_r, b_r):
    blocks_per_step = ATT_TILE // ATT_BLOCK
    cur = lambda w: pl.BlockSpec((ATT_TILE, w), lambda i: (i, 0))
    prev = lambda w: pl.BlockSpec((ATT_BLOCK, w), lambda i: (jnp.maximum(i * blocks_per_step - 1, 0), 0))
    full = lambda a, b: pl.BlockSpec((a, b), lambda i: (0, 0))
    return pl.pallas_call(
        _attn1_kernel,
        grid=(N_TOK // ATT_TILE,),
        in_specs=[pl.BlockSpec(memory_space=pltpu.SMEM),
                  cur(D_MODEL), cur(KV_DIM), prev(KV_DIM), cur(KV_DIM), prev(KV_DIM), cur(D_MODEL),
                  full(D_MODEL, D_MODEL), full(1, D_MODEL), full(D_MODEL, 2 * LANES), full(1, LANES)],
        out_specs=[cur(D_MODEL),
                   pl.BlockSpec((ATT_TILE * CHUNKS, LANES), lambda i: (i, 0)),
                   pl.BlockSpec((SUBLANES, ATT_TILE), lambda i: (0, i)), cur(LANES),
                   pl.BlockSpec((SUBLANES, LANES), lambda i: (0, 0))],
        out_shape=[jax.ShapeDtypeStruct((N_TOK, D_MODEL), F32),
                   jax.ShapeDtypeStruct((N_TOK * CHUNKS, LANES), F32),
                   jax.ShapeDtypeStruct((SUBLANES, N_TOK), jnp.int32),
                   jax.ShapeDtypeStruct((N_TOK, LANES), F32),
                   jax.ShapeDtypeStruct((SUBLANES, LANES), F32)],
        scratch_shapes=[pltpu.VMEM((SUBLANES, LANES), F32),
                        pltpu.VMEM((ATT_TILE + ATT_BLOCK, KV_DIM), BF16),
                        pltpu.VMEM((ATT_TILE + ATT_BLOCK, KV_DIM), BF16),
                        pltpu.VMEM((ATT_TILE, D_MODEL), BF16)],
        compiler_params=pltpu.CompilerParams(dimension_semantics=("arbitrary",), vmem_limit_bytes=48 << 20),
        name="attn1",
    )(sinks, q, k, k, v, v, h, w_o, g_ffn, w_r, b_r)


def _plan_kernel(cnt_ref, meta_ref, pos_ref, vt_ref, ve_ref, gs_ref, ge_ref, nv_ref, nx_ref):
    def per_expert(e, carry):
        off, v = carry
        c = cnt_ref[e]
        gs_ref[e] = off
        ge_ref[e] = off + c
        first = off // EXPERT_TILE
        n_vis = jnp.where(c > 0, (off + c - 1) // EXPERT_TILE - first + 1, 0)

        def visit(j, _):
            vt_ref[v + j] = first + j
            ve_ref[v + j] = e
            return 0

        lax.fori_loop(0, n_vis, visit, 0)
        return off + c, v + n_vis

    _, total = lax.fori_loop(0, N_EXPERTS, per_expert, (jnp.int32(0), jnp.int32(0)))
    nv_ref[0] = total

    def pad(j, _):
        vt_ref[j] = vt_ref[total - 1]
        ve_ref[j] = ve_ref[total - 1]
        return 0

    lax.fori_loop(total, N_VISITS, pad, 0)

    def link(k, nxt):
        e = N_EXPERTS - 1 - k
        nx_ref[e] = jnp.where(nxt >= 0, nxt, e)
        return jnp.where(cnt_ref[e] > 0, e, nxt)

    lax.fori_loop(0, N_EXPERTS, link, jnp.int32(-1))

    meta = meta_ref[...]
    start = jnp.zeros_like(meta)
    for e in range(N_EXPERTS):
        start = jnp.where(meta == e, gs_ref[e], start)
    pos_ref[...] = start + pltpu.roll(meta, SUBLANES - 2, axis=0)


def _routing_tables(meta_i, counts_f):
    counts = counts_f[0, :N_EXPERTS].astype(jnp.int32)
    smem = lambda: pl.BlockSpec(memory_space=pltpu.SMEM)
    i32 = lambda n: jax.ShapeDtypeStruct((n,), jnp.int32)
    pos, vt, ve, gs, ge, nv, nx = pl.pallas_call(
        _plan_kernel,
        in_specs=[smem(), pl.BlockSpec(memory_space=pltpu.VMEM)],
        out_specs=[pl.BlockSpec(memory_space=pltpu.VMEM), smem(), smem(), smem(), smem(), smem(), smem()],
        out_shape=[jax.ShapeDtypeStruct((SUBLANES, N_TOK), jnp.int32),
                   i32(N_VISITS), i32(N_VISITS), i32(N_EXPERTS), i32(N_EXPERTS), i32(1), i32(N_EXPERTS)],
        name="plan",
    )(counts, meta_i)
    return pos[0], pos[1], vt, ve, gs, ge, nv, nx


def _router_weights(w_grp, b_grp, w_exp, b_exp):
    w = jnp.zeros((D_MODEL, LANES), F32)
    w = w.at[:, :N_EXPERTS].set(w_exp).at[:, GROUP_LANE0:GROUP_LANE0 + N_GROUPS].set(w_grp)
    b = jnp.zeros((1, LANES), F32)
    b = b.at[0, :N_EXPERTS].set(b_exp).at[0, GROUP_LANE0:GROUP_LANE0 + N_GROUPS].set(b_grp)
    w_hi = w.astype(BF16)
    w_lo = (w - w_hi.astype(F32)).astype(BF16)
    return jnp.concatenate([w_hi, w_lo], axis=1), b


def _rope_tables(positions):
    inv = ROPE_THETA ** (-jnp.arange(0, HEAD_DIM, 2, dtype=F32) / HEAD_DIM)
    ang = positions.astype(F32).reshape(N_TOK, 1) * inv
    cos = jnp.cos(ang)
    sin = jnp.sin(ang)
    cos_t = jnp.concatenate([cos, cos, cos, cos], axis=1)
    sin_t = jnp.concatenate([-sin, sin, -sin, sin], axis=1)
    return cos_t, sin_t


def _moe(layer, meta_i, counts_f, xn_slab, w_gate, w_up, w_down):
    pos0, pos1, vt, ve, gs, ge, nv, nx = _routing_tables(meta_i, counts_f)
    xs_slab = _dispatch(pos0, pos1, xn_slab)
    ys_slab = _experts(layer, vt, ve, gs, ge, nv, nx, xs_slab, w_gate, w_up, w_down)
    return pos0, pos1, ys_slab


def kernel(x, p, positions, g_mix, g_ffn, pool_w, pool_scale, g_kv, w_k, w_v, w_q, w_o, sinks, w_group_router, b_group_router, w_expert_router, b_expert_router, w_exp_gate, w_exp_up, w_exp_down, g_ple, w_ple_gate, w_ple_up, g_final):
    row = lambda a: a.reshape(1, -1)
    x2 = x.reshape(N_TOK, D_MODEL)
    p2 = p.reshape(2, N_TOK, PLE_DIM)
    cos_t, sin_t = _rope_tables(positions)
    w_q_perm = w_q[0].reshape(D_MODEL, N_KV_HEADS, Q_PER_KV, HEAD_DIM).transpose(0, 2, 1, 3).reshape(D_MODEL, D_MODEL)
    w_o_perm = w_o[0].reshape(N_KV_HEADS, Q_PER_KV, HEAD_DIM, D_MODEL).transpose(1, 0, 2, 3).reshape(D_MODEL, D_MODEL)

    wr0, br0 = _router_weights(w_group_router[0], b_group_router[0], w_expert_router[0], b_expert_router[0])
    h1, xn_slab, meta_i, meta_f, counts = _mixer0(
        x2, row(g_mix[0]), pool_w[0].astype(BF16), row(pool_scale[0]), row(g_ffn[0]), wr0, br0)
    pos0, pos1, ys_slab = _moe(0, meta_i, counts, xn_slab, w_exp_gate, w_exp_up, w_exp_down)
    h3, k, v, q = _combine0(
        pos0, pos1, h1, meta_f, p2, cos_t, sin_t, row(g_ple[0]), w_ple_gate[0].astype(BF16),
        w_ple_up[0].astype(BF16), row(g_kv), w_k.astype(BF16), w_v.astype(BF16), row(g_mix[1]),
        w_q_perm.astype(BF16), ys_slab)

    wr1, br1 = _router_weights(w_group_router[1], b_group_router[1], w_expert_router[1], b_expert_router[1])
    h4, xn_slab, meta_i, meta_f, counts = _attn1(
        sinks[0], q, k, v, h3, w_o_perm.astype(BF16), row(g_ffn[1]), wr1, br1)
    pos0, pos1, ys_slab = _moe(1, meta_i, counts, xn_slab, w_exp_gate, w_exp_up, w_exp_down)
    out = _combine1(pos0, pos1, h4, meta_f, p2, row(g_ple[1]), w_ple_gate[1].astype(BF16),
                    w_ple_up[1].astype(BF16), row(g_final), ys_slab)
    return out.reshape(BATCH, SEQ, D_MODEL)
```

```python
import functools

import jax
import jax.numpy as jnp
from jax import lax
from jax.experimental import pallas as pl
from jax.experimental.pallas import tpu as pltpu

D_MODEL = 1024
BATCH = 2
SEQ = 8192
N_TOK = BATCH * SEQ
POOL_WINDOWS = (2, 4, 8, 16)
POOL_GROUP = D_MODEL // len(POOL_WINDOWS)
POOL_HALO = 16
HEAD_DIM = 64
N_Q_HEADS = 16
N_KV_HEADS = 4
Q_PER_KV = 4
KV_DIM = N_KV_HEADS * HEAD_DIM
WINDOW = 128
ATT_BLOCK = 128
ROPE_THETA = 10000.0
N_GROUPS = 4
EXPERTS_PER_GROUP = 8
N_EXPERTS = 32
D_EXPERT = 512
PLE_DIM = 256
EPS = 1e-6

LANES = 128
SUBLANES = 8
HALF = D_MODEL // 2
SLAB_ROWS = HALF // LANES
N_ASSIGN = 2 * N_TOK

TOK_TILE = 256
MIX_TILE = 512
ATT_TILE = 512
DISPATCH_TILE = 512
EXPERT_TILE = 256
N_ROW_TILES = N_ASSIGN // EXPERT_TILE
N_VISITS = N_ROW_TILES + N_EXPERTS - 1
GROUP_LANE0 = N_EXPERTS

F32 = jnp.float32
BF16 = jnp.bfloat16
U32 = jnp.uint32
NEG_INF = float("-inf")


def _rms(x, g):
    return x * lax.rsqrt(jnp.mean(x * x, axis=-1, keepdims=True) + EPS) * g


def _pack_rows(val):
    lo = pltpu.bitcast(val[:, :HALF].astype(BF16).astype(F32), U32)
    hi = pltpu.bitcast(val[:, HALF:].astype(BF16).astype(F32), U32)
    return hi | (lo >> 16)


def _unpack_rows(word):
    lo = pltpu.bitcast(word << 16, F32)
    hi = pltpu.bitcast(word & jnp.uint32(0xFFFF0000), F32)
    return jnp.concatenate([lo, hi], axis=1)


def _store_slab(ref, word):
    n = word.shape[0]
    for c in range(SLAB_ROWS):
        ref[pl.ds(c, n, stride=SLAB_ROWS), :] = word[:, c * LANES:(c + 1) * LANES]


def _load_slab(ref, n):
    return jnp.concatenate([ref[pl.ds(c, n, stride=SLAB_ROWS), :] for c in range(SLAB_ROWS)], axis=1)


def _route_and_rank(xn, wr_ref, br_ref, run_ref, mi_ref, mf_ref, cnt_ref):
    n = xn.shape[0]
    x_hi = xn.astype(BF16)
    x_lo = (xn - x_hi.astype(F32)).astype(BF16)
    hh_hl = jnp.dot(x_hi, wr_ref[...], preferred_element_type=F32)
    lh = jnp.dot(x_lo, wr_ref[:, :LANES], preferred_element_type=F32)
    logits = hh_hl[:, :LANES] + (hh_hl[:, LANES:] + lh) + br_ref[...]
    lane = lax.broadcasted_iota(jnp.int32, (n, LANES), 1).astype(F32)
    big = float(LANES)
    is_g = (lane >= GROUP_LANE0) & (lane < GROUP_LANE0 + N_GROUPS)
    gl = jnp.where(is_g, logits, NEG_INF)
    gmax = jnp.max(gl, axis=1, keepdims=True)
    gidx = jnp.min(jnp.where(gl == gmax, lane, big), axis=1, keepdims=True) - GROUP_LANE0
    pg = 1.0 / jnp.sum(jnp.exp(gl - gmax), axis=1, keepdims=True)
    lo = gidx * EXPERTS_PER_GROUP
    el = jnp.where((lane >= lo) & (lane < lo + EXPERTS_PER_GROUP), logits, NEG_INF)
    m1 = jnp.max(el, axis=1, keepdims=True)
    i1 = jnp.min(jnp.where(el == m1, lane, big), axis=1, keepdims=True)
    el2 = jnp.where(lane == i1, NEG_INF, el)
    m2 = jnp.max(el2, axis=1, keepdims=True)
    i2 = jnp.min(jnp.where(el2 == m2, lane, big), axis=1, keepdims=True)
    z = jnp.exp(m2 - m1)
    gate0 = pg / (1.0 + z)
    gate1 = pg * z / (1.0 + z)

    oh0 = jnp.where(lane == i1, 1.0, 0.0)
    oh1 = jnp.where(lane == i2, 1.0, 0.0)
    oh = oh0 + oh1
    r = lax.broadcasted_iota(jnp.int32, (n, n), 0)
    c = lax.broadcasted_iota(jnp.int32, (n, n), 1)
    lower = jnp.where(r > c, 1.0, 0.0).astype(BF16)
    prefix = jnp.dot(lower, oh.astype(BF16), preferred_element_type=F32)
    tot = prefix + run_ref[0:1, :]
    rank0 = jnp.sum(oh0 * tot, axis=1, keepdims=True)
    rank1 = jnp.sum(oh1 * tot, axis=1, keepdims=True)
    new_run = run_ref[0:1, :] + jnp.sum(oh, axis=0, keepdims=True)
    run_ref[0:1, :] = new_run
    cnt_ref[...] = jnp.broadcast_to(new_run, cnt_ref.shape)

    meta = jnp.where(lane == 0.0, i1, jnp.where(lane == 1.0, i2,
                     jnp.where(lane == 2.0, rank0, jnp.where(lane == 3.0, rank1, 0.0))))
    mi_ref[...] = meta.T[:SUBLANES, :].astype(jnp.int32)
    mf_ref[...] = jnp.where(lane == 0.0, gate0, jnp.where(lane == 1.0, gate1, 0.0))


def _mixer0_kernel(x_ref, halo_ref, gmix_ref, poolw_ref, pscale_ref, gffn_ref, wr_ref, br_ref,
                   h_ref, xn_ref, mi_ref, mf_ref, cnt_ref, run_ref):
    i = pl.program_id(0)
    tiles_per_seq = SEQ // MIX_TILE

    @pl.when(i == 0)
    def _():
        run_ref[...] = jnp.zeros_like(run_ref)

    seq_tile = i % tiles_per_seq
    x = x_ref[...]
    g = gmix_ref[...]
    hn = _rms(x, g)
    halo = jnp.where(seq_tile == 0, 0.0, _rms(halo_ref[...], g))
    full = jnp.concatenate([halo, hn], axis=0)
    t = (seq_tile * MIX_TILE + lax.broadcasted_iota(jnp.int32, (MIX_TILE, 1), 0) + 1).astype(F32)
    outs = []
    for gi, w in enumerate(POOL_WINDOWS):
        acc = full[:, gi * POOL_GROUP:(gi + 1) * POOL_GROUP]
        shift = 1
        while shift < w:
            acc = acc + pltpu.roll(acc, shift, axis=0)
            shift *= 2
        win = acc[POOL_HALO:, :]
        mean = win / jnp.minimum(t, float(w))
        dlt = (mean - hn[:, gi * POOL_GROUP:(gi + 1) * POOL_GROUP]).astype(BF16)
        outs.append(jnp.dot(dlt, poolw_ref[gi], preferred_element_type=F32))
    y = jnp.concatenate(outs, axis=1)
    h = x + y * pscale_ref[...]
    h_ref[...] = h
    xn = _rms(h, gffn_ref[...])
    _store_slab(xn_ref, _pack_rows(xn))
    _route_and_rank(xn, wr_ref, br_ref, run_ref, mi_ref, mf_ref, cnt_ref)


def _mixer0(x, g_mix, pool_w, pool_scale, g_ffn, w_r, b_r):
    n_tiles = N_TOK // MIX_TILE
    halo_per_tile = MIX_TILE // POOL_HALO
    vec = lambda: pl.BlockSpec((1, D_MODEL), lambda i: (0, 0))
    return pl.pallas_call(
        _mixer0_kernel,
        grid=(n_tiles,),
        in_specs=[
            pl.BlockSpec((MIX_TILE, D_MODEL), lambda i: (i, 0)),
            pl.BlockSpec((POOL_HALO, D_MODEL), lambda i: (jnp.maximum(i * halo_per_tile - 1, 0), 0)),
            vec(),
            pl.BlockSpec((len(POOL_WINDOWS), POOL_GROUP, POOL_GROUP), lambda i: (0, 0, 0)),
            vec(), vec(),
            pl.BlockSpec((D_MODEL, 2 * LANES), lambda i: (0, 0)),
            pl.BlockSpec((1, LANES), lambda i: (0, 0)),
        ],
        out_specs=[
            pl.BlockSpec((MIX_TILE, D_MODEL), lambda i: (i, 0)),
            pl.BlockSpec((MIX_TILE * SLAB_ROWS, LANES), lambda i: (i, 0)),
            pl.BlockSpec((SUBLANES, MIX_TILE), lambda i: (0, i)),
            pl.BlockSpec((MIX_TILE, LANES), lambda i: (i, 0)),
            pl.BlockSpec((SUBLANES, LANES), lambda i: (0, 0)),
        ],
        out_shape=[
            jax.ShapeDtypeStruct((N_TOK, D_MODEL), F32),
            jax.ShapeDtypeStruct((N_TOK * SLAB_ROWS, LANES), U32),
            jax.ShapeDtypeStruct((SUBLANES, N_TOK), jnp.int32),
            jax.ShapeDtypeStruct((N_TOK, LANES), F32),
            jax.ShapeDtypeStruct((SUBLANES, LANES), F32),
        ],
        scratch_shapes=[pltpu.VMEM((SUBLANES, LANES), F32)],
        compiler_params=pltpu.CompilerParams(dimension_semantics=("arbitrary",), vmem_limit_bytes=48 << 20),
        name="mixer0",
    )(x, x, g_mix, pool_w, pool_scale, g_ffn, w_r, b_r)


def _row_copy(src, src_row, dst, dst_row, sem):
    return pltpu.make_async_copy(
        src.at[pl.ds(pl.multiple_of(src_row * SLAB_ROWS, SLAB_ROWS), SLAB_ROWS), :],
        dst.at[pl.ds(pl.multiple_of(dst_row * SLAB_ROWS, SLAB_ROWS), SLAB_ROWS), :],
        sem)


def _dispatch_kernel(pos0_ref, pos1_ref, src_ref, dst_ref, sem):
    unroll = 8

    def issue(jo, carry):
        for u in range(unroll):
            j = jo * unroll + u
            _row_copy(src_ref, j, dst_ref, pos0_ref[j], sem).start(priority=0)
            _row_copy(src_ref, j, dst_ref, pos1_ref[j], sem).start(priority=1)
        return carry

    lax.fori_loop(0, DISPATCH_TILE // unroll, issue, 0)

    def drain(jo, carry):
        for u in range(2 * unroll):
            _row_copy(src_ref, 0, dst_ref, 0, sem).wait()
        return carry

    lax.fori_loop(0, DISPATCH_TILE // unroll, drain, 0)


def _dispatch(pos0, pos1, xn_slab):
    smem = lambda: pl.BlockSpec((DISPATCH_TILE,), lambda i: (i,), memory_space=pltpu.SMEM)
    return pl.pallas_call(
        _dispatch_kernel,
        grid=(N_TOK // DISPATCH_TILE,),
        in_specs=[smem(), smem(), pl.BlockSpec((DISPATCH_TILE * SLAB_ROWS, LANES), lambda i: (i, 0))],
        out_specs=pl.BlockSpec(memory_space=pl.ANY),
        out_shape=jax.ShapeDtypeStruct((N_ASSIGN * SLAB_ROWS, LANES), U32),
        scratch_shapes=[pltpu.SemaphoreType.DMA(())],
        compiler_params=pltpu.CompilerParams(dimension_semantics=("arbitrary",), has_side_effects=True),
        name="dispatch",
    )(pos0, pos1, xn_slab)


X_SLOTS = 3
Y_SLOTS = 2


def _experts_kernel(layer, vt_ref, ve_ref, gs_ref, ge_ref, nv_ref, nx_ref, xs_hbm, wg_hbm, wu_hbm, wd_hbm, ys_hbm,
                    xb, yb, wg_st, wu_st, wd_st, wgu_s, wd_s, xsem, ysem, wsem):
    i = pl.program_id(0)
    rows = EXPERT_TILE * SLAB_ROWS
    n_visits = nv_ref[0]
    prev = jnp.maximum(i - 1, 0)
    e = ve_ref[i]
    tile = vt_ref[i]
    valid = i < n_visits
    new_expert = (i == 0) | (e != ve_ref[prev])
    new_tile = (i == 0) | (tile != vt_ref[prev])

    def x_copy(t):
        return pltpu.make_async_copy(xs_hbm.at[pl.ds(pl.multiple_of(t * rows, rows), rows), :],
                                     xb.at[t % X_SLOTS], xsem.at[t % X_SLOTS])

    def y_copy(t):
        return pltpu.make_async_copy(yb.at[t % Y_SLOTS],
                                     ys_hbm.at[pl.ds(pl.multiple_of(t * rows, rows), rows), :], ysem.at[t % Y_SLOTS])

    def w_copies(ex):
        return (pltpu.make_async_copy(wg_hbm.at[layer, ex], wg_st, wsem.at[0]),
                pltpu.make_async_copy(wu_hbm.at[layer, ex], wu_st, wsem.at[1]),
                pltpu.make_async_copy(wd_hbm.at[layer, ex], wd_st, wsem.at[2]))

    @pl.when(i == 0)
    def _():
        x_copy(0).start()
        x_copy(1).start()
        for c in w_copies(e):
            c.start()

    @pl.when(valid & new_tile)
    def _():
        x_copy(tile).wait()

        @pl.when(tile + 2 < N_ROW_TILES)
        def _():
            x_copy(tile + 2).start()

        @pl.when(tile >= 2)
        def _():
            y_copy(tile - 2).wait()

        @pl.when(tile >= 1)
        def _():
            y_copy(tile - 1).start()

    @pl.when(valid & new_expert)
    def _():
        for c in w_copies(e):
            c.wait()
        wgu_s[:, :D_EXPERT] = wg_st[...].astype(BF16)
        wgu_s[:, D_EXPERT:] = wu_st[...].astype(BF16)
        wd_s[...] = wd_st[...].astype(BF16)
        nxt = nx_ref[e]

        @pl.when(nxt != e)
        def _():
            for c in w_copies(nxt):
                c.start()

    @pl.when(valid)
    def _():
        xs = tile % X_SLOTS
        ys = tile % Y_SLOTS
        x = _unpack_rows(jnp.concatenate(
            [xb[xs, pl.ds(c, EXPERT_TILE, stride=SLAB_ROWS), :] for c in range(SLAB_ROWS)], axis=1))
        gu = jnp.dot(x.astype(BF16), wgu_s[...], preferred_element_type=F32)
        gate = gu[:, :D_EXPERT]
        act = (gate / (1.0 + jnp.exp(-gate)) * gu[:, D_EXPERT:]).astype(BF16)
        y = _pack_rows(jnp.dot(act, wd_s[...], preferred_element_type=F32))
        r = tile * EXPERT_TILE + lax.broadcasted_iota(jnp.int32, (EXPERT_TILE, 1), 0)
        mine = (r >= gs_ref[e]) & (r < ge_ref[e])

        @pl.when(new_tile)
        def _():
            for c in range(SLAB_ROWS):
                yb[ys, pl.ds(c, EXPERT_TILE, stride=SLAB_ROWS), :] = jnp.where(
                    mine, y[:, c * LANES:(c + 1) * LANES], jnp.uint32(0))

        @pl.when(jnp.logical_not(new_tile))
        def _():
            for c in range(SLAB_ROWS):
                old = yb[ys, pl.ds(c, EXPERT_TILE, stride=SLAB_ROWS), :]
                yb[ys, pl.ds(c, EXPERT_TILE, stride=SLAB_ROWS), :] = jnp.where(mine, y[:, c * LANES:(c + 1) * LANES], old)

        @pl.when(i == n_visits - 1)
        def _():
            y_copy(tile).start()
            y_copy(tile).wait()

            @pl.when(tile >= 1)
            def _():
                y_copy(tile - 1).wait()


def _experts(layer, visit_tile, visit_expert, g_start, g_end, n_visits, next_expert, xs_slab, w_gate, w_up, w_down):
    rows = EXPERT_TILE * SLAB_ROWS
    any_spec = lambda: pl.BlockSpec(memory_space=pl.ANY)
    grid_spec = pltpu.PrefetchScalarGridSpec(
        num_scalar_prefetch=6,
        grid=(N_VISITS,),
        in_specs=[any_spec(), any_spec(), any_spec(), any_spec()],
        out_specs=any_spec(),
        scratch_shapes=[pltpu.VMEM((X_SLOTS, rows, LANES), U32), pltpu.VMEM((Y_SLOTS, rows, LANES), U32),
                        pltpu.VMEM((D_MODEL, D_EXPERT), F32), pltpu.VMEM((D_MODEL, D_EXPERT), F32),
                        pltpu.VMEM((D_EXPERT, D_MODEL), F32),
                        pltpu.VMEM((D_MODEL, 2 * D_EXPERT), BF16), pltpu.VMEM((D_EXPERT, D_MODEL), BF16),
                        pltpu.SemaphoreType.DMA((X_SLOTS,)), pltpu.SemaphoreType.DMA((Y_SLOTS,)),
                        pltpu.SemaphoreType.DMA((3,))],
    )
    return pl.pallas_call(
        functools.partial(_experts_kernel, layer),
        grid_spec=grid_spec,
        out_shape=jax.ShapeDtypeStruct((N_ASSIGN * SLAB_ROWS, LANES), U32),
        compiler_params=pltpu.CompilerParams(dimension_semantics=("arbitrary",), vmem_limit_bytes=48 << 20,
                                             has_side_effects=True),
        name="experts",
    )(visit_tile, visit_expert, g_start, g_end, n_visits, next_expert, xs_slab, w_gate, w_up, w_down)


def _with_expert_rows(pos0_ref, pos1_ref, npos0_ref, npos1_ref, ys_ref, bufs, sem, body):
    i = pl.program_id(0)

    def copies(p0_ref, p1_ref, s):
        for j in range(TOK_TILE):
            yield _row_copy(ys_ref, p0_ref[j], bufs[s][0], j, sem.at[s])
            yield _row_copy(ys_ref, p1_ref[j], bufs[s][1], j, sem.at[s])

    @pl.when(i == 0)
    def _():
        for n, c in enumerate(copies(pos0_ref, pos1_ref, 0)):
            c.start(priority=n % 2)

    for s in range(2):
        @pl.when(i % 2 == s)
        def _():
            for n, c in enumerate(copies(npos0_ref, npos1_ref, 1 - s)):
                c.start(priority=n % 2)
            for c in copies(pos0_ref, pos1_ref, s):
                c.wait()
            body(_unpack_rows(_load_slab(bufs[s][0], TOK_TILE)), _unpack_rows(_load_slab(bufs[s][1], TOK_TILE)))

            @pl.when(i == pl.num_programs(0) - 1)
            def _():
                for c in copies(npos0_ref, npos1_ref, 1 - s):
                    c.wait()


def _moe_residual_and_ple(h_ref, mf_ref, y0, y1, p_ref, gple_ref, wpg_ref, wpu_ref):
    gates = mf_ref[...]
    h = h_ref[...] + (y0 * gates[:, 0:1] + y1 * gates[:, 1:2])
    hn = _rms(h, gple_ref[...]).astype(BF16)
    gate = jnp.dot(hn, wpg_ref[...], preferred_element_type=F32)
    gate = 1.0 / (1.0 + jnp.exp(-gate))
    up = jnp.dot(p_ref[...].astype(BF16), wpu_ref[...], preferred_element_type=F32)
    return h + gate * up


def _rope(x, cos, sin_signed):
    n, width = x.shape
    reps = width // LANES
    cos_t = jnp.concatenate([cos] * reps, axis=1)
    sin_t = jnp.concatenate([sin_signed] * reps, axis=1)
    lane = lax.broadcasted_iota(jnp.int32, (n, width), 1)
    first_half = (lane % HEAD_DIM) < (HEAD_DIM // 2)
    partner = jnp.where(first_half, pltpu.roll(x, width - HEAD_DIM // 2, axis=1), pltpu.roll(x, HEAD_DIM // 2, axis=1))
    return x * cos_t + partner * sin_t


def _combine0_kernel(pos0_ref, pos1_ref, npos0_ref, npos1_ref, h_ref, mf_ref, p_ref, cos_ref, sin_ref, gple_ref, wpg_ref, wpu_ref,
                     gkv_ref, wk_ref, wv_ref, gq_ref, wq_ref, ys_ref,
                     ho_ref, k_ref, v_ref, q_ref, b00, b01, b10, b11, sem):
    def body(y0, y1):
        h = _moe_residual_and_ple(h_ref, mf_ref, y0, y1, p_ref, gple_ref, wpg_ref, wpu_ref)
        ho_ref[...] = h
        cos = cos_ref[...]
        sin = sin_ref[...]
        kvn = _rms(h, gkv_ref[...]).astype(BF16)
        k = jnp.dot(kvn, wk_ref[...], preferred_element_type=F32)
        k_ref[...] = _rope(k, cos, sin).astype(BF16)
        v_ref[...] = jnp.dot(kvn, wv_ref[...], preferred_element_type=F32).astype(BF16)
        qn = _rms(h, gq_ref[...]).astype(BF16)
        q = jnp.dot(qn, wq_ref[...], preferred_element_type=F32)
        q = (_rope(q, cos, sin) * (HEAD_DIM ** -0.5)).astype(BF16)
        for g in range(Q_PER_KV):
            q_ref[g] = q[:, g * KV_DIM:(g + 1) * KV_DIM]

    _with_expert_rows(pos0_ref, pos1_ref, npos0_ref, npos1_ref, ys_ref, ((b00, b01), (b10, b11)), sem, body)


def _combine1_kernel(pos0_ref, pos1_ref, npos0_ref, npos1_ref, h_ref, mf_ref, p_ref, gple_ref, wpg_ref, wpu_ref, gfin_ref, ys_ref,
                     out_ref, b00, b01, b10, b11, sem):
    def body(y0, y1):
        h = _moe_residual_and_ple(h_ref, mf_ref, y0, y1, p_ref, gple_ref, wpg_ref, wpu_ref)
        out_ref[...] = _rms(h, gfin_ref[...])

    _with_expert_rows(pos0_ref, pos1_ref, npos0_ref, npos1_ref, ys_ref, ((b00, b01), (b10, b11)), sem, body)


def _combine_specs():
    last = N_TOK // TOK_TILE - 1
    smem = lambda: pl.BlockSpec((TOK_TILE,), lambda i: (i,), memory_space=pltpu.SMEM)
    smem_next = lambda: pl.BlockSpec((TOK_TILE,), lambda i: (jnp.minimum(i + 1, last),), memory_space=pltpu.SMEM)
    tok = lambda w: pl.BlockSpec((TOK_TILE, w), lambda i: (i, 0))
    full = lambda a, b: pl.BlockSpec((a, b), lambda i: (0, 0))
    scratch = [pltpu.VMEM((TOK_TILE * SLAB_ROWS, LANES), U32) for _ in range(4)] + [pltpu.SemaphoreType.DMA((2,))]
    ple = lambda layer: pl.BlockSpec((None, TOK_TILE, PLE_DIM), lambda i: (layer, i, 0))
    return smem, smem_next, tok, ple, full, scratch


def _combine0(pos0, pos1, h, mf, p, cos_t, sin_t, g_ple, w_pg, w_pu, g_kv, w_k, w_v, g_q, w_q, ys_slab):
    smem, smem_next, tok, ple, full, scratch = _combine_specs()
    return pl.pallas_call(
        _combine0_kernel,
        grid=(N_TOK // TOK_TILE,),
        in_specs=[smem(), smem(), smem_next(), smem_next(), tok(D_MODEL), tok(LANES), ple(0), tok(LANES), tok(LANES),
                  full(1, D_MODEL), full(D_MODEL, D_MODEL), full(PLE_DIM, D_MODEL),
                  full(1, D_MODEL), full(D_MODEL, KV_DIM), full(D_MODEL, KV_DIM),
                  full(1, D_MODEL), full(D_MODEL, D_MODEL),
                  pl.BlockSpec(memory_space=pl.ANY)],
        out_specs=[tok(D_MODEL), tok(KV_DIM), tok(KV_DIM),
                   pl.BlockSpec((Q_PER_KV, TOK_TILE, KV_DIM), lambda i: (0, i, 0))],
        out_shape=[jax.ShapeDtypeStruct((N_TOK, D_MODEL), F32),
                   jax.ShapeDtypeStruct((N_TOK, KV_DIM), BF16),
                   jax.ShapeDtypeStruct((N_TOK, KV_DIM), BF16),
                   jax.ShapeDtypeStruct((Q_PER_KV, N_TOK, KV_DIM), BF16)],
        scratch_shapes=scratch,
        compiler_params=pltpu.CompilerParams(dimension_semantics=("arbitrary",), vmem_limit_bytes=56 << 20),
        name="combine0",
    )(pos0, pos1, pos0, pos1, h, mf, p, cos_t, sin_t, g_ple, w_pg, w_pu, g_kv, w_k, w_v, g_q, w_q, ys_slab)


def _combine1(pos0, pos1, h, mf, p, g_ple, w_pg, w_pu, g_fin, ys_slab):
    smem, smem_next, tok, ple, full, scratch = _combine_specs()
    return pl.pallas_call(
        _combine1_kernel,
        grid=(N_TOK // TOK_TILE,),
        in_specs=[smem(), smem(), smem_next(), smem_next(), tok(D_MODEL), tok(LANES), ple(1),
                  full(1, D_MODEL), full(D_MODEL, D_MODEL), full(PLE_DIM, D_MODEL), full(1, D_MODEL),
                  pl.BlockSpec(memory_space=pl.ANY)],
        out_specs=tok(D_MODEL),
        out_shape=jax.ShapeDtypeStruct((N_TOK, D_MODEL), F32),
        scratch_shapes=scratch,
        compiler_params=pltpu.CompilerParams(dimension_semantics=("arbitrary",), vmem_limit_bytes=48 << 20),
        name="combine1",
    )(pos0, pos1, pos0, pos1, h, mf, p, g_ple, w_pg, w_pu, g_fin, ys_slab)


def _attn1_kernel(sink_ref, q_ref, kc_ref, kp_ref, vc_ref, vp_ref, h_ref, wo_ref, gffn_ref, wr_ref, br_ref,
                  ho_ref, xn_ref, mi_ref, mf_ref, cnt_ref, run_ref, k_s, v_s, o_s):
    i = pl.program_id(0)
    steps_per_seq = SEQ // ATT_TILE

    @pl.when(i == 0)
    def _():
        run_ref[...] = jnp.zeros_like(run_ref)

    k_s[:ATT_BLOCK, :] = kp_ref[...]
    k_s[ATT_BLOCK:, :] = kc_ref[...]
    v_s[:ATT_BLOCK, :] = vp_ref[...]
    v_s[ATT_BLOCK:, :] = vc_ref[...]
    seq_start = (i % steps_per_seq) == 0

    n_rows = Q_PER_KV * ATT_BLOCK
    assert WINDOW == ATT_BLOCK
    row = lax.broadcasted_iota(jnp.int32, (n_rows, ATT_BLOCK), 0)
    col = lax.broadcasted_iota(jnp.int32, (n_rows, ATT_BLOCK), 1)
    upper = col > (row % ATT_BLOCK)
    head_of_lane = lax.broadcasted_iota(jnp.int32, (2 * ATT_BLOCK, KV_DIM), 1) // HEAD_DIM
    grp_of_row = lax.broadcasted_iota(jnp.int32, (n_rows, 1), 0) // ATT_BLOCK

    def block(b, carry):
        r0 = pl.multiple_of(b * ATT_BLOCK, ATT_BLOCK)
        has_prev = jnp.logical_not(seq_start & (b == 0))
        kcat = k_s[pl.ds(r0, 2 * ATT_BLOCK), :]
        vcat = v_s[pl.ds(r0, 2 * ATT_BLOCK), :]
        qs = jnp.concatenate([q_ref[g, pl.ds(r0, ATT_BLOCK), :] for g in range(Q_PER_KV)], axis=0)
        o = jnp.zeros((n_rows, KV_DIM), F32)
        for kh in range(N_KV_HEADS):
            k_h = jnp.where(head_of_lane == kh, kcat, jnp.zeros_like(kcat))
            v_h = jnp.where(head_of_lane == kh, vcat, jnp.zeros_like(vcat))
            s = lax.dot_general(qs, k_h, (((1,), (1,)), ((), ())), preferred_element_type=F32)
            s = jnp.where(upper, jnp.where(has_prev, s[:, :ATT_BLOCK], NEG_INF), s[:, ATT_BLOCK:])
            sink = jnp.zeros((n_rows, 1), F32)
            for g in range(Q_PER_KV):
                sink = jnp.where(grp_of_row == g, sink_ref[kh * Q_PER_KV + g], sink)
            m = jnp.maximum(jnp.max(s, axis=1, keepdims=True), sink)
            e = jnp.exp(s - m)
            denom = jnp.sum(e, axis=1, keepdims=True) + jnp.exp(sink - m)
            pr = e / denom
            pr = jnp.concatenate([jnp.where(upper, pr, 0.0), jnp.where(upper, 0.0, pr)], axis=1).astype(BF16)
            o = o + jnp.dot(pr, v_h, preferred_element_type=F32)
        for g in range(Q_PER_KV):
            o_s[g, pl.ds(r0, ATT_BLOCK), :] = o[g * ATT_BLOCK:(g + 1) * ATT_BLOCK, :].astype(BF16)
        return carry

    lax.fori_loop(0, ATT_TILE // ATT_BLOCK, block, 0)
    o_cat = jnp.concatenate([o_s[g] for g in range(Q_PER_KV)], axis=1)
    h = h_ref[...] + jnp.dot(o_cat, wo_ref[...], preferred_element_type=F32)
    ho_ref[...] = h
    xn = _rms(h, gffn_ref[...])
    _store_slab(xn_ref, _pack_rows(xn))
    _route_and_rank(xn, wr_ref, br_ref, run_ref, mi_ref, mf_ref, cnt_ref)


def _attn1(sinks, q, k, v, h, w_o, g_ffn, w_r, b_r):
    blocks_per_step = ATT_TILE // ATT_BLOCK
    cur = lambda w: pl.BlockSpec((ATT_TILE, w), lambda i: (i, 0))
    prev = lambda w: pl.BlockSpec((ATT_BLOCK, w), lambda i: (jnp.maximum(i * blocks_per_step - 1, 0), 0))
    full = lambda a, b: pl.BlockSpec((a, b), lambda i: (0, 0))
    return pl.pallas_call(
        _attn1_kernel,
        grid=(N_TOK // ATT_TILE,),
        in_specs=[pl.BlockSpec(memory_space=pltpu.SMEM),
                  pl.BlockSpec((Q_PER_KV, ATT_TILE, KV_DIM), lambda i: (0, i, 0)),
                  cur(KV_DIM), prev(KV_DIM), cur(KV_DIM), prev(KV_DIM), cur(D_MODEL),
                  full(D_MODEL, D_MODEL), full(1, D_MODEL), full(D_MODEL, 2 * LANES), full(1, LANES)],
        out_specs=[cur(D_MODEL),
                   pl.BlockSpec((ATT_TILE * SLAB_ROWS, LANES), lambda i: (i, 0)),
                   pl.BlockSpec((SUBLANES, ATT_TILE), lambda i: (0, i)), cur(LANES),
                   pl.BlockSpec((SUBLANES, LANES), lambda i: (0, 0))],
        out_shape=[jax.ShapeDtypeStruct((N_TOK, D_MODEL), F32),
                   jax.ShapeDtypeStruct((N_TOK * SLAB_ROWS, LANES), U32),
                   jax.ShapeDtypeStruct((SUBLANES, N_TOK), jnp.int32),
                   jax.ShapeDtypeStruct((N_TOK, LANES), F32),
                   jax.ShapeDtypeStruct((SUBLANES, LANES), F32)],
        scratch_shapes=[pltpu.VMEM((SUBLANES, LANES), F32),
                        pltpu.VMEM((ATT_TILE + ATT_BLOCK, KV_DIM), BF16),
                        pltpu.VMEM((ATT_TILE + ATT_BLOCK, KV_DIM), BF16),
                        pltpu.VMEM((Q_PER_KV, ATT_TILE, KV_DIM), BF16)],
        compiler_params=pltpu.CompilerParams(dimension_semantics=("arbitrary",), vmem_limit_bytes=48 << 20),
        name="attn1",
    )(sinks, q, k, k, v, v, h, w_o, g_ffn, w_r, b_r)


def _plan_kernel(cnt_ref, meta_ref, pos_ref, vt_ref, ve_ref, gs_ref, ge_ref, nv_ref, nx_ref):
    def per_expert(e, carry):
        off, v = carry
        c = cnt_ref[e]
        gs_ref[e] = off
        ge_ref[e] = off + c
        first = off // EXPERT_TILE
        n_vis = jnp.where(c > 0, (off + c - 1) // EXPERT_TILE - first + 1, 0)

        def visit(j, _):
            vt_ref[v + j] = first + j
            ve_ref[v + j] = e
            return 0

        lax.fori_loop(0, n_vis, visit, 0)
        return off + c, v + n_vis

    _, total = lax.fori_loop(0, N_EXPERTS, per_expert, (jnp.int32(0), jnp.int32(0)))
    nv_ref[0] = total

    def pad(j, _):
        vt_ref[j] = vt_ref[total - 1]
        ve_ref[j] = ve_ref[total - 1]
        return 0

    lax.fori_loop(total, N_VISITS, pad, 0)

    def link(k, nxt):
        e = N_EXPERTS - 1 - k
        nx_ref[e] = jnp.where(nxt >= 0, nxt, e)
        return jnp.where(cnt_ref[e] > 0, e, nxt)

    lax.fori_loop(0, N_EXPERTS, link, jnp.int32(-1))

    meta = meta_ref[...]
    start = jnp.zeros_like(meta)
    for e in range(N_EXPERTS):
        start = jnp.where(meta == e, gs_ref[e], start)
    pos_ref[...] = start + pltpu.roll(meta, SUBLANES - 2, axis=0)


def _routing_tables(meta_i, counts_f):
    counts = counts_f[0, :N_EXPERTS].astype(jnp.int32)
    smem = lambda: pl.BlockSpec(memory_space=pltpu.SMEM)
    i32 = lambda n: jax.ShapeDtypeStruct((n,), jnp.int32)
    pos, vt, ve, gs, ge, nv, nx = pl.pallas_call(
        _plan_kernel,
        in_specs=[smem(), pl.BlockSpec(memory_space=pltpu.VMEM)],
        out_specs=[pl.BlockSpec(memory_space=pltpu.VMEM), smem(), smem(), smem(), smem(), smem(), smem()],
        out_shape=[jax.ShapeDtypeStruct((SUBLANES, N_TOK), jnp.int32),
                   i32(N_VISITS), i32(N_VISITS), i32(N_EXPERTS), i32(N_EXPERTS), i32(1), i32(N_EXPERTS)],
        name="plan",
    )(counts, meta_i)
    return pos[0], pos[1], vt, ve, gs, ge, nv, nx


def _router_weights(w_grp, b_grp, w_exp, b_exp):
    w = jnp.zeros((D_MODEL, LANES), F32)
    w = w.at[:, :N_EXPERTS].set(w_exp).at[:, GROUP_LANE0:GROUP_LANE0 + N_GROUPS].set(w_grp)
    b = jnp.zeros((1, LANES), F32)
    b = b.at[0, :N_EXPERTS].set(b_exp).at[0, GROUP_LANE0:GROUP_LANE0 + N_GROUPS].set(b_grp)
    w_hi = w.astype(BF16)
    w_lo = (w - w_hi.astype(F32)).astype(BF16)
    return jnp.concatenate([w_hi, w_lo], axis=1), b


def _rope_tables(positions):
    inv = ROPE_THETA ** (-jnp.arange(0, HEAD_DIM, 2, dtype=F32) / HEAD_DIM)
    ang = positions.astype(F32).reshape(N_TOK, 1) * inv
    cos = jnp.cos(ang)
    sin = jnp.sin(ang)
    cos_t = jnp.concatenate([cos, cos, cos, cos], axis=1)
    sin_t = jnp.concatenate([-sin, sin, -sin, sin], axis=1)
    return cos_t, sin_t


def _moe(layer, meta_i, counts_f, xn_slab, w_gate, w_up, w_down):
    pos0, pos1, vt, ve, gs, ge, nv, nx = _routing_tables(meta_i, counts_f)
    xs_slab = _dispatch(pos0, pos1, xn_slab)
    ys_slab = _experts(layer, vt, ve, gs, ge, nv, nx, xs_slab, w_gate, w_up, w_down)
    return pos0, pos1, ys_slab


def kernel(x, p, positions, g_mix, g_ffn, pool_w, pool_scale, g_kv, w_k, w_v, w_q, w_o, sinks, w_group_router, b_group_router, w_expert_router, b_expert_router, w_exp_gate, w_exp_up, w_exp_down, g_ple, w_ple_gate, w_ple_up, g_final):
    row = lambda a: a.reshape(1, -1)
    x2 = x.reshape(N_TOK, D_MODEL)
    p2 = p.reshape(2, N_TOK, PLE_DIM)
    cos_t, sin_t = _rope_tables(positions)
    w_q_perm = w_q[0].reshape(D_MODEL, N_KV_HEADS, Q_PER_KV, HEAD_DIM).transpose(0, 2, 1, 3).reshape(D_MODEL, D_MODEL)
    w_o_perm = w_o[0].reshape(N_KV_HEADS, Q_PER_KV, HEAD_DIM, D_MODEL).transpose(1, 0, 2, 3).reshape(D_MODEL, D_MODEL)

    wr0, br0 = _router_weights(w_group_router[0], b_group_router[0], w_expert_router[0], b_expert_router[0])
    h1, xn_slab, meta_i, meta_f, counts = _mixer0(
        x2, row(g_mix[0]), pool_w[0].astype(BF16), row(pool_scale[0]), row(g_ffn[0]), wr0, br0)
    pos0, pos1, ys_slab = _moe(0, meta_i, counts, xn_slab, w_exp_gate, w_exp_up, w_exp_down)
    h3, k, v, q = _combine0(
        pos0, pos1, h1, meta_f, p2, cos_t, sin_t, row(g_ple[0]), w_ple_gate[0].astype(BF16),
        w_ple_up[0].astype(BF16), row(g_kv), w_k.astype(BF16), w_v.astype(BF16), row(g_mix[1]),
        w_q_perm.astype(BF16), ys_slab)

    wr1, br1 = _router_weights(w_group_router[1], b_group_router[1], w_expert_router[1], b_expert_router[1])
    h4, xn_slab, meta_i, meta_f, counts = _attn1(
        sinks[0], q, k, v, h3, w_o_perm.astype(BF16), row(g_ffn[1]), wr1, br1)
    pos0, pos1, ys_slab = _moe(1, meta_i, counts, xn_slab, w_exp_gate, w_exp_up, w_exp_down)
    out = _combine1(pos0, pos1, h4, meta_f, p2, row(g_ple[1]), w_ple_gate[1].astype(BF16),
                    w_ple_up[1].astype(BF16), row(g_final), ys_slab)
    return out.reshape(BATCH, SEQ, D_MODEL)
```

```python
import functools

import jax
import jax.numpy as jnp
from jax import lax
from jax.experimental import pallas as pl
from jax.experimental.pallas import tpu as pltpu

D_MODEL = 1024
BATCH = 2
SEQ = 8192
N_TOK = BATCH * SEQ
POOL_WINDOWS = (2, 4, 8, 16)
POOL_GROUP = D_MODEL // len(POOL_WINDOWS)
POOL_HALO = 16
HEAD_DIM = 64
N_Q_HEADS = 16
N_KV_HEADS = 4
Q_PER_KV = 4
KV_DIM = N_KV_HEADS * HEAD_DIM
WINDOW = 128
ATT_BLOCK = 128
ROPE_THETA = 10000.0
N_GROUPS = 4
EXPERTS_PER_GROUP = 8
N_EXPERTS = 32
D_EXPERT = 512
PLE_DIM = 256
EPS = 1e-6

LANES = 128
SUBLANES = 8
HALF = D_MODEL // 2
SLAB_ROWS = HALF // LANES
N_ASSIGN = 2 * N_TOK

TOK_TILE = 256
MIX_TILE = 512
ATT_TILE = 512
DISPATCH_TILE = 512
EXPERT_TILE = 256
N_ROW_TILES = N_ASSIGN // EXPERT_TILE
N_VISITS = N_ROW_TILES + N_EXPERTS - 1
GROUP_LANE0 = N_EXPERTS

F32 = jnp.float32
BF16 = jnp.bfloat16
U32 = jnp.uint32
NEG_INF = float("-inf")


def _rms(x, g):
    return x * lax.rsqrt(jnp.mean(x * x, axis=-1, keepdims=True) + EPS) * g


def _pack_rows(val):
    lo = pltpu.bitcast(val[:, :HALF].astype(BF16).astype(F32), U32)
    hi = pltpu.bitcast(val[:, HALF:].astype(BF16).astype(F32), U32)
    return hi | (lo >> 16)


def _unpack_rows(word):
    lo = pltpu.bitcast(word << 16, F32)
    hi = pltpu.bitcast(word & jnp.uint32(0xFFFF0000), F32)
    return jnp.concatenate([lo, hi], axis=1)


def _store_slab(ref, word):
    n = word.shape[0]
    for c in range(SLAB_ROWS):
        ref[pl.ds(c, n, stride=SLAB_ROWS), :] = word[:, c * LANES:(c + 1) * LANES]


def _load_slab(ref, n):
    return jnp.concatenate([ref[pl.ds(c, n, stride=SLAB_ROWS), :] for c in range(SLAB_ROWS)], axis=1)


def _route_and_rank(xn, wr_ref, br_ref, up_ref, run_ref, mi_ref, mf_ref, cnt_ref):
    n = xn.shape[0]
    x_hi = xn.astype(BF16)
    x_lo = (xn - x_hi.astype(F32)).astype(BF16)
    hh_hl = jnp.dot(x_hi, wr_ref[...], preferred_element_type=F32)
    lh = jnp.dot(x_lo, wr_ref[:, :LANES], preferred_element_type=F32)
    logits = (hh_hl[:, :LANES] + (hh_hl[:, LANES:] + lh) + br_ref[...]).T
    big = float(LANES)

    grow = lax.broadcasted_iota(jnp.int32, (SUBLANES, n), 0).astype(F32)
    gl = jnp.where(grow < N_GROUPS, logits[GROUP_LANE0:GROUP_LANE0 + SUBLANES, :], NEG_INF)
    gmax = jnp.max(gl, axis=0, keepdims=True)
    gidx = jnp.min(jnp.where(gl == gmax, grow, big), axis=0, keepdims=True)
    pg = 1.0 / jnp.sum(jnp.exp(gl - gmax), axis=0, keepdims=True)

    erow = lax.broadcasted_iota(jnp.int32, (N_EXPERTS, n), 0).astype(F32)
    lo = gidx * EXPERTS_PER_GROUP
    el = jnp.where((erow >= lo) & (erow < lo + EXPERTS_PER_GROUP), logits[:N_EXPERTS, :], NEG_INF)
    m1 = jnp.max(el, axis=0, keepdims=True)
    i1 = jnp.min(jnp.where(el == m1, erow, big), axis=0, keepdims=True)
    el2 = jnp.where(erow == i1, NEG_INF, el)
    m2 = jnp.max(el2, axis=0, keepdims=True)
    i2 = jnp.min(jnp.where(el2 == m2, erow, big), axis=0, keepdims=True)
    z = jnp.exp(m2 - m1)
    gate0 = pg / (1.0 + z)
    gate1 = pg * z / (1.0 + z)

    row = lax.broadcasted_iota(jnp.int32, (LANES, n), 0).astype(F32)
    oh0 = jnp.where(row == i1, 1.0, 0.0)
    oh1 = jnp.where(row == i2, 1.0, 0.0)
    oh = oh0 + oh1
    prefix = jnp.dot(oh.astype(BF16), up_ref[...], preferred_element_type=F32)
    tot = prefix + run_ref[...]
    rank0 = jnp.sum(oh0 * tot, axis=0, keepdims=True)
    rank1 = jnp.sum(oh1 * tot, axis=0, keepdims=True)
    new_run = run_ref[...] + jnp.sum(oh, axis=1, keepdims=True)
    run_ref[...] = new_run
    cnt_ref[...] = jnp.broadcast_to(new_run, cnt_ref.shape)

    r8 = lax.broadcasted_iota(jnp.int32, (SUBLANES, n), 0)
    meta = jnp.where(r8 == 0, i1, jnp.where(r8 == 1, i2, jnp.where(r8 == 2, rank0, jnp.where(r8 == 3, rank1, 0.0))))
    mi_ref[...] = meta.astype(jnp.int32)
    mf_ref[...] = jnp.where(row == 0.0, gate0, jnp.where(row == 1.0, gate1, 0.0)).T


def _mixer0_kernel(x_ref, halo_ref, gmix_ref, poolw_ref, pscale_ref, gffn_ref, wr_ref, br_ref, up_ref,
                   h_ref, xn_ref, mi_ref, mf_ref, cnt_ref, run_ref):
    i = pl.program_id(0)
    tiles_per_seq = SEQ // MIX_TILE

    @pl.when(i == 0)
    def _():
        run_ref[...] = jnp.zeros_like(run_ref)

    seq_tile = i % tiles_per_seq
    x = x_ref[...]
    g = gmix_ref[...]
    hn = _rms(x, g)
    halo = jnp.where(seq_tile == 0, 0.0, _rms(halo_ref[...], g))
    full = jnp.concatenate([halo, hn], axis=0)
    t = (seq_tile * MIX_TILE + lax.broadcasted_iota(jnp.int32, (MIX_TILE, 1), 0) + 1).astype(F32)
    outs = []
    for gi, w in enumerate(POOL_WINDOWS):
        acc = full[:, gi * POOL_GROUP:(gi + 1) * POOL_GROUP]
        shift = 1
        while shift < w:
            acc = acc + pltpu.roll(acc, shift, axis=0)
            shift *= 2
        win = acc[POOL_HALO:, :]
        mean = win / jnp.minimum(t, float(w))
        dlt = (mean - hn[:, gi * POOL_GROUP:(gi + 1) * POOL_GROUP]).astype(BF16)
        outs.append(jnp.dot(dlt, poolw_ref[gi], preferred_element_type=F32))
    y = jnp.concatenate(outs, axis=1)
    h = x + y * pscale_ref[...]
    h_ref[...] = h
    xn = _rms(h, gffn_ref[...])
    _store_slab(xn_ref, _pack_rows(xn))
    _route_and_rank(xn, wr_ref, br_ref, up_ref, run_ref, mi_ref, mf_ref, cnt_ref)


def _mixer0(x, g_mix, pool_w, pool_scale, g_ffn, w_r, b_r, upper):
    n_tiles = N_TOK // MIX_TILE
    halo_per_tile = MIX_TILE // POOL_HALO
    vec = lambda: pl.BlockSpec((1, D_MODEL), lambda i: (0, 0))
    return pl.pallas_call(
        _mixer0_kernel,
        grid=(n_tiles,),
        in_specs=[
            pl.BlockSpec((MIX_TILE, D_MODEL), lambda i: (i, 0)),
            pl.BlockSpec((POOL_HALO, D_MODEL), lambda i: (jnp.maximum(i * halo_per_tile - 1, 0), 0)),
            vec(),
            pl.BlockSpec((len(POOL_WINDOWS), POOL_GROUP, POOL_GROUP), lambda i: (0, 0, 0)),
            vec(), vec(),
            pl.BlockSpec((D_MODEL, 2 * LANES), lambda i: (0, 0)),
            pl.BlockSpec((1, LANES), lambda i: (0, 0)),
            pl.BlockSpec((MIX_TILE, MIX_TILE), lambda i: (0, 0)),
        ],
        out_specs=[
            pl.BlockSpec((MIX_TILE, D_MODEL), lambda i: (i, 0)),
            pl.BlockSpec((MIX_TILE * SLAB_ROWS, LANES), lambda i: (i, 0)),
            pl.BlockSpec((SUBLANES, MIX_TILE), lambda i: (0, i)),
            pl.BlockSpec((MIX_TILE, LANES), lambda i: (i, 0)),
            pl.BlockSpec((LANES, LANES), lambda i: (0, 0)),
        ],
        out_shape=[
            jax.ShapeDtypeStruct((N_TOK, D_MODEL), F32),
            jax.ShapeDtypeStruct((N_TOK * SLAB_ROWS, LANES), U32),
            jax.ShapeDtypeStruct((SUBLANES, N_TOK), jnp.int32),
            jax.ShapeDtypeStruct((N_TOK, LANES), F32),
            jax.ShapeDtypeStruct((LANES, LANES), F32),
        ],
        scratch_shapes=[pltpu.VMEM((LANES, 1), F32)],
        compiler_params=pltpu.CompilerParams(dimension_semantics=("arbitrary",), vmem_limit_bytes=48 << 20),
        name="mixer0",
    )(x, x, g_mix, pool_w, pool_scale, g_ffn, w_r, b_r, upper)


def _row_copy(src, src_row, dst, dst_row, sem):
    return pltpu.make_async_copy(
        src.at[pl.ds(pl.multiple_of(src_row * SLAB_ROWS, SLAB_ROWS), SLAB_ROWS), :],
        dst.at[pl.ds(pl.multiple_of(dst_row * SLAB_ROWS, SLAB_ROWS), SLAB_ROWS), :],
        sem)


def _dispatch_kernel(pos0_ref, pos1_ref, src_ref, dst_ref, sem):
    unroll = 8

    def issue(jo, carry):
        for u in range(unroll):
            j = jo * unroll + u
            _row_copy(src_ref, j, dst_ref, pos0_ref[j], sem).start(priority=0)
            _row_copy(src_ref, j, dst_ref, pos1_ref[j], sem).start(priority=1)
        return carry

    lax.fori_loop(0, DISPATCH_TILE // unroll, issue, 0)

    def drain(jo, carry):
        for u in range(2 * unroll):
            _row_copy(src_ref, 0, dst_ref, 0, sem).wait()
        return carry

    lax.fori_loop(0, DISPATCH_TILE // unroll, drain, 0)


def _dispatch(pos0, pos1, xn_slab):
    smem = lambda: pl.BlockSpec((DISPATCH_TILE,), lambda i: (i,), memory_space=pltpu.SMEM)
    return pl.pallas_call(
        _dispatch_kernel,
        grid=(N_TOK // DISPATCH_TILE,),
        in_specs=[smem(), smem(), pl.BlockSpec((DISPATCH_TILE * SLAB_ROWS, LANES), lambda i: (i, 0))],
        out_specs=pl.BlockSpec(memory_space=pl.ANY),
        out_shape=jax.ShapeDtypeStruct((N_ASSIGN * SLAB_ROWS, LANES), U32),
        scratch_shapes=[pltpu.SemaphoreType.DMA(())],
        compiler_params=pltpu.CompilerParams(dimension_semantics=("arbitrary",), has_side_effects=True),
        name="dispatch",
    )(pos0, pos1, xn_slab)


X_SLOTS = 3
Y_SLOTS = 2


def _experts_kernel(layer, vt_ref, ve_ref, gs_ref, ge_ref, nv_ref, nx_ref, xs_hbm, wg_hbm, wu_hbm, wd_hbm, ys_hbm,
                    xb, yb, wg_st, wu_st, wd_st, wgu_s, wd_s, xsem, ysem, wsem):
    i = pl.program_id(0)
    rows = EXPERT_TILE * SLAB_ROWS
    n_visits = nv_ref[0]
    prev = jnp.maximum(i - 1, 0)
    e = ve_ref[i]
    tile = vt_ref[i]
    valid = i < n_visits
    new_expert = (i == 0) | (e != ve_ref[prev])
    new_tile = (i == 0) | (tile != vt_ref[prev])

    def x_copy(t):
        return pltpu.make_async_copy(xs_hbm.at[pl.ds(pl.multiple_of(t * rows, rows), rows), :],
                                     xb.at[t % X_SLOTS], xsem.at[t % X_SLOTS])

    def y_copy(t):
        return pltpu.make_async_copy(yb.at[t % Y_SLOTS],
                                     ys_hbm.at[pl.ds(pl.multiple_of(t * rows, rows), rows), :], ysem.at[t % Y_SLOTS])

    def w_copies(ex):
        return (pltpu.make_async_copy(wg_hbm.at[layer, ex], wg_st, wsem.at[0]),
                pltpu.make_async_copy(wu_hbm.at[layer, ex], wu_st, wsem.at[1]),
                pltpu.make_async_copy(wd_hbm.at[layer, ex], wd_st, wsem.at[2]))

    @pl.when(i == 0)
    def _():
        x_copy(0).start()
        x_copy(1).start()
        for c in w_copies(e):
            c.start()

    @pl.when(valid & new_tile)
    def _():
        x_copy(tile).wait()

        @pl.when(tile + 2 < N_ROW_TILES)
        def _():
            x_copy(tile + 2).start()

        @pl.when(tile >= 2)
        def _():
            y_copy(tile - 2).wait()

        @pl.when(tile >= 1)
        def _():
            y_copy(tile - 1).start()

    @pl.when(valid & new_expert)
    def _():
        for c in w_copies(e):
            c.wait()
        wgu_s[:, :D_EXPERT] = wg_st[...].astype(BF16)
        wgu_s[:, D_EXPERT:] = wu_st[...].astype(BF16)
        wd_s[...] = wd_st[...].astype(BF16)
        nxt = nx_ref[e]

        @pl.when(nxt != e)
        def _():
            for c in w_copies(nxt):
                c.start()

    @pl.when(valid)
    def _():
        xs = tile % X_SLOTS
        ys = tile % Y_SLOTS
        x = _unpack_rows(jnp.concatenate(
            [xb[xs, pl.ds(c, EXPERT_TILE, stride=SLAB_ROWS), :] for c in range(SLAB_ROWS)], axis=1))
        gu = jnp.dot(x.astype(BF16), wgu_s[...], preferred_element_type=F32)
        gate = gu[:, :D_EXPERT]
        act = (gate / (1.0 + jnp.exp(-gate)) * gu[:, D_EXPERT:]).astype(BF16)
        y = _pack_rows(jnp.dot(act, wd_s[...], preferred_element_type=F32))
        r = tile * EXPERT_TILE + lax.broadcasted_iota(jnp.int32, (EXPERT_TILE, 1), 0)
        mine = (r >= gs_ref[e]) & (r < ge_ref[e])

        @pl.when(new_tile)
        def _():
            for c in range(SLAB_ROWS):
                yb[ys, pl.ds(c, EXPERT_TILE, stride=SLAB_ROWS), :] = jnp.where(
                    mine, y[:, c * LANES:(c + 1) * LANES], jnp.uint32(0))

        @pl.when(jnp.logical_not(new_tile))
        def _():
            for c in range(SLAB_ROWS):
                old = yb[ys, pl.ds(c, EXPERT_TILE, stride=SLAB_ROWS), :]
                yb[ys, pl.ds(c, EXPERT_TILE, stride=SLAB_ROWS), :] = jnp.where(mine, y[:, c * LANES:(c + 1) * LANES], old)

        @pl.when(i == n_visits - 1)
        def _():
            y_copy(tile).start()
            y_copy(tile).wait()

            @pl.when(tile >= 1)
            def _():
                y_copy(tile - 1).wait()


def _experts(layer, visit_tile, visit_expert, g_start, g_end, n_visits, next_expert, xs_slab, w_gate, w_up, w_down):
    rows = EXPERT_TILE * SLAB_ROWS
    any_spec = lambda: pl.BlockSpec(memory_space=pl.ANY)
    grid_spec = pltpu.PrefetchScalarGridSpec(
        num_scalar_prefetch=6,
        grid=(N_VISITS,),
        in_specs=[any_spec(), any_spec(), any_spec(), any_spec()],
        out_specs=any_spec(),
        scratch_shapes=[pltpu.VMEM((X_SLOTS, rows, LANES), U32), pltpu.VMEM((Y_SLOTS, rows, LANES), U32),
                        pltpu.VMEM((D_MODEL, D_EXPERT), F32), pltpu.VMEM((D_MODEL, D_EXPERT), F32),
                        pltpu.VMEM((D_EXPERT, D_MODEL), F32),
                        pltpu.VMEM((D_MODEL, 2 * D_EXPERT), BF16), pltpu.VMEM((D_EXPERT, D_MODEL), BF16),
                        pltpu.SemaphoreType.DMA((X_SLOTS,)), pltpu.SemaphoreType.DMA((Y_SLOTS,)),
                        pltpu.SemaphoreType.DMA((3,))],
    )
    return pl.pallas_call(
        functools.partial(_experts_kernel, layer),
        grid_spec=grid_spec,
        out_shape=jax.ShapeDtypeStruct((N_ASSIGN * SLAB_ROWS, LANES), U32),
        compiler_params=pltpu.CompilerParams(dimension_semantics=("arbitrary",), vmem_limit_bytes=48 << 20,
                                             has_side_effects=True),
        name="experts",
    )(visit_tile, visit_expert, g_start, g_end, n_visits, next_expert, xs_slab, w_gate, w_up, w_down)


def _with_expert_rows(pos0_ref, pos1_ref, npos0_ref, npos1_ref, ys_ref, bufs, sem, body):
    i = pl.program_id(0)

    def copies(p0_ref, p1_ref, s):
        for j in range(TOK_TILE):
            yield _row_copy(ys_ref, p0_ref[j], bufs[s][0], j, sem.at[s])
            yield _row_copy(ys_ref, p1_ref[j], bufs[s][1], j, sem.at[s])

    @pl.when(i == 0)
    def _():
        for n, c in enumerate(copies(pos0_ref, pos1_ref, 0)):
            c.start(priority=n % 2)

    for s in range(2):
        @pl.when(i % 2 == s)
        def _():
            for n, c in enumerate(copies(npos0_ref, npos1_ref, 1 - s)):
                c.start(priority=n % 2)
            for c in copies(pos0_ref, pos1_ref, s):
                c.wait()
            body(_unpack_rows(_load_slab(bufs[s][0], TOK_TILE)), _unpack_rows(_load_slab(bufs[s][1], TOK_TILE)))

            @pl.when(i == pl.num_programs(0) - 1)
            def _():
                for c in copies(npos0_ref, npos1_ref, 1 - s):
                    c.wait()


def _moe_residual_and_ple(h_ref, mf_ref, y0, y1, p_ref, gple_ref, wpg_ref, wpu_ref):
    gates = mf_ref[...]
    h = h_ref[...] + (y0 * gates[:, 0:1] + y1 * gates[:, 1:2])
    hn = _rms(h, gple_ref[...]).astype(BF16)
    gate = jnp.dot(hn, wpg_ref[...], preferred_element_type=F32)
    gate = 1.0 / (1.0 + jnp.exp(-gate))
    up = jnp.dot(p_ref[...].astype(BF16), wpu_ref[...], preferred_element_type=F32)
    return h + gate * up


def _rope(x, cos, sin_signed):
    n, width = x.shape
    reps = width // LANES
    cos_t = jnp.concatenate([cos] * reps, axis=1)
    sin_t = jnp.concatenate([sin_signed] * reps, axis=1)
    lane = lax.broadcasted_iota(jnp.int32, (n, width), 1)
    first_half = (lane % HEAD_DIM) < (HEAD_DIM // 2)
    partner = jnp.where(first_half, pltpu.roll(x, width - HEAD_DIM // 2, axis=1), pltpu.roll(x, HEAD_DIM // 2, axis=1))
    return x * cos_t + partner * sin_t


def _combine0_kernel(pos0_ref, pos1_ref, npos0_ref, npos1_ref, h_ref, mf_ref, p_ref, cos_ref, sin_ref, gple_ref, wpg_ref, wpu_ref,
                     gkv_ref, wk_ref, wv_ref, gq_ref, wq_ref, ys_ref,
                     ho_ref, k_ref, v_ref, q_ref, b00, b01, b10, b11, sem):
    def body(y0, y1):
        h = _moe_residual_and_ple(h_ref, mf_ref, y0, y1, p_ref, gple_ref, wpg_ref, wpu_ref)
        ho_ref[...] = h
        cos = cos_ref[...]
        sin = sin_ref[...]
        kvn = _rms(h, gkv_ref[...]).astype(BF16)
        k = jnp.dot(kvn, wk_ref[...], preferred_element_type=F32)
        k_ref[...] = _rope(k, cos, sin).astype(BF16)
        v_ref[...] = jnp.dot(kvn, wv_ref[...], preferred_element_type=F32).astype(BF16)
        qn = _rms(h, gq_ref[...]).astype(BF16)
        q = jnp.dot(qn, wq_ref[...], preferred_element_type=F32)
        q = (_rope(q, cos, sin) * (HEAD_DIM ** -0.5)).astype(BF16)
        for g in range(Q_PER_KV):
            q_ref[g] = q[:, g * KV_DIM:(g + 1) * KV_DIM]

    _with_expert_rows(pos0_ref, pos1_ref, npos0_ref, npos1_ref, ys_ref, ((b00, b01), (b10, b11)), sem, body)


def _combine1_kernel(pos0_ref, pos1_ref, npos0_ref, npos1_ref, h_ref, mf_ref, p_ref, gple_ref, wpg_ref, wpu_ref, gfin_ref, ys_ref,
                     out_ref, b00, b01, b10, b11, sem):
    def body(y0, y1):
        h = _moe_residual_and_ple(h_ref, mf_ref, y0, y1, p_ref, gple_ref, wpg_ref, wpu_ref)
        out_ref[...] = _rms(h, gfin_ref[...])

    _with_expert_rows(pos0_ref, pos1_ref, npos0_ref, npos1_ref, ys_ref, ((b00, b01), (b10, b11)), sem, body)


def _combine_specs():
    last = N_TOK // TOK_TILE - 1
    smem = lambda: pl.BlockSpec((TOK_TILE,), lambda i: (i,), memory_space=pltpu.SMEM)
    smem_next = lambda: pl.BlockSpec((TOK_TILE,), lambda i: (jnp.minimum(i + 1, last),), memory_space=pltpu.SMEM)
    tok = lambda w: pl.BlockSpec((TOK_TILE, w), lambda i: (i, 0))
    full = lambda a, b: pl.BlockSpec((a, b), lambda i: (0, 0))
    scratch = [pltpu.VMEM((TOK_TILE * SLAB_ROWS, LANES), U32) for _ in range(4)] + [pltpu.SemaphoreType.DMA((2,))]
    ple = lambda layer: pl.BlockSpec((None, TOK_TILE, PLE_DIM), lambda i: (layer, i, 0))
    return smem, smem_next, tok, ple, full, scratch


def _combine0(pos0, pos1, h, mf, p, cos_t, sin_t, g_ple, w_pg, w_pu, g_kv, w_k, w_v, g_q, w_q, ys_slab):
    smem, smem_next, tok, ple, full, scratch = _combine_specs()
    return pl.pallas_call(
        _combine0_kernel,
        grid=(N_TOK // TOK_TILE,),
        in_specs=[smem(), smem(), smem_next(), smem_next(), tok(D_MODEL), tok(LANES), ple(0), tok(LANES), tok(LANES),
                  full(1, D_MODEL), full(D_MODEL, D_MODEL), full(PLE_DIM, D_MODEL),
                  full(1, D_MODEL), full(D_MODEL, KV_DIM), full(D_MODEL, KV_DIM),
                  full(1, D_MODEL), full(D_MODEL, D_MODEL),
                  pl.BlockSpec(memory_space=pl.ANY)],
        out_specs=[tok(D_MODEL), tok(KV_DIM), tok(KV_DIM),
                   pl.BlockSpec((Q_PER_KV, TOK_TILE, KV_DIM), lambda i: (0, i, 0))],
        out_shape=[jax.ShapeDtypeStruct((N_TOK, D_MODEL), F32),
                   jax.ShapeDtypeStruct((N_TOK, KV_DIM), BF16),
                   jax.ShapeDtypeStruct((N_TOK, KV_DIM), BF16),
                   jax.ShapeDtypeStruct((Q_PER_KV, N_TOK, KV_DIM), BF16)],
        scratch_shapes=scratch,
        compiler_params=pltpu.CompilerParams(dimension_semantics=("arbitrary",), vmem_limit_bytes=56 << 20),
        name="combine0",
    )(pos0, pos1, pos0, pos1, h, mf, p, cos_t, sin_t, g_ple, w_pg, w_pu, g_kv, w_k, w_v, g_q, w_q, ys_slab)


def _combine1(pos0, pos1, h, mf, p, g_ple, w_pg, w_pu, g_fin, ys_slab):
    smem, smem_next, tok, ple, full, scratch = _combine_specs()
    return pl.pallas_call(
        _combine1_kernel,
        grid=(N_TOK // TOK_TILE,),
        in_specs=[smem(), smem(), smem_next(), smem_next(), tok(D_MODEL), tok(LANES), ple(1),
                  full(1, D_MODEL), full(D_MODEL, D_MODEL), full(PLE_DIM, D_MODEL), full(1, D_MODEL),
                  pl.BlockSpec(memory_space=pl.ANY)],
        out_specs=tok(D_MODEL),
        out_shape=jax.ShapeDtypeStruct((N_TOK, D_MODEL), F32),
        scratch_shapes=scratch,
        compiler_params=pltpu.CompilerParams(dimension_semantics=("arbitrary",), vmem_limit_bytes=48 << 20),
        name="combine1",
    )(pos0, pos1, pos0, pos1, h, mf, p, g_ple, w_pg, w_pu, g_fin, ys_slab)


def _attn1_kernel(sink_ref, q_ref, kc_ref, kp_ref, vc_ref, vp_ref, h_ref, wo_ref, gffn_ref, wr_ref, br_ref, up_ref,
                  ho_ref, xn_ref, mi_ref, mf_ref, cnt_ref, run_ref, k_s, v_s, o_s):
    i = pl.program_id(0)
    steps_per_seq = SEQ // ATT_TILE

    @pl.when(i == 0)
    def _():
        run_ref[...] = jnp.zeros_like(run_ref)

    k_s[:ATT_BLOCK, :] = kp_ref[...]
    k_s[ATT_BLOCK:, :] = kc_ref[...]
    v_s[:ATT_BLOCK, :] = vp_ref[...]
    v_s[ATT_BLOCK:, :] = vc_ref[...]
    seq_start = (i % steps_per_seq) == 0

    n_rows = Q_PER_KV * ATT_BLOCK
    assert WINDOW == ATT_BLOCK
    row = lax.broadcasted_iota(jnp.int32, (n_rows, ATT_BLOCK), 0)
    col = lax.broadcasted_iota(jnp.int32, (n_rows, ATT_BLOCK), 1)
    upper = col > (row % ATT_BLOCK)
    head_of_lane = lax.broadcasted_iota(jnp.int32, (2 * ATT_BLOCK, KV_DIM), 1) // HEAD_DIM
    grp_of_row = lax.broadcasted_iota(jnp.int32, (n_rows, 1), 0) // ATT_BLOCK

    def block(b, carry):
        r0 = pl.multiple_of(b * ATT_BLOCK, ATT_BLOCK)
        has_prev = jnp.logical_not(seq_start & (b == 0))
        kcat = k_s[pl.ds(r0, 2 * ATT_BLOCK), :]
        vcat = v_s[pl.ds(r0, 2 * ATT_BLOCK), :]
        qs = jnp.concatenate([q_ref[g, pl.ds(r0, ATT_BLOCK), :] for g in range(Q_PER_KV)], axis=0)
        o = jnp.zeros((n_rows, KV_DIM), F32)
        for kh in range(N_KV_HEADS):
            k_h = jnp.where(head_of_lane == kh, kcat, jnp.zeros_like(kcat))
            v_h = jnp.where(head_of_lane == kh, vcat, jnp.zeros_like(vcat))
            s = lax.dot_general(qs, k_h, (((1,), (1,)), ((), ())), preferred_element_type=F32)
            s = jnp.where(upper, jnp.where(has_prev, s[:, :ATT_BLOCK], NEG_INF), s[:, ATT_BLOCK:])
            sink = jnp.zeros((n_rows, 1), F32)
            for g in range(Q_PER_KV):
                sink = jnp.where(grp_of_row == g, sink_ref[kh * Q_PER_KV + g], sink)
            m = jnp.maximum(jnp.max(s, axis=1, keepdims=True), sink)
            e = jnp.exp(s - m)
            denom = jnp.sum(e, axis=1, keepdims=True) + jnp.exp(sink - m)
            pr = e / denom
            pr = jnp.concatenate([jnp.where(upper, pr, 0.0), jnp.where(upper, 0.0, pr)], axis=1).astype(BF16)
            o = o + jnp.dot(pr, v_h, preferred_element_type=F32)
        for g in range(Q_PER_KV):
            o_s[g, pl.ds(r0, ATT_BLOCK), :] = o[g * ATT_BLOCK:(g + 1) * ATT_BLOCK, :].astype(BF16)
        return carry

    lax.fori_loop(0, ATT_TILE // ATT_BLOCK, block, 0)
    o_cat = jnp.concatenate([o_s[g] for g in range(Q_PER_KV)], axis=1)
    h = h_ref[...] + jnp.dot(o_cat, wo_ref[...], preferred_element_type=F32)
    ho_ref[...] = h
    xn = _rms(h, gffn_ref[...])
    _store_slab(xn_ref, _pack_rows(xn))
    _route_and_rank(xn, wr_ref, br_ref, up_ref, run_ref, mi_ref, mf_ref, cnt_ref)


def _attn1(sinks, q, k, v, h, w_o, g_ffn, w_r, b_r, upper):
    blocks_per_step = ATT_TILE // ATT_BLOCK
    cur = lambda w: pl.BlockSpec((ATT_TILE, w), lambda i: (i, 0))
    prev = lambda w: pl.BlockSpec((ATT_BLOCK, w), lambda i: (jnp.maximum(i * blocks_per_step - 1, 0), 0))
    full = lambda a, b: pl.BlockSpec((a, b), lambda i: (0, 0))
    return pl.pallas_call(
        _attn1_kernel,
        grid=(N_TOK // ATT_TILE,),
        in_specs=[pl.BlockSpec(memory_space=pltpu.SMEM),
                  pl.BlockSpec((Q_PER_KV, ATT_TILE, KV_DIM), lambda i: (0, i, 0)),
                  cur(KV_DIM), prev(KV_DIM), cur(KV_DIM), prev(KV_DIM), cur(D_MODEL),
                  full(D_MODEL, D_MODEL), full(1, D_MODEL), full(D_MODEL, 2 * LANES), full(1, LANES),
                  full(ATT_TILE, ATT_TILE)],
        out_specs=[cur(D_MODEL),
                   pl.BlockSpec((ATT_TILE * SLAB_ROWS, LANES), lambda i: (i, 0)),
                   pl.BlockSpec((SUBLANES, ATT_TILE), lambda i: (0, i)), cur(LANES),
                   pl.BlockSpec((LANES, LANES), lambda i: (0, 0))],
        out_shape=[jax.ShapeDtypeStruct((N_TOK, D_MODEL), F32),
                   jax.ShapeDtypeStruct((N_TOK * SLAB_ROWS, LANES), U32),
                   jax.ShapeDtypeStruct((SUBLANES, N_TOK), jnp.int32),
                   jax.ShapeDtypeStruct((N_TOK, LANES), F32),
                   jax.ShapeDtypeStruct((LANES, LANES), F32)],
        scratch_shapes=[pltpu.VMEM((LANES, 1), F32),
                        pltpu.VMEM((ATT_TILE + ATT_BLOCK, KV_DIM), BF16),
                        pltpu.VMEM((ATT_TILE + ATT_BLOCK, KV_DIM), BF16),
                        pltpu.VMEM((Q_PER_KV, ATT_TILE, KV_DIM), BF16)],
        compiler_params=pltpu.CompilerParams(dimension_semantics=("arbitrary",), vmem_limit_bytes=48 << 20),
        name="attn1",
    )(sinks, q, k, k, v, v, h, w_o, g_ffn, w_r, b_r, upper)


def _plan_kernel(cnt_ref, meta_ref, pos_ref, vt_ref, ve_ref, gs_ref, ge_ref, nv_ref, nx_ref):
    def per_expert(e, carry):
        off, v = carry
        c = cnt_ref[e]
        gs_ref[e] = off
        ge_ref[e] = off + c
        first = off // EXPERT_TILE
        n_vis = jnp.where(c > 0, (off + c - 1) // EXPERT_TILE - first + 1, 0)

        def visit(j, _):
            vt_ref[v + j] = first + j
            ve_ref[v + j] = e
            return 0

        lax.fori_loop(0, n_vis, visit, 0)
        return off + c, v + n_vis

    _, total = lax.fori_loop(0, N_EXPERTS, per_expert, (jnp.int32(0), jnp.int32(0)))
    nv_ref[0] = total

    def pad(j, _):
        vt_ref[j] = vt_ref[total - 1]
        ve_ref[j] = ve_ref[total - 1]
        return 0

    lax.fori_loop(total, N_VISITS, pad, 0)

    def link(k, nxt):
        e = N_EXPERTS - 1 - k
        nx_ref[e] = jnp.where(nxt >= 0, nxt, e)
        return jnp.where(cnt_ref[e] > 0, e, nxt)

    lax.fori_loop(0, N_EXPERTS, link, jnp.int32(-1))

    meta = meta_ref[...]
    start = jnp.zeros_like(meta)
    for e in range(N_EXPERTS):
        start = jnp.where(meta == e, gs_ref[e], start)
    pos_ref[...] = start + pltpu.roll(meta, SUBLANES - 2, axis=0)


def _routing_tables(meta_i, counts_f):
    counts = counts_f[:N_EXPERTS, 0].astype(jnp.int32)
    smem = lambda: pl.BlockSpec(memory_space=pltpu.SMEM)
    i32 = lambda n: jax.ShapeDtypeStruct((n,), jnp.int32)
    pos, vt, ve, gs, ge, nv, nx = pl.pallas_call(
        _plan_kernel,
        in_specs=[smem(), pl.BlockSpec(memory_space=pltpu.VMEM)],
        out_specs=[pl.BlockSpec(memory_space=pltpu.VMEM), smem(), smem(), smem(), smem(), smem(), smem()],
        out_shape=[jax.ShapeDtypeStruct((SUBLANES, N_TOK), jnp.int32),
                   i32(N_VISITS), i32(N_VISITS), i32(N_EXPERTS), i32(N_EXPERTS), i32(1), i32(N_EXPERTS)],
        name="plan",
    )(counts, meta_i)
    return pos[0], pos[1], vt, ve, gs, ge, nv, nx


def _router_weights(w_grp, b_grp, w_exp, b_exp):
    w = jnp.zeros((D_MODEL, LANES), F32)
    w = w.at[:, :N_EXPERTS].set(w_exp).at[:, GROUP_LANE0:GROUP_LANE0 + N_GROUPS].set(w_grp)
    b = jnp.zeros((1, LANES), F32)
    b = b.at[0, :N_EXPERTS].set(b_exp).at[0, GROUP_LANE0:GROUP_LANE0 + N_GROUPS].set(b_grp)
    w_hi = w.astype(BF16)
    w_lo = (w - w_hi.astype(F32)).astype(BF16)
    return jnp.concatenate([w_hi, w_lo], axis=1), b


def _rope_tables(positions):
    inv = ROPE_THETA ** (-jnp.arange(0, HEAD_DIM, 2, dtype=F32) / HEAD_DIM)
    ang = positions.astype(F32).reshape(N_TOK, 1) * inv
    cos = jnp.cos(ang)
    sin = jnp.sin(ang)
    cos_t = jnp.concatenate([cos, cos, cos, cos], axis=1)
    sin_t = jnp.concatenate([-sin, sin, -sin, sin], axis=1)
    return cos_t, sin_t


def _moe(layer, meta_i, counts_f, xn_slab, w_gate, w_up, w_down):
    pos0, pos1, vt, ve, gs, ge, nv, nx = _routing_tables(meta_i, counts_f)
    xs_slab = _dispatch(pos0, pos1, xn_slab)
    ys_slab = _experts(layer, vt, ve, gs, ge, nv, nx, xs_slab, w_gate, w_up, w_down)
    return pos0, pos1, ys_slab


def kernel(x, p, positions, g_mix, g_ffn, pool_w, pool_scale, g_kv, w_k, w_v, w_q, w_o, sinks, w_group_router, b_group_router, w_expert_router, b_expert_router, w_exp_gate, w_exp_up, w_exp_down, g_ple, w_ple_gate, w_ple_up, g_final):
    row = lambda a: a.reshape(1, -1)
    x2 = x.reshape(N_TOK, D_MODEL)
    p2 = p.reshape(2, N_TOK, PLE_DIM)
    cos_t, sin_t = _rope_tables(positions)
    assert MIX_TILE == ATT_TILE
    upper = jnp.triu(jnp.ones((MIX_TILE, MIX_TILE), BF16), k=1)
    w_q_perm = w_q[0].reshape(D_MODEL, N_KV_HEADS, Q_PER_KV, HEAD_DIM).transpose(0, 2, 1, 3).reshape(D_MODEL, D_MODEL)
    w_o_perm = w_o[0].reshape(N_KV_HEADS, Q_PER_KV, HEAD_DIM, D_MODEL).transpose(1, 0, 2, 3).reshape(D_MODEL, D_MODEL)

    wr0, br0 = _router_weights(w_group_router[0], b_group_router[0], w_expert_router[0], b_expert_router[0])
    h1, xn_slab, meta_i, meta_f, counts = _mixer0(
        x2, row(g_mix[0]), pool_w[0].astype(BF16), row(pool_scale[0]), row(g_ffn[0]), wr0, br0, upper)
    pos0, pos1, ys_slab = _moe(0, meta_i, counts, xn_slab, w_exp_gate, w_exp_up, w_exp_down)
    h3, k, v, q = _combine0(
        pos0, pos1, h1, meta_f, p2, cos_t, sin_t, row(g_ple[0]), w_ple_gate[0].astype(BF16),
        w_ple_up[0].astype(BF16), row(g_kv), w_k.astype(BF16), w_v.astype(BF16), row(g_mix[1]),
        w_q_perm.astype(BF16), ys_slab)

    wr1, br1 = _router_weights(w_group_router[1], b_group_router[1], w_expert_router[1], b_expert_router[1])
    h4, xn_slab, meta_i, meta_f, counts = _attn1(
        sinks[0], q, k, v, h3, w_o_perm.astype(BF16), row(g_ffn[1]), wr1, br1, upper)
    pos0, pos1, ys_slab = _moe(1, meta_i, counts, xn_slab, w_exp_gate, w_exp_up, w_exp_down)
    out = _combine1(pos0, pos1, h4, meta_f, p2, row(g_ple[1]), w_ple_gate[1].astype(BF16),
                    w_ple_up[1].astype(BF16), row(g_final), ys_slab)
    return out.reshape(BATCH, SEQ, D_MODEL)
```

```python
import functools

import jax
import jax.numpy as jnp
from jax import lax
from jax.experimental import pallas as pl
from jax.experimental.pallas import tpu as pltpu

D_MODEL = 1024
BATCH = 2
SEQ = 8192
N_TOK = BATCH * SEQ
POOL_WINDOWS = (2, 4, 8, 16)
POOL_GROUP = D_MODEL // len(POOL_WINDOWS)
POOL_HALO = 16
HEAD_DIM = 64
N_Q_HEADS = 16
N_KV_HEADS = 4
Q_PER_KV = 4
KV_DIM = N_KV_HEADS * HEAD_DIM
WINDOW = 128
ATT_BLOCK = 128
ROPE_THETA = 10000.0
N_GROUPS = 4
EXPERTS_PER_GROUP = 8
N_EXPERTS = 32
D_EXPERT = 512
PLE_DIM = 256
EPS = 1e-6

LANES = 128
SUBLANES = 8
HALF = D_MODEL // 2
SLAB_ROWS = HALF // LANES
N_ASSIGN = 2 * N_TOK

TOK_TILE = 512
MIX_TILE = 512
ATT_TILE = 512
DISPATCH_TILE = 512
EXPERT_TILE = 256
N_ROW_TILES = N_ASSIGN // EXPERT_TILE
N_VISITS = N_ROW_TILES + N_EXPERTS - 1
GROUP_LANE0 = N_EXPERTS

F32 = jnp.float32
BF16 = jnp.bfloat16
U32 = jnp.uint32
NEG_INF = float("-inf")


def _rms(x, g):
    return x * lax.rsqrt(jnp.mean(x * x, axis=-1, keepdims=True) + EPS) * g


def _pack_rows(val):
    lo = pltpu.bitcast(val[:, :HALF].astype(BF16).astype(F32), U32)
    hi = pltpu.bitcast(val[:, HALF:].astype(BF16).astype(F32), U32)
    return hi | (lo >> 16)


def _unpack_rows(word):
    lo = pltpu.bitcast(word << 16, F32)
    hi = pltpu.bitcast(word & jnp.uint32(0xFFFF0000), F32)
    return jnp.concatenate([lo, hi], axis=1)


def _store_slab(ref, word):
    n = word.shape[0]
    for c in range(SLAB_ROWS):
        ref[pl.ds(c, n, stride=SLAB_ROWS), :] = word[:, c * LANES:(c + 1) * LANES]


def _load_slab(ref, n):
    return jnp.concatenate([ref[pl.ds(c, n, stride=SLAB_ROWS), :] for c in range(SLAB_ROWS)], axis=1)


def _route_and_rank(xn, wr_ref, br_ref, up_ref, run_ref, mi_ref, mf_ref, cnt_ref):
    n = xn.shape[0]
    x_hi = xn.astype(BF16)
    x_lo = (xn - x_hi.astype(F32)).astype(BF16)
    hh_hl = jnp.dot(x_hi, wr_ref[...], preferred_element_type=F32)
    lh = jnp.dot(x_lo, wr_ref[:, :LANES], preferred_element_type=F32)
    logits = (hh_hl[:, :LANES] + (hh_hl[:, LANES:] + lh) + br_ref[...]).T
    big = float(LANES)

    grow = lax.broadcasted_iota(jnp.int32, (SUBLANES, n), 0).astype(F32)
    gl = jnp.where(grow < N_GROUPS, logits[GROUP_LANE0:GROUP_LANE0 + SUBLANES, :], NEG_INF)
    gmax = jnp.max(gl, axis=0, keepdims=True)
    gidx = jnp.min(jnp.where(gl == gmax, grow, big), axis=0, keepdims=True)
    pg = 1.0 / jnp.sum(jnp.exp(gl - gmax), axis=0, keepdims=True)

    erow = lax.broadcasted_iota(jnp.int32, (N_EXPERTS, n), 0).astype(F32)
    lo = gidx * EXPERTS_PER_GROUP
    el = jnp.where((erow >= lo) & (erow < lo + EXPERTS_PER_GROUP), logits[:N_EXPERTS, :], NEG_INF)
    m1 = jnp.max(el, axis=0, keepdims=True)
    i1 = jnp.min(jnp.where(el == m1, erow, big), axis=0, keepdims=True)
    el2 = jnp.where(erow == i1, NEG_INF, el)
    m2 = jnp.max(el2, axis=0, keepdims=True)
    i2 = jnp.min(jnp.where(el2 == m2, erow, big), axis=0, keepdims=True)
    z = jnp.exp(m2 - m1)
    gate0 = pg / (1.0 + z)
    gate1 = pg * z / (1.0 + z)

    row = lax.broadcasted_iota(jnp.int32, (LANES, n), 0).astype(F32)
    oh0 = jnp.where(row == i1, 1.0, 0.0)
    oh1 = jnp.where(row == i2, 1.0, 0.0)
    oh = oh0 + oh1
    prefix = jnp.dot(oh.astype(BF16), up_ref[...], preferred_element_type=F32)
    tot = prefix + run_ref[...]
    rank0 = jnp.sum(oh0 * tot, axis=0, keepdims=True)
    rank1 = jnp.sum(oh1 * tot, axis=0, keepdims=True)
    new_run = run_ref[...] + jnp.sum(oh, axis=1, keepdims=True)
    run_ref[...] = new_run
    cnt_ref[...] = jnp.broadcast_to(new_run, cnt_ref.shape)

    r8 = lax.broadcasted_iota(jnp.int32, (SUBLANES, n), 0)
    meta = jnp.where(r8 == 0, i1, jnp.where(r8 == 1, i2, jnp.where(r8 == 2, rank0, jnp.where(r8 == 3, rank1, 0.0))))
    mi_ref[...] = meta.astype(jnp.int32)
    mf_ref[...] = jnp.where(row == 0.0, gate0, jnp.where(row == 1.0, gate1, 0.0)).T


def _mixer0_kernel(x_ref, halo_ref, gmix_ref, poolw_ref, pscale_ref, gffn_ref, wr_ref, br_ref, up_ref,
                   h_ref, xn_ref, mi_ref, mf_ref, cnt_ref, run_ref):
    i = pl.program_id(0)
    tiles_per_seq = SEQ // MIX_TILE

    @pl.when(i == 0)
    def _():
        run_ref[...] = jnp.zeros_like(run_ref)

    seq_tile = i % tiles_per_seq
    x = x_ref[...]
    g = gmix_ref[...]
    hn = _rms(x, g)
    halo = jnp.where(seq_tile == 0, 0.0, _rms(halo_ref[...], g))
    full = jnp.concatenate([halo, hn], axis=0)
    t = (seq_tile * MIX_TILE + lax.broadcasted_iota(jnp.int32, (MIX_TILE, 1), 0) + 1).astype(F32)
    outs = []
    for gi, w in enumerate(POOL_WINDOWS):
        acc = full[:, gi * POOL_GROUP:(gi + 1) * POOL_GROUP]
        shift = 1
        while shift < w:
            acc = acc + pltpu.roll(acc, shift, axis=0)
            shift *= 2
        win = acc[POOL_HALO:, :]
        mean = win / jnp.minimum(t, float(w))
        dlt = (mean - hn[:, gi * POOL_GROUP:(gi + 1) * POOL_GROUP]).astype(BF16)
        outs.append(jnp.dot(dlt, poolw_ref[gi], preferred_element_type=F32))
    y = jnp.concatenate(outs, axis=1)
    h = x + y * pscale_ref[...]
    h_ref[...] = h
    xn = _rms(h, gffn_ref[...])
    _store_slab(xn_ref, _pack_rows(xn))
    _route_and_rank(xn, wr_ref, br_ref, up_ref, run_ref, mi_ref, mf_ref, cnt_ref)


def _mixer0(x, g_mix, pool_w, pool_scale, g_ffn, w_r, b_r, upper):
    n_tiles = N_TOK // MIX_TILE
    halo_per_tile = MIX_TILE // POOL_HALO
    vec = lambda: pl.BlockSpec((1, D_MODEL), lambda i: (0, 0))
    return pl.pallas_call(
        _mixer0_kernel,
        grid=(n_tiles,),
        in_specs=[
            pl.BlockSpec((MIX_TILE, D_MODEL), lambda i: (i, 0)),
            pl.BlockSpec((POOL_HALO, D_MODEL), lambda i: (jnp.maximum(i * halo_per_tile - 1, 0), 0)),
            vec(),
            pl.BlockSpec((len(POOL_WINDOWS), POOL_GROUP, POOL_GROUP), lambda i: (0, 0, 0)),
            vec(), vec(),
            pl.BlockSpec((D_MODEL, 2 * LANES), lambda i: (0, 0)),
            pl.BlockSpec((1, LANES), lambda i: (0, 0)),
            pl.BlockSpec((MIX_TILE, MIX_TILE), lambda i: (0, 0)),
        ],
        out_specs=[
            pl.BlockSpec((MIX_TILE, D_MODEL), lambda i: (i, 0)),
            pl.BlockSpec((MIX_TILE * SLAB_ROWS, LANES), lambda i: (i, 0)),
            pl.BlockSpec((SUBLANES, MIX_TILE), lambda i: (0, i)),
            pl.BlockSpec((MIX_TILE, LANES), lambda i: (i, 0)),
            pl.BlockSpec((LANES, LANES), lambda i: (0, 0)),
        ],
        out_shape=[
            jax.ShapeDtypeStruct((N_TOK, D_MODEL), F32),
            jax.ShapeDtypeStruct((N_TOK * SLAB_ROWS, LANES), U32),
            jax.ShapeDtypeStruct((SUBLANES, N_TOK), jnp.int32),
            jax.ShapeDtypeStruct((N_TOK, LANES), F32),
            jax.ShapeDtypeStruct((LANES, LANES), F32),
        ],
        scratch_shapes=[pltpu.VMEM((LANES, 1), F32)],
        compiler_params=pltpu.CompilerParams(dimension_semantics=("arbitrary",), vmem_limit_bytes=48 << 20),
        name="mixer0",
    )(x, x, g_mix, pool_w, pool_scale, g_ffn, w_r, b_r, upper)


def _row_copy(src, src_row, dst, dst_row, sem):
    return pltpu.make_async_copy(
        src.at[pl.ds(pl.multiple_of(src_row * SLAB_ROWS, SLAB_ROWS), SLAB_ROWS), :],
        dst.at[pl.ds(pl.multiple_of(dst_row * SLAB_ROWS, SLAB_ROWS), SLAB_ROWS), :],
        sem)


def _dispatch_kernel(pos0_ref, pos1_ref, src_ref, dst_ref, sem):
    unroll = 8

    def issue(jo, carry):
        for u in range(unroll):
            j = jo * unroll + u
            _row_copy(src_ref, j, dst_ref, pos0_ref[j], sem).start(priority=0)
            _row_copy(src_ref, j, dst_ref, pos1_ref[j], sem).start(priority=1)
        return carry

    lax.fori_loop(0, DISPATCH_TILE // unroll, issue, 0)

    def drain(jo, carry):
        for u in range(2 * unroll):
            _row_copy(src_ref, 0, dst_ref, 0, sem).wait()
        return carry

    lax.fori_loop(0, DISPATCH_TILE // unroll, drain, 0)


def _dispatch(pos0, pos1, xn_slab):
    smem = lambda: pl.BlockSpec((DISPATCH_TILE,), lambda i: (i,), memory_space=pltpu.SMEM)
    return pl.pallas_call(
        _dispatch_kernel,
        grid=(N_TOK // DISPATCH_TILE,),
        in_specs=[smem(), smem(), pl.BlockSpec((DISPATCH_TILE * SLAB_ROWS, LANES), lambda i: (i, 0))],
        out_specs=pl.BlockSpec(memory_space=pl.ANY),
        out_shape=jax.ShapeDtypeStruct((N_ASSIGN * SLAB_ROWS, LANES), U32),
        scratch_shapes=[pltpu.SemaphoreType.DMA(())],
        compiler_params=pltpu.CompilerParams(dimension_semantics=("arbitrary",), has_side_effects=True),
        name="dispatch",
    )(pos0, pos1, xn_slab)


X_SLOTS = 3
Y_SLOTS = 2


def _experts_kernel(layer, vt_ref, ve_ref, gs_ref, ge_ref, nv_ref, nx_ref, xs_hbm, wg_hbm, wu_hbm, wd_hbm, ys_hbm,
                    xb, yb, wg_st, wu_st, wd_st, wgu_s, wd_s, xsem, ysem, wsem):
    i = pl.program_id(0)
    rows = EXPERT_TILE * SLAB_ROWS
    n_visits = nv_ref[0]
    prev = jnp.maximum(i - 1, 0)
    e = ve_ref[i]
    tile = vt_ref[i]
    valid = i < n_visits
    new_expert = (i == 0) | (e != ve_ref[prev])
    new_tile = (i == 0) | (tile != vt_ref[prev])

    def x_copy(t):
        return pltpu.make_async_copy(xs_hbm.at[pl.ds(pl.multiple_of(t * rows, rows), rows), :],
                                     xb.at[t % X_SLOTS], xsem.at[t % X_SLOTS])

    def y_copy(t):
        return pltpu.make_async_copy(yb.at[t % Y_SLOTS],
                                     ys_hbm.at[pl.ds(pl.multiple_of(t * rows, rows), rows), :], ysem.at[t % Y_SLOTS])

    def w_copies(ex):
        return (pltpu.make_async_copy(wg_hbm.at[layer, ex], wg_st, wsem.at[0]),
                pltpu.make_async_copy(wu_hbm.at[layer, ex], wu_st, wsem.at[1]),
                pltpu.make_async_copy(wd_hbm.at[layer, ex], wd_st, wsem.at[2]))

    @pl.when(i == 0)
    def _():
        x_copy(0).start()
        x_copy(1).start()
        for c in w_copies(e):
            c.start()

    @pl.when(valid & new_tile)
    def _():
        x_copy(tile).wait()

        @pl.when(tile + 2 < N_ROW_TILES)
        def _():
            x_copy(tile + 2).start()

        @pl.when(tile >= 2)
        def _():
            y_copy(tile - 2).wait()

        @pl.when(tile >= 1)
        def _():
            y_copy(tile - 1).start()

    @pl.when(valid & new_expert)
    def _():
        for c in w_copies(e):
            c.wait()
        wgu_s[:, :D_EXPERT] = wg_st[...].astype(BF16)
        wgu_s[:, D_EXPERT:] = wu_st[...].astype(BF16)
        wd_s[...] = wd_st[...].astype(BF16)
        nxt = nx_ref[e]

        @pl.when(nxt != e)
        def _():
            for c in w_copies(nxt):
                c.start(priority=1)

    @pl.when(valid)
    def _():
        xs = tile % X_SLOTS
        ys = tile % Y_SLOTS
        x = _unpack_rows(jnp.concatenate(
            [xb[xs, pl.ds(c, EXPERT_TILE, stride=SLAB_ROWS), :] for c in range(SLAB_ROWS)], axis=1))
        gu = jnp.dot(x.astype(BF16), wgu_s[...], preferred_element_type=F32)
        gate = gu[:, :D_EXPERT]
        act = (gate / (1.0 + jnp.exp(-gate)) * gu[:, D_EXPERT:]).astype(BF16)
        y = _pack_rows(jnp.dot(act, wd_s[...], preferred_element_type=F32))
        r = tile * EXPERT_TILE + lax.broadcasted_iota(jnp.int32, (EXPERT_TILE, 1), 0)
        mine = (r >= gs_ref[e]) & (r < ge_ref[e])

        @pl.when(new_tile)
        def _():
            for c in range(SLAB_ROWS):
                yb[ys, pl.ds(c, EXPERT_TILE, stride=SLAB_ROWS), :] = jnp.where(
                    mine, y[:, c * LANES:(c + 1) * LANES], jnp.uint32(0))

        @pl.when(jnp.logical_not(new_tile))
        def _():
            for c in range(SLAB_ROWS):
                old = yb[ys, pl.ds(c, EXPERT_TILE, stride=SLAB_ROWS), :]
                yb[ys, pl.ds(c, EXPERT_TILE, stride=SLAB_ROWS), :] = jnp.where(mine, y[:, c * LANES:(c + 1) * LANES], old)

        @pl.when(i == n_visits - 1)
        def _():
            y_copy(tile).start()
            y_copy(tile).wait()

            @pl.when(tile >= 1)
            def _():
                y_copy(tile - 1).wait()


def _experts(layer, visit_tile, visit_expert, g_start, g_end, n_visits, next_expert, xs_slab, w_gate, w_up, w_down):
    rows = EXPERT_TILE * SLAB_ROWS
    any_spec = lambda: pl.BlockSpec(memory_space=pl.ANY)
    grid_spec = pltpu.PrefetchScalarGridSpec(
        num_scalar_prefetch=6,
        grid=(N_VISITS,),
        in_specs=[any_spec(), any_spec(), any_spec(), any_spec()],
        out_specs=any_spec(),
        scratch_shapes=[pltpu.VMEM((X_SLOTS, rows, LANES), U32), pltpu.VMEM((Y_SLOTS, rows, LANES), U32),
                        pltpu.VMEM((D_MODEL, D_EXPERT), F32), pltpu.VMEM((D_MODEL, D_EXPERT), F32),
                        pltpu.VMEM((D_EXPERT, D_MODEL), F32),
                        pltpu.VMEM((D_MODEL, 2 * D_EXPERT), BF16), pltpu.VMEM((D_EXPERT, D_MODEL), BF16),
                        pltpu.SemaphoreType.DMA((X_SLOTS,)), pltpu.SemaphoreType.DMA((Y_SLOTS,)),
                        pltpu.SemaphoreType.DMA((3,))],
    )
    return pl.pallas_call(
        functools.partial(_experts_kernel, layer),
        grid_spec=grid_spec,
        out_shape=jax.ShapeDtypeStruct((N_ASSIGN * SLAB_ROWS, LANES), U32),
        compiler_params=pltpu.CompilerParams(dimension_semantics=("arbitrary",), vmem_limit_bytes=48 << 20,
                                             has_side_effects=True),
        name="experts",
    )(visit_tile, visit_expert, g_start, g_end, n_visits, next_expert, xs_slab, w_gate, w_up, w_down)


def _with_expert_rows(pos0_ref, pos1_ref, npos0_ref, npos1_ref, ys_ref, bufs, sem, body):
    i = pl.program_id(0)

    def copies(p0_ref, p1_ref, s):
        for j in range(TOK_TILE):
            yield _row_copy(ys_ref, p0_ref[j], bufs[s][0], j, sem.at[s])
            yield _row_copy(ys_ref, p1_ref[j], bufs[s][1], j, sem.at[s])

    @pl.when(i == 0)
    def _():
        for n, c in enumerate(copies(pos0_ref, pos1_ref, 0)):
            c.start(priority=n % 2)

    for s in range(2):
        @pl.when(i % 2 == s)
        def _():
            for n, c in enumerate(copies(npos0_ref, npos1_ref, 1 - s)):
                c.start(priority=n % 2)
            for c in copies(pos0_ref, pos1_ref, s):
                c.wait()
            body(_unpack_rows(_load_slab(bufs[s][0], TOK_TILE)), _unpack_rows(_load_slab(bufs[s][1], TOK_TILE)))

            @pl.when(i == pl.num_programs(0) - 1)
            def _():
                for c in copies(npos0_ref, npos1_ref, 1 - s):
                    c.wait()


def _moe_residual_and_ple(h_ref, mf_ref, y0, y1, p_ref, gple_ref, wpg_ref, wpu_ref):
    gates = mf_ref[...]
    h = h_ref[...] + (y0 * gates[:, 0:1] + y1 * gates[:, 1:2])
    hn = _rms(h, gple_ref[...]).astype(BF16)
    gate = jnp.dot(hn, wpg_ref[...], preferred_element_type=F32)
    gate = 1.0 / (1.0 + jnp.exp(-gate))
    up = jnp.dot(p_ref[...].astype(BF16), wpu_ref[...], preferred_element_type=F32)
    return h + gate * up


def _rope(x, cos, sin_signed):
    n, width = x.shape
    reps = width // LANES
    cos_t = jnp.concatenate([cos] * reps, axis=1)
    sin_t = jnp.concatenate([sin_signed] * reps, axis=1)
    lane = lax.broadcasted_iota(jnp.int32, (n, width), 1)
    first_half = (lane % HEAD_DIM) < (HEAD_DIM // 2)
    partner = jnp.where(first_half, pltpu.roll(x, width - HEAD_DIM // 2, axis=1), pltpu.roll(x, HEAD_DIM // 2, axis=1))
    return x * cos_t + partner * sin_t


def _combine0_kernel(pos0_ref, pos1_ref, npos0_ref, npos1_ref, h_ref, mf_ref, p_ref, cos_ref, sin_ref, gple_ref, wpg_ref, wpu_ref,
                     gkv_ref, wk_ref, wv_ref, gq_ref, wq_ref, ys_ref,
                     ho_ref, k_ref, v_ref, q_ref, b00, b01, b10, b11, sem):
    def body(y0, y1):
        h = _moe_residual_and_ple(h_ref, mf_ref, y0, y1, p_ref, gple_ref, wpg_ref, wpu_ref)
        ho_ref[...] = h
        cos = cos_ref[...]
        sin = sin_ref[...]
        kvn = _rms(h, gkv_ref[...]).astype(BF16)
        k = jnp.dot(kvn, wk_ref[...], preferred_element_type=F32)
        k_ref[...] = _rope(k, cos, sin).astype(BF16)
        v_ref[...] = jnp.dot(kvn, wv_ref[...], preferred_element_type=F32).astype(BF16)
        qn = _rms(h, gq_ref[...]).astype(BF16)
        q = jnp.dot(qn, wq_ref[...], preferred_element_type=F32)
        q = (_rope(q, cos, sin) * (HEAD_DIM ** -0.5)).astype(BF16)
        for g in range(Q_PER_KV):
            q_ref[g] = q[:, g * KV_DIM:(g + 1) * KV_DIM]

    _with_expert_rows(pos0_ref, pos1_ref, npos0_ref, npos1_ref, ys_ref, ((b00, b01), (b10, b11)), sem, body)


def _combine1_kernel(pos0_ref, pos1_ref, npos0_ref, npos1_ref, h_ref, mf_ref, p_ref, gple_ref, wpg_ref, wpu_ref, gfin_ref, ys_ref,
                     out_ref, b00, b01, b10, b11, sem):
    def body(y0, y1):
        h = _moe_residual_and_ple(h_ref, mf_ref, y0, y1, p_ref, gple_ref, wpg_ref, wpu_ref)
        out_ref[...] = _rms(h, gfin_ref[...])

    _with_expert_rows(pos0_ref, pos1_ref, npos0_ref, npos1_ref, ys_ref, ((b00, b01), (b10, b11)), sem, body)


def _combine_specs():
    last = N_TOK // TOK_TILE - 1
    smem = lambda: pl.BlockSpec((TOK_TILE,), lambda i: (i,), memory_space=pltpu.SMEM)
    smem_next = lambda: pl.BlockSpec((TOK_TILE,), lambda i: (jnp.minimum(i + 1, last),), memory_space=pltpu.SMEM)
    tok = lambda w: pl.BlockSpec((TOK_TILE, w), lambda i: (i, 0))
    full = lambda a, b: pl.BlockSpec((a, b), lambda i: (0, 0))
    scratch = [pltpu.VMEM((TOK_TILE * SLAB_ROWS, LANES), U32) for _ in range(4)] + [pltpu.SemaphoreType.DMA((2,))]
    ple = lambda layer: pl.BlockSpec((None, TOK_TILE, PLE_DIM), lambda i: (layer, i, 0))
    return smem, smem_next, tok, ple, full, scratch


def _combine0(pos0, pos1, h, mf, p, cos_t, sin_t, g_ple, w_pg, w_pu, g_kv, w_k, w_v, g_q, w_q, ys_slab):
    smem, smem_next, tok, ple, full, scratch = _combine_specs()
    return pl.pallas_call(
        _combine0_kernel,
        grid=(N_TOK // TOK_TILE,),
        in_specs=[smem(), smem(), smem_next(), smem_next(), tok(D_MODEL), tok(LANES), ple(0), tok(LANES), tok(LANES),
                  full(1, D_MODEL), full(D_MODEL, D_MODEL), full(PLE_DIM, D_MODEL),
                  full(1, D_MODEL), full(D_MODEL, KV_DIM), full(D_MODEL, KV_DIM),
                  full(1, D_MODEL), full(D_MODEL, D_MODEL),
                  pl.BlockSpec(memory_space=pl.ANY)],
        out_specs=[tok(D_MODEL), tok(KV_DIM), tok(KV_DIM),
                   pl.BlockSpec((Q_PER_KV, TOK_TILE, KV_DIM), lambda i: (0, i, 0))],
        out_shape=[jax.ShapeDtypeStruct((N_TOK, D_MODEL), F32),
                   jax.ShapeDtypeStruct((N_TOK, KV_DIM), BF16),
                   jax.ShapeDtypeStruct((N_TOK, KV_DIM), BF16),
                   jax.ShapeDtypeStruct((Q_PER_KV, N_TOK, KV_DIM), BF16)],
        scratch_shapes=scratch,
        compiler_params=pltpu.CompilerParams(dimension_semantics=("arbitrary",), vmem_limit_bytes=56 << 20),
        name="combine0",
    )(pos0, pos1, pos0, pos1, h, mf, p, cos_t, sin_t, g_ple, w_pg, w_pu, g_kv, w_k, w_v, g_q, w_q, ys_slab)


def _combine1(pos0, pos1, h, mf, p, g_ple, w_pg, w_pu, g_fin, ys_slab):
    smem, smem_next, tok, ple, full, scratch = _combine_specs()
    return pl.pallas_call(
        _combine1_kernel,
        grid=(N_TOK // TOK_TILE,),
        in_specs=[smem(), smem(), smem_next(), smem_next(), tok(D_MODEL), tok(LANES), ple(1),
                  full(1, D_MODEL), full(D_MODEL, D_MODEL), full(PLE_DIM, D_MODEL), full(1, D_MODEL),
                  pl.BlockSpec(memory_space=pl.ANY)],
        out_specs=tok(D_MODEL),
        out_shape=jax.ShapeDtypeStruct((N_TOK, D_MODEL), F32),
        scratch_shapes=scratch,
        compiler_params=pltpu.CompilerParams(dimension_semantics=("arbitrary",), vmem_limit_bytes=48 << 20),
        name="combine1",
    )(pos0, pos1, pos0, pos1, h, mf, p, g_ple, w_pg, w_pu, g_fin, ys_slab)


def _attn1_kernel(sink_ref, q_ref, kc_ref, kp_ref, vc_ref, vp_ref, h_ref, wo_ref, gffn_ref, wr_ref, br_ref, up_ref,
                  ho_ref, xn_ref, mi_ref, mf_ref, cnt_ref, run_ref, k_s, v_s, o_s):
    i = pl.program_id(0)
    steps_per_seq = SEQ // ATT_TILE

    @pl.when(i == 0)
    def _():
        run_ref[...] = jnp.zeros_like(run_ref)

    k_s[:ATT_BLOCK, :] = kp_ref[...]
    k_s[ATT_BLOCK:, :] = kc_ref[...]
    v_s[:ATT_BLOCK, :] = vp_ref[...]
    v_s[ATT_BLOCK:, :] = vc_ref[...]
    seq_start = (i % steps_per_seq) == 0

    n_rows = Q_PER_KV * ATT_BLOCK
    assert WINDOW == ATT_BLOCK
    row = lax.broadcasted_iota(jnp.int32, (n_rows, ATT_BLOCK), 0)
    col = lax.broadcasted_iota(jnp.int32, (n_rows, ATT_BLOCK), 1)
    upper = col > (row % ATT_BLOCK)
    head_of_lane = lax.broadcasted_iota(jnp.int32, (2 * ATT_BLOCK, KV_DIM), 1) // HEAD_DIM
    grp_of_row = lax.broadcasted_iota(jnp.int32, (n_rows, 1), 0) // ATT_BLOCK

    def block(b, carry):
        r0 = pl.multiple_of(b * ATT_BLOCK, ATT_BLOCK)
        has_prev = jnp.logical_not(seq_start & (b == 0))
        kcat = k_s[pl.ds(r0, 2 * ATT_BLOCK), :]
        vcat = v_s[pl.ds(r0, 2 * ATT_BLOCK), :]
        qs = jnp.concatenate([q_ref[g, pl.ds(r0, ATT_BLOCK), :] for g in range(Q_PER_KV)], axis=0)
        o = jnp.zeros((n_rows, KV_DIM), F32)
        for kh in range(N_KV_HEADS):
            k_h = jnp.where(head_of_lane == kh, kcat, jnp.zeros_like(kcat))
            v_h = jnp.where(head_of_lane == kh, vcat, jnp.zeros_like(vcat))
            s = lax.dot_general(qs, k_h, (((1,), (1,)), ((), ())), preferred_element_type=F32)
            s = jnp.where(upper, jnp.where(has_prev, s[:, :ATT_BLOCK], NEG_INF), s[:, ATT_BLOCK:])
            sink = jnp.zeros((n_rows, 1), F32)
            for g in range(Q_PER_KV):
                sink = jnp.where(grp_of_row == g, sink_ref[kh * Q_PER_KV + g], sink)
            m = jnp.maximum(jnp.max(s, axis=1, keepdims=True), sink)
            e = jnp.exp(s - m)
            denom = jnp.sum(e, axis=1, keepdims=True) + jnp.exp(sink - m)
            pr = e / denom
            pr = jnp.concatenate([jnp.where(upper, pr, 0.0), jnp.where(upper, 0.0, pr)], axis=1).astype(BF16)
            o = o + jnp.dot(pr, v_h, preferred_element_type=F32)
        for g in range(Q_PER_KV):
            o_s[g, pl.ds(r0, ATT_BLOCK), :] = o[g * ATT_BLOCK:(g + 1) * ATT_BLOCK, :].astype(BF16)
        return carry

    lax.fori_loop(0, ATT_TILE // ATT_BLOCK, block, 0)
    o_cat = jnp.concatenate([o_s[g] for g in range(Q_PER_KV)], axis=1)
    h = h_ref[...] + jnp.dot(o_cat, wo_ref[...], preferred_element_type=F32)
    ho_ref[...] = h
    xn = _rms(h, gffn_ref[...])
    _store_slab(xn_ref, _pack_rows(xn))
    _route_and_rank(xn, wr_ref, br_ref, up_ref, run_ref, mi_ref, mf_ref, cnt_ref)


def _attn1(sinks, q, k, v, h, w_o, g_ffn, w_r, b_r, upper):
    blocks_per_step = ATT_TILE // ATT_BLOCK
    cur = lambda w: pl.BlockSpec((ATT_TILE, w), lambda i: (i, 0))
    prev = lambda w: pl.BlockSpec((ATT_BLOCK, w), lambda i: (jnp.maximum(i * blocks_per_step - 1, 0), 0))
    full = lambda a, b: pl.BlockSpec((a, b), lambda i: (0, 0))
    return pl.pallas_call(
        _attn1_kernel,
        grid=(N_TOK // ATT_TILE,),
        in_specs=[pl.BlockSpec(memory_space=pltpu.SMEM),
                  pl.BlockSpec((Q_PER_KV, ATT_TILE, KV_DIM), lambda i: (0, i, 0)),
                  cur(KV_DIM), prev(KV_DIM), cur(KV_DIM), prev(KV_DIM), cur(D_MODEL),
                  full(D_MODEL, D_MODEL), full(1, D_MODEL), full(D_MODEL, 2 * LANES), full(1, LANES),
                  full(ATT_TILE, ATT_TILE)],
        out_specs=[cur(D_MODEL),
                   pl.BlockSpec((ATT_TILE * SLAB_ROWS, LANES), lambda i: (i, 0)),
                   pl.BlockSpec((SUBLANES, ATT_TILE), lambda i: (0, i)), cur(LANES),
                   pl.BlockSpec((LANES, LANES), lambda i: (0, 0))],
        out_shape=[jax.ShapeDtypeStruct((N_TOK, D_MODEL), F32),
                   jax.ShapeDtypeStruct((N_TOK * SLAB_ROWS, LANES), U32),
                   jax.ShapeDtypeStruct((SUBLANES, N_TOK), jnp.int32),
                   jax.ShapeDtypeStruct((N_TOK, LANES), F32),
                   jax.ShapeDtypeStruct((LANES, LANES), F32)],
        scratch_shapes=[pltpu.VMEM((LANES, 1), F32),
                        pltpu.VMEM((ATT_TILE + ATT_BLOCK, KV_DIM), BF16),
                        pltpu.VMEM((ATT_TILE + ATT_BLOCK, KV_DIM), BF16),
                        pltpu.VMEM((Q_PER_KV, ATT_TILE, KV_DIM), BF16)],
        compiler_params=pltpu.CompilerParams(dimension_semantics=("arbitrary",), vmem_limit_bytes=48 << 20),
        name="attn1",
    )(sinks, q, k, k, v, v, h, w_o, g_ffn, w_r, b_r, upper)


def _plan_kernel(cnt_ref, meta_ref, pos_ref, vt_ref, ve_ref, gs_ref, ge_ref, nv_ref, nx_ref):
    def per_expert(e, carry):
        off, v = carry
        c = cnt_ref[e]
        gs_ref[e] = off
        ge_ref[e] = off + c
        first = off // EXPERT_TILE
        n_vis = jnp.where(c > 0, (off + c - 1) // EXPERT_TILE - first + 1, 0)

        def visit(j, _):
            vt_ref[v + j] = first + j
            ve_ref[v + j] = e
            return 0

        lax.fori_loop(0, n_vis, visit, 0)
        return off + c, v + n_vis

    _, total = lax.fori_loop(0, N_EXPERTS, per_expert, (jnp.int32(0), jnp.int32(0)))
    nv_ref[0] = total

    def pad(j, _):
        vt_ref[j] = vt_ref[total - 1]
        ve_ref[j] = ve_ref[total - 1]
        return 0

    lax.fori_loop(total, N_VISITS, pad, 0)

    def link(k, nxt):
        e = N_EXPERTS - 1 - k
        nx_ref[e] = jnp.where(nxt >= 0, nxt, e)
        return jnp.where(cnt_ref[e] > 0, e, nxt)

    lax.fori_loop(0, N_EXPERTS, link, jnp.int32(-1))

    meta = meta_ref[...]
    start = jnp.zeros_like(meta)
    for e in range(N_EXPERTS):
        start = jnp.where(meta == e, gs_ref[e], start)
    pos_ref[...] = start + pltpu.roll(meta, SUBLANES - 2, axis=0)


def _routing_tables(meta_i, counts_f):
    counts = counts_f[:N_EXPERTS, 0].astype(jnp.int32)
    smem = lambda: pl.BlockSpec(memory_space=pltpu.SMEM)
    i32 = lambda n: jax.ShapeDtypeStruct((n,), jnp.int32)
    pos, vt, ve, gs, ge, nv, nx = pl.pallas_call(
        _plan_kernel,
        in_specs=[smem(), pl.BlockSpec(memory_space=pltpu.VMEM)],
        out_specs=[pl.BlockSpec(memory_space=pltpu.VMEM), smem(), smem(), smem(), smem(), smem(), smem()],
        out_shape=[jax.ShapeDtypeStruct((SUBLANES, N_TOK), jnp.int32),
                   i32(N_VISITS), i32(N_VISITS), i32(N_EXPERTS), i32(N_EXPERTS), i32(1), i32(N_EXPERTS)],
        name="plan",
    )(counts, meta_i)
    return pos[0], pos[1], vt, ve, gs, ge, nv, nx


def _router_weights(w_grp, b_grp, w_exp, b_exp):
    w = jnp.zeros((D_MODEL, LANES), F32)
    w = w.at[:, :N_EXPERTS].set(w_exp).at[:, GROUP_LANE0:GROUP_LANE0 + N_GROUPS].set(w_grp)
    b = jnp.zeros((1, LANES), F32)
    b = b.at[0, :N_EXPERTS].set(b_exp).at[0, GROUP_LANE0:GROUP_LANE0 + N_GROUPS].set(b_grp)
    w_hi = w.astype(BF16)
    w_lo = (w - w_hi.astype(F32)).astype(BF16)
    return jnp.concatenate([w_hi, w_lo], axis=1), b


def _rope_tables(positions):
    inv = ROPE_THETA ** (-jnp.arange(0, HEAD_DIM, 2, dtype=F32) / HEAD_DIM)
    per_row = LANES // inv.shape[0]
    pos = positions.astype(F32).reshape(N_TOK // per_row, per_row)
    ang = jnp.repeat(pos, inv.shape[0], axis=1) * jnp.tile(inv, per_row)
    cos = jnp.cos(ang).reshape(N_TOK, inv.shape[0])
    sin = jnp.sin(ang).reshape(N_TOK, inv.shape[0])
    cos_t = jnp.concatenate([cos, cos, cos, cos], axis=1)
    sin_t = jnp.concatenate([-sin, sin, -sin, sin], axis=1)
    return cos_t, sin_t


def _moe(layer, meta_i, counts_f, xn_slab, w_gate, w_up, w_down):
    pos0, pos1, vt, ve, gs, ge, nv, nx = _routing_tables(meta_i, counts_f)
    xs_slab = _dispatch(pos0, pos1, xn_slab)
    ys_slab = _experts(layer, vt, ve, gs, ge, nv, nx, xs_slab, w_gate, w_up, w_down)
    return pos0, pos1, ys_slab


def kernel(x, p, positions, g_mix, g_ffn, pool_w, pool_scale, g_kv, w_k, w_v, w_q, w_o, sinks, w_group_router, b_group_router, w_expert_router, b_expert_router, w_exp_gate, w_exp_up, w_exp_down, g_ple, w_ple_gate, w_ple_up, g_final):
    row = lambda a: a.reshape(1, -1)
    x2 = x.reshape(N_TOK, D_MODEL)
    p2 = p.reshape(2, N_TOK, PLE_DIM)
    cos_t, sin_t = _rope_tables(positions)
    assert MIX_TILE == ATT_TILE
    upper = jnp.triu(jnp.ones((MIX_TILE, MIX_TILE), BF16), k=1)
    w_q_perm = w_q[0].reshape(D_MODEL, N_KV_HEADS, Q_PER_KV, HEAD_DIM).transpose(0, 2, 1, 3).reshape(D_MODEL, D_MODEL)
    w_o_perm = w_o[0].reshape(N_KV_HEADS, Q_PER_KV, HEAD_DIM, D_MODEL).transpose(1, 0, 2, 3).reshape(D_MODEL, D_MODEL)

    wr0, br0 = _router_weights(w_group_router[0], b_group_router[0], w_expert_router[0], b_expert_router[0])
    h1, xn_slab, meta_i, meta_f, counts = _mixer0(
        x2, row(g_mix[0]), pool_w[0].astype(BF16), row(pool_scale[0]), row(g_ffn[0]), wr0, br0, upper)
    pos0, pos1, ys_slab = _moe(0, meta_i, counts, xn_slab, w_exp_gate, w_exp_up, w_exp_down)
    h3, k, v, q = _combine0(
        pos0, pos1, h1, meta_f, p2, cos_t, sin_t, row(g_ple[0]), w_ple_gate[0].astype(BF16),
        w_ple_up[0].astype(BF16), row(g_kv), w_k.astype(BF16), w_v.astype(BF16), row(g_mix[1]),
        w_q_perm.astype(BF16), ys_slab)

    wr1, br1 = _router_weights(w_group_router[1], b_group_router[1], w_expert_router[1], b_expert_router[1])
    h4, xn_slab, meta_i, meta_f, counts = _attn1(
        sinks[0], q, k, v, h3, w_o_perm.astype(BF16), row(g_ffn[1]), wr1, br1, upper)
    pos0, pos1, ys_slab = _moe(1, meta_i, counts, xn_slab, w_exp_gate, w_exp_up, w_exp_down)
    out = _combine1(pos0, pos1, h4, meta_f, p2, row(g_ple[1]), w_ple_gate[1].astype(BF16),
                    w_ple_up[1].astype(BF16), row(g_final), ys_slab)
    return out.reshape(BATCH, SEQ, D_MODEL)
```

```python
import functools

import jax
import jax.numpy as jnp
from jax import lax
from jax.experimental import pallas as pl
from jax.experimental.pallas import tpu as pltpu

D_MODEL = 1024
BATCH = 2
SEQ = 8192
N_TOK = BATCH * SEQ
POOL_WINDOWS = (2, 4, 8, 16)
POOL_GROUP = D_MODEL // len(POOL_WINDOWS)
POOL_HALO = 16
HEAD_DIM = 64
N_Q_HEADS = 16
N_KV_HEADS = 4
Q_PER_KV = 4
KV_DIM = N_KV_HEADS * HEAD_DIM
WINDOW = 128
ATT_BLOCK = 128
ROPE_THETA = 10000.0
N_GROUPS = 4
EXPERTS_PER_GROUP = 8
N_EXPERTS = 32
D_EXPERT = 512
PLE_DIM = 256
EPS = 1e-6

LANES = 128
SUBLANES = 8
HALF = D_MODEL // 2
SLAB_ROWS = HALF // LANES
N_ASSIGN = 2 * N_TOK

TOK_TILE = 256
MIX_TILE = 512
ATT_TILE = 512
DISPATCH_TILE = 512
EXPERT_TILE = 256
N_ROW_TILES = N_ASSIGN // EXPERT_TILE
N_VISITS = N_ROW_TILES + N_EXPERTS - 1
GROUP_LANE0 = N_EXPERTS

F32 = jnp.float32
BF16 = jnp.bfloat16
U32 = jnp.uint32
NEG_INF = float("-inf")


def _rms(x, g):
    return x * lax.rsqrt(jnp.mean(x * x, axis=-1, keepdims=True) + EPS) * g


def _pack_rows(val):
    lo = pltpu.bitcast(val[:, :HALF].astype(BF16).astype(F32), U32)
    hi = pltpu.bitcast(val[:, HALF:].astype(BF16).astype(F32), U32)
    return hi | (lo >> 16)


def _unpack_rows(word):
    lo = pltpu.bitcast(word << 16, F32)
    hi = pltpu.bitcast(word & jnp.uint32(0xFFFF0000), F32)
    return jnp.concatenate([lo, hi], axis=1)


def _store_slab(ref, word):
    n = word.shape[0]
    for c in range(SLAB_ROWS):
        ref[pl.ds(c, n, stride=SLAB_ROWS), :] = word[:, c * LANES:(c + 1) * LANES]


def _load_slab(ref, n):
    return jnp.concatenate([ref[pl.ds(c, n, stride=SLAB_ROWS), :] for c in range(SLAB_ROWS)], axis=1)


def _route_and_rank(xn, wr_ref, br_ref, up_ref, run_ref, mi_ref, mf_ref, cnt_ref):
    n = xn.shape[0]
    x_hi = xn.astype(BF16)
    x_lo = (xn - x_hi.astype(F32)).astype(BF16)
    hh_hl = jnp.dot(x_hi, wr_ref[...], preferred_element_type=F32)
    lh = jnp.dot(x_lo, wr_ref[:, :LANES], preferred_element_type=F32)
    logits = (hh_hl[:, :LANES] + (hh_hl[:, LANES:] + lh) + br_ref[...]).T
    big = float(LANES)

    grow = lax.broadcasted_iota(jnp.int32, (SUBLANES, n), 0).astype(F32)
    gl = jnp.where(grow < N_GROUPS, logits[GROUP_LANE0:GROUP_LANE0 + SUBLANES, :], NEG_INF)
    gmax = jnp.max(gl, axis=0, keepdims=True)
    gidx = jnp.min(jnp.where(gl == gmax, grow, big), axis=0, keepdims=True)
    pg = 1.0 / jnp.sum(jnp.exp(gl - gmax), axis=0, keepdims=True)

    erow = lax.broadcasted_iota(jnp.int32, (N_EXPERTS, n), 0).astype(F32)
    lo = gidx * EXPERTS_PER_GROUP
    el = jnp.where((erow >= lo) & (erow < lo + EXPERTS_PER_GROUP), logits[:N_EXPERTS, :], NEG_INF)
    m1 = jnp.max(el, axis=0, keepdims=True)
    i1 = jnp.min(jnp.where(el == m1, erow, big), axis=0, keepdims=True)
    el2 = jnp.where(erow == i1, NEG_INF, el)
    m2 = jnp.max(el2, axis=0, keepdims=True)
    i2 = jnp.min(jnp.where(el2 == m2, erow, big), axis=0, keepdims=True)
    z = jnp.exp(m2 - m1)
    gate0 = pg / (1.0 + z)
    gate1 = pg * z / (1.0 + z)

    row = lax.broadcasted_iota(jnp.int32, (LANES, n), 0).astype(F32)
    oh0 = jnp.where(row == i1, 1.0, 0.0)
    oh1 = jnp.where(row == i2, 1.0, 0.0)
    oh = oh0 + oh1
    prefix = jnp.dot(oh.astype(BF16), up_ref[...], preferred_element_type=F32)
    tot = prefix + run_ref[...]
    rank0 = jnp.sum(oh0 * tot, axis=0, keepdims=True)
    rank1 = jnp.sum(oh1 * tot, axis=0, keepdims=True)
    new_run = run_ref[...] + jnp.sum(oh, axis=1, keepdims=True)
    run_ref[...] = new_run
    cnt_ref[...] = jnp.broadcast_to(new_run, cnt_ref.shape)

    r8 = lax.broadcasted_iota(jnp.int32, (SUBLANES, n), 0)
    meta = jnp.where(r8 == 0, i1, jnp.where(r8 == 1, i2, jnp.where(r8 == 2, rank0, jnp.where(r8 == 3, rank1, 0.0))))
    mi_ref[...] = meta.astype(jnp.int32)
    mf_ref[...] = jnp.where(row == 0.0, gate0, jnp.where(row == 1.0, gate1, 0.0)).T


def _mixer0_kernel(x_ref, halo_ref, gmix_ref, poolw_ref, pscale_ref, gffn_ref, wr_ref, br_ref, up_ref,
                   h_ref, xn_ref, mi_ref, mf_ref, cnt_ref, run_ref):
    i = pl.program_id(0)
    tiles_per_seq = SEQ // MIX_TILE

    @pl.when(i == 0)
    def _():
        run_ref[...] = jnp.zeros_like(run_ref)

    seq_tile = i % tiles_per_seq
    x = x_ref[...]
    g = gmix_ref[...]
    hn = _rms(x, g)
    halo = jnp.where(seq_tile == 0, 0.0, _rms(halo_ref[...], g))
    full = jnp.concatenate([halo, hn], axis=0)
    t = (seq_tile * MIX_TILE + lax.broadcasted_iota(jnp.int32, (MIX_TILE, 1), 0) + 1).astype(F32)
    outs = []
    for gi, w in enumerate(POOL_WINDOWS):
        acc = full[:, gi * POOL_GROUP:(gi + 1) * POOL_GROUP]
        shift = 1
        while shift < w:
            acc = acc + pltpu.roll(acc, shift, axis=0)
            shift *= 2
        win = acc[POOL_HALO:, :]
        mean = win / jnp.minimum(t, float(w))
        dlt = (mean - hn[:, gi * POOL_GROUP:(gi + 1) * POOL_GROUP]).astype(BF16)
        outs.append(jnp.dot(dlt, poolw_ref[gi], preferred_element_type=F32))
    y = jnp.concatenate(outs, axis=1)
    h = x + y * pscale_ref[...]
    h_ref[...] = h
    xn = _rms(h, gffn_ref[...])
    _store_slab(xn_ref, _pack_rows(xn))
    _route_and_rank(xn, wr_ref, br_ref, up_ref, run_ref, mi_ref, mf_ref, cnt_ref)


def _mixer0(x, g_mix, pool_w, pool_scale, g_ffn, w_r, b_r, upper):
    n_tiles = N_TOK // MIX_TILE
    halo_per_tile = MIX_TILE // POOL_HALO
    vec = lambda: pl.BlockSpec((1, D_MODEL), lambda i: (0, 0))
    return pl.pallas_call(
        _mixer0_kernel,
        grid=(n_tiles,),
        in_specs=[
            pl.BlockSpec((MIX_TILE, D_MODEL), lambda i: (i, 0)),
            pl.BlockSpec((POOL_HALO, D_MODEL), lambda i: (jnp.maximum(i * halo_per_tile - 1, 0), 0)),
            vec(),
            pl.BlockSpec((len(POOL_WINDOWS), POOL_GROUP, POOL_GROUP), lambda i: (0, 0, 0)),
            vec(), vec(),
            pl.BlockSpec((D_MODEL, 2 * LANES), lambda i: (0, 0)),
            pl.BlockSpec((1, LANES), lambda i: (0, 0)),
            pl.BlockSpec((MIX_TILE, MIX_TILE), lambda i: (0, 0)),
        ],
        out_specs=[
            pl.BlockSpec((MIX_TILE, D_MODEL), lambda i: (i, 0)),
            pl.BlockSpec((MIX_TILE * SLAB_ROWS, LANES), lambda i: (i, 0)),
            pl.BlockSpec((SUBLANES, MIX_TILE), lambda i: (0, i)),
            pl.BlockSpec((MIX_TILE, LANES), lambda i: (i, 0)),
            pl.BlockSpec((LANES, LANES), lambda i: (0, 0)),
        ],
        out_shape=[
            jax.ShapeDtypeStruct((N_TOK, D_MODEL), F32),
            jax.ShapeDtypeStruct((N_TOK * SLAB_ROWS, LANES), U32),
            jax.ShapeDtypeStruct((SUBLANES, N_TOK), jnp.int32),
            jax.ShapeDtypeStruct((N_TOK, LANES), F32),
            jax.ShapeDtypeStruct((LANES, LANES), F32),
        ],
        scratch_shapes=[pltpu.VMEM((LANES, 1), F32)],
        compiler_params=pltpu.CompilerParams(dimension_semantics=("arbitrary",), vmem_limit_bytes=48 << 20),
        name="mixer0",
    )(x, x, g_mix, pool_w, pool_scale, g_ffn, w_r, b_r, upper)


def _row_copy(src, src_row, dst, dst_row, sem):
    return pltpu.make_async_copy(
        src.at[pl.ds(pl.multiple_of(src_row * SLAB_ROWS, SLAB_ROWS), SLAB_ROWS), :],
        dst.at[pl.ds(pl.multiple_of(dst_row * SLAB_ROWS, SLAB_ROWS), SLAB_ROWS), :],
        sem)


def _dispatch_kernel(pos0_ref, pos1_ref, src_ref, dst_ref, sem):
    unroll = 8

    def issue(jo, carry):
        for u in range(unroll):
            j = jo * unroll + u
            _row_copy(src_ref, j, dst_ref, pos0_ref[j], sem).start(priority=0)
            _row_copy(src_ref, j, dst_ref, pos1_ref[j], sem).start(priority=1)
        return carry

    lax.fori_loop(0, DISPATCH_TILE // unroll, issue, 0)

    def drain(jo, carry):
        for u in range(2 * unroll):
            _row_copy(src_ref, 0, dst_ref, 0, sem).wait()
        return carry

    lax.fori_loop(0, DISPATCH_TILE // unroll, drain, 0)


def _dispatch(pos0, pos1, xn_slab):
    smem = lambda: pl.BlockSpec((DISPATCH_TILE,), lambda i: (i,), memory_space=pltpu.SMEM)
    return pl.pallas_call(
        _dispatch_kernel,
        grid=(N_TOK // DISPATCH_TILE,),
        in_specs=[smem(), smem(), pl.BlockSpec((DISPATCH_TILE * SLAB_ROWS, LANES), lambda i: (i, 0))],
        out_specs=pl.BlockSpec(memory_space=pl.ANY),
        out_shape=jax.ShapeDtypeStruct((N_ASSIGN * SLAB_ROWS, LANES), U32),
        scratch_shapes=[pltpu.SemaphoreType.DMA(())],
        compiler_params=pltpu.CompilerParams(dimension_semantics=("arbitrary",), has_side_effects=True),
        name="dispatch",
    )(pos0, pos1, xn_slab)


X_SLOTS = 3
Y_SLOTS = 2


def _experts_kernel(layer, vt_ref, ve_ref, gs_ref, ge_ref, nv_ref, nx_ref, xs_hbm, wg_hbm, wu_hbm, wd_hbm, ys_hbm,
                    xb, yb, wg_st, wu_st, wd_st, wgu_s, wd_s, xsem, ysem, wsem):
    rows = EXPERT_TILE * SLAB_ROWS
    n_visits = nv_ref[0]

    def x_copy(t):
        return pltpu.make_async_copy(xs_hbm.at[pl.ds(pl.multiple_of(t * rows, rows), rows), :],
                                     xb.at[t % X_SLOTS], xsem.at[t % X_SLOTS])

    def y_copy(t):
        return pltpu.make_async_copy(yb.at[t % Y_SLOTS],
                                     ys_hbm.at[pl.ds(pl.multiple_of(t * rows, rows), rows), :], ysem.at[t % Y_SLOTS])

    def w_copies(ex):
        return (pltpu.make_async_copy(wg_hbm.at[layer, ex], wg_st, wsem.at[0]),
                pltpu.make_async_copy(wu_hbm.at[layer, ex], wu_st, wsem.at[1]),
                pltpu.make_async_copy(wd_hbm.at[layer, ex], wd_st, wsem.at[2]))

    x_copy(0).start()
    x_copy(1).start()
    for c in w_copies(ve_ref[0]):
        c.start()

    def visit(i, carry):
        prev = jnp.maximum(i - 1, 0)
        e = ve_ref[i]
        tile = vt_ref[i]
        new_expert = (i == 0) | (e != ve_ref[prev])
        new_tile = (i == 0) | (tile != vt_ref[prev])

        @pl.when(new_tile)
        def _():
            x_copy(tile).wait()

            @pl.when(tile + 2 < N_ROW_TILES)
            def _():
                x_copy(tile + 2).start()

            @pl.when(tile >= 2)
            def _():
                y_copy(tile - 2).wait()

            @pl.when(tile >= 1)
            def _():
                y_copy(tile - 1).start()

        @pl.when(new_expert)
        def _():
            for c in w_copies(e):
                c.wait()
            wgu_s[:, :D_EXPERT] = wg_st[...].astype(BF16)
            wgu_s[:, D_EXPERT:] = wu_st[...].astype(BF16)
            wd_s[...] = wd_st[...].astype(BF16)
            nxt = nx_ref[e]

            @pl.when(nxt != e)
            def _():
                for c in w_copies(nxt):
                    c.start()

        xs = tile % X_SLOTS
        ys = tile % Y_SLOTS
        x = _unpack_rows(jnp.concatenate(
            [xb[xs, pl.ds(c, EXPERT_TILE, stride=SLAB_ROWS), :] for c in range(SLAB_ROWS)], axis=1))
        gu = jnp.dot(x.astype(BF16), wgu_s[...], preferred_element_type=F32)
        gate = gu[:, :D_EXPERT]
        act = (gate / (1.0 + jnp.exp(-gate)) * gu[:, D_EXPERT:]).astype(BF16)
        y = _pack_rows(jnp.dot(act, wd_s[...], preferred_element_type=F32))
        r = tile * EXPERT_TILE + lax.broadcasted_iota(jnp.int32, (EXPERT_TILE, 1), 0)
        mine = (r >= gs_ref[e]) & (r < ge_ref[e])

        @pl.when(new_tile)
        def _():
            for c in range(SLAB_ROWS):
                yb[ys, pl.ds(c, EXPERT_TILE, stride=SLAB_ROWS), :] = jnp.where(
                    mine, y[:, c * LANES:(c + 1) * LANES], jnp.uint32(0))

        @pl.when(jnp.logical_not(new_tile))
        def _():
            for c in range(SLAB_ROWS):
                old = yb[ys, pl.ds(c, EXPERT_TILE, stride=SLAB_ROWS), :]
                yb[ys, pl.ds(c, EXPERT_TILE, stride=SLAB_ROWS), :] = jnp.where(mine, y[:, c * LANES:(c + 1) * LANES], old)

        return carry

    lax.fori_loop(0, n_visits, visit, 0)
    last_tile = vt_ref[n_visits - 1]
    y_copy(last_tile).start()
    y_copy(last_tile).wait()

    @pl.when(last_tile >= 1)
    def _():
        y_copy(last_tile - 1).wait()


def _experts(layer, visit_tile, visit_expert, g_start, g_end, n_visits, next_expert, xs_slab, w_gate, w_up, w_down):
    rows = EXPERT_TILE * SLAB_ROWS
    any_spec = lambda: pl.BlockSpec(memory_space=pl.ANY)
    grid_spec = pltpu.PrefetchScalarGridSpec(
        num_scalar_prefetch=6,
        grid=(1,),
        in_specs=[any_spec(), any_spec(), any_spec(), any_spec()],
        out_specs=any_spec(),
        scratch_shapes=[pltpu.VMEM((X_SLOTS, rows, LANES), U32), pltpu.VMEM((Y_SLOTS, rows, LANES), U32),
                        pltpu.VMEM((D_MODEL, D_EXPERT), F32), pltpu.VMEM((D_MODEL, D_EXPERT), F32),
                        pltpu.VMEM((D_EXPERT, D_MODEL), F32),
                        pltpu.VMEM((D_MODEL, 2 * D_EXPERT), BF16), pltpu.VMEM((D_EXPERT, D_MODEL), BF16),
                        pltpu.SemaphoreType.DMA((X_SLOTS,)), pltpu.SemaphoreType.DMA((Y_SLOTS,)),
                        pltpu.SemaphoreType.DMA((3,))],
    )
    return pl.pallas_call(
        functools.partial(_experts_kernel, layer),
        grid_spec=grid_spec,
        out_shape=jax.ShapeDtypeStruct((N_ASSIGN * SLAB_ROWS, LANES), U32),
        compiler_params=pltpu.CompilerParams(dimension_semantics=("arbitrary",), vmem_limit_bytes=48 << 20,
                                             has_side_effects=True),
        name="experts",
    )(visit_tile, visit_expert, g_start, g_end, n_visits, next_expert, xs_slab, w_gate, w_up, w_down)


def _with_expert_rows(pos0_ref, pos1_ref, npos0_ref, npos1_ref, ys_ref, bufs, sem, body):
    i = pl.program_id(0)

    def copies(p0_ref, p1_ref, s):
        for j in range(TOK_TILE):
            yield _row_copy(ys_ref, p0_ref[j], bufs[s][0], j, sem.at[s])
            yield _row_copy(ys_ref, p1_ref[j], bufs[s][1], j, sem.at[s])

    @pl.when(i == 0)
    def _():
        for n, c in enumerate(copies(pos0_ref, pos1_ref, 0)):
            c.start(priority=n % 2)

    for s in range(2):
        @pl.when(i % 2 == s)
        def _():
            for n, c in enumerate(copies(npos0_ref, npos1_ref, 1 - s)):
                c.start(priority=n % 2)
            for c in copies(pos0_ref, pos1_ref, s):
                c.wait()
            body(_unpack_rows(_load_slab(bufs[s][0], TOK_TILE)), _unpack_rows(_load_slab(bufs[s][1], TOK_TILE)))

            @pl.when(i == pl.num_programs(0) - 1)
            def _():
                for c in copies(npos0_ref, npos1_ref, 1 - s):
                    c.wait()


def _moe_residual_and_ple(h_ref, mf_ref, y0, y1, p_ref, gple_ref, wpg_ref, wpu_ref):
    gates = mf_ref[...]
    h = h_ref[...] + (y0 * gates[:, 0:1] + y1 * gates[:, 1:2])
    hn = _rms(h, gple_ref[...]).astype(BF16)
    gate = jnp.dot(hn, wpg_ref[...], preferred_element_type=F32)
    gate = 1.0 / (1.0 + jnp.exp(-gate))
    up = jnp.dot(p_ref[...].astype(BF16), wpu_ref[...], preferred_element_type=F32)
    return h + gate * up


def _rope(x, cos, sin_signed):
    n, width = x.shape
    reps = width // LANES
    cos_t = jnp.concatenate([cos] * reps, axis=1)
    sin_t = jnp.concatenate([sin_signed] * reps, axis=1)
    lane = lax.broadcasted_iota(jnp.int32, (n, width), 1)
    first_half = (lane % HEAD_DIM) < (HEAD_DIM // 2)
    partner = jnp.where(first_half, pltpu.roll(x, width - HEAD_DIM // 2, axis=1), pltpu.roll(x, HEAD_DIM // 2, axis=1))
    return x * cos_t + partner * sin_t


def _combine0_kernel(pos0_ref, pos1_ref, npos0_ref, npos1_ref, h_ref, mf_ref, p_ref, cos_ref, sin_ref, gple_ref, wpg_ref, wpu_ref,
                     gkv_ref, wk_ref, wv_ref, gq_ref, wq_ref, ys_ref,
                     ho_ref, k_ref, v_ref, q_ref, b00, b01, b10, b11, sem):
    def body(y0, y1):
        h = _moe_residual_and_ple(h_ref, mf_ref, y0, y1, p_ref, gple_ref, wpg_ref, wpu_ref)
        ho_ref[...] = h
        cos = cos_ref[...]
        sin = sin_ref[...]
        kvn = _rms(h, gkv_ref[...]).astype(BF16)
        k = jnp.dot(kvn, wk_ref[...], preferred_element_type=F32)
        k_ref[...] = _rope(k, cos, sin).astype(BF16)
        v_ref[...] = jnp.dot(kvn, wv_ref[...], preferred_element_type=F32).astype(BF16)
        qn = _rms(h, gq_ref[...]).astype(BF16)
        q = jnp.dot(qn, wq_ref[...], preferred_element_type=F32)
        q = (_rope(q, cos, sin) * (HEAD_DIM ** -0.5)).astype(BF16)
        for g in range(Q_PER_KV):
            q_ref[g] = q[:, g * KV_DIM:(g + 1) * KV_DIM]

    _with_expert_rows(pos0_ref, pos1_ref, npos0_ref, npos1_ref, ys_ref, ((b00, b01), (b10, b11)), sem, body)


def _combine1_kernel(pos0_ref, pos1_ref, npos0_ref, npos1_ref, h_ref, mf_ref, p_ref, gple_ref, wpg_ref, wpu_ref, gfin_ref, ys_ref,
                     out_ref, b00, b01, b10, b11, sem):
    def body(y0, y1):
        h = _moe_residual_and_ple(h_ref, mf_ref, y0, y1, p_ref, gple_ref, wpg_ref, wpu_ref)
        out_ref[...] = _rms(h, gfin_ref[...])

    _with_expert_rows(pos0_ref, pos1_ref, npos0_ref, npos1_ref, ys_ref, ((b00, b01), (b10, b11)), sem, body)


def _combine_specs():
    last = N_TOK // TOK_TILE - 1
    smem = lambda: pl.BlockSpec((TOK_TILE,), lambda i: (i,), memory_space=pltpu.SMEM)
    smem_next = lambda: pl.BlockSpec((TOK_TILE,), lambda i: (jnp.minimum(i + 1, last),), memory_space=pltpu.SMEM)
    tok = lambda w: pl.BlockSpec((TOK_TILE, w), lambda i: (i, 0))
    full = lambda a, b: pl.BlockSpec((a, b), lambda i: (0, 0))
    scratch = [pltpu.VMEM((TOK_TILE * SLAB_ROWS, LANES), U32) for _ in range(4)] + [pltpu.SemaphoreType.DMA((2,))]
    ple = lambda layer: pl.BlockSpec((None, TOK_TILE, PLE_DIM), lambda i: (layer, i, 0))
    return smem, smem_next, tok, ple, full, scratch


def _combine0(pos0, pos1, h, mf, p, cos_t, sin_t, g_ple, w_pg, w_pu, g_kv, w_k, w_v, g_q, w_q, ys_slab):
    smem, smem_next, tok, ple, full, scratch = _combine_specs()
    return pl.pallas_call(
        _combine0_kernel,
        grid=(N_TOK // TOK_TILE,),
        in_specs=[smem(), smem(), smem_next(), smem_next(), tok(D_MODEL), tok(LANES), ple(0), tok(LANES), tok(LANES),
                  full(1, D_MODEL), full(D_MODEL, D_MODEL), full(PLE_DIM, D_MODEL),
                  full(1, D_MODEL), full(D_MODEL, KV_DIM), full(D_MODEL, KV_DIM),
                  full(1, D_MODEL), full(D_MODEL, D_MODEL),
                  pl.BlockSpec(memory_space=pl.ANY)],
        out_specs=[tok(D_MODEL), tok(KV_DIM), tok(KV_DIM),
                   pl.BlockSpec((Q_PER_KV, TOK_TILE, KV_DIM), lambda i: (0, i, 0))],
        out_shape=[jax.ShapeDtypeStruct((N_TOK, D_MODEL), F32),
                   jax.ShapeDtypeStruct((N_TOK, KV_DIM), BF16),
                   jax.ShapeDtypeStruct((N_TOK, KV_DIM), BF16),
                   jax.ShapeDtypeStruct((Q_PER_KV, N_TOK, KV_DIM), BF16)],
        scratch_shapes=scratch,
        compiler_params=pltpu.CompilerParams(dimension_semantics=("arbitrary",), vmem_limit_bytes=56 << 20),
        name="combine0",
    )(pos0, pos1, pos0, pos1, h, mf, p, cos_t, sin_t, g_ple, w_pg, w_pu, g_kv, w_k, w_v, g_q, w_q, ys_slab)


def _combine1(pos0, pos1, h, mf, p, g_ple, w_pg, w_pu, g_fin, ys_slab):
    smem, smem_next, tok, ple, full, scratch = _combine_specs()
    return pl.pallas_call(
        _combine1_kernel,
        grid=(N_TOK // TOK_TILE,),
        in_specs=[smem(), smem(), smem_next(), smem_next(), tok(D_MODEL), tok(LANES), ple(1),
                  full(1, D_MODEL), full(D_MODEL, D_MODEL), full(PLE_DIM, D_MODEL), full(1, D_MODEL),
                  pl.BlockSpec(memory_space=pl.ANY)],
        out_specs=tok(D_MODEL),
        out_shape=jax.ShapeDtypeStruct((N_TOK, D_MODEL), F32),
        scratch_shapes=scratch,
        compiler_params=pltpu.CompilerParams(dimension_semantics=("arbitrary",), vmem_limit_bytes=48 << 20),
        name="combine1",
    )(pos0, pos1, pos0, pos1, h, mf, p, g_ple, w_pg, w_pu, g_fin, ys_slab)


def _attn1_kernel(sink_ref, q_ref, kc_ref, kp_ref, vc_ref, vp_ref, h_ref, wo_ref, gffn_ref, wr_ref, br_ref, up_ref,
                  ho_ref, xn_ref, mi_ref, mf_ref, cnt_ref, run_ref, k_s, v_s, o_s):
    i = pl.program_id(0)
    steps_per_seq = SEQ // ATT_TILE

    @pl.when(i == 0)
    def _():
        run_ref[...] = jnp.zeros_like(run_ref)

    k_s[:ATT_BLOCK, :] = kp_ref[...]
    k_s[ATT_BLOCK:, :] = kc_ref[...]
    v_s[:ATT_BLOCK, :] = vp_ref[...]
    v_s[ATT_BLOCK:, :] = vc_ref[...]
    seq_start = (i % steps_per_seq) == 0

    n_rows = Q_PER_KV * ATT_BLOCK
    assert WINDOW == ATT_BLOCK
    row = lax.broadcasted_iota(jnp.int32, (n_rows, ATT_BLOCK), 0)
    col = lax.broadcasted_iota(jnp.int32, (n_rows, ATT_BLOCK), 1)
    upper = col > (row % ATT_BLOCK)
    head_of_lane = lax.broadcasted_iota(jnp.int32, (2 * ATT_BLOCK, KV_DIM), 1) // HEAD_DIM
    grp_of_row = lax.broadcasted_iota(jnp.int32, (n_rows, 1), 0) // ATT_BLOCK

    def block(b, carry):
        r0 = pl.multiple_of(b * ATT_BLOCK, ATT_BLOCK)
        has_prev = jnp.logical_not(seq_start & (b == 0))
        kcat = k_s[pl.ds(r0, 2 * ATT_BLOCK), :]
        vcat = v_s[pl.ds(r0, 2 * ATT_BLOCK), :]
        qs = jnp.concatenate([q_ref[g, pl.ds(r0, ATT_BLOCK), :] for g in range(Q_PER_KV)], axis=0)
        o = jnp.zeros((n_rows, KV_DIM), F32)
        for kh in range(N_KV_HEADS):
            k_h = jnp.where(head_of_lane == kh, kcat, jnp.zeros_like(kcat))
            v_h = jnp.where(head_of_lane == kh, vcat, jnp.zeros_like(vcat))
            s = lax.dot_general(qs, k_h, (((1,), (1,)), ((), ())), preferred_element_type=F32)
            s = jnp.where(upper, jnp.where(has_prev, s[:, :ATT_BLOCK], NEG_INF), s[:, ATT_BLOCK:])
            sink = jnp.zeros((n_rows, 1), F32)
            for g in range(Q_PER_KV):
                sink = jnp.where(grp_of_row == g, sink_ref[kh * Q_PER_KV + g], sink)
            m = jnp.maximum(jnp.max(s, axis=1, keepdims=True), sink)
            e = jnp.exp(s - m)
            denom = jnp.sum(e, axis=1, keepdims=True) + jnp.exp(sink - m)
            pr = e / denom
            pr = jnp.concatenate([jnp.where(upper, pr, 0.0), jnp.where(upper, 0.0, pr)], axis=1).astype(BF16)
            o = o + jnp.dot(pr, v_h, preferred_element_type=F32)
        for g in range(Q_PER_KV):
            o_s[g, pl.ds(r0, ATT_BLOCK), :] = o[g * ATT_BLOCK:(g + 1) * ATT_BLOCK, :].astype(BF16)
        return carry

    lax.fori_loop(0, ATT_TILE // ATT_BLOCK, block, 0)
    o_cat = jnp.concatenate([o_s[g] for g in range(Q_PER_KV)], axis=1)
    h = h_ref[...] + jnp.dot(o_cat, wo_ref[...], preferred_element_type=F32)
    ho_ref[...] = h
    xn = _rms(h, gffn_ref[...])
    _store_slab(xn_ref, _pack_rows(xn))
    _route_and_rank(xn, wr_ref, br_ref, up_ref, run_ref, mi_ref, mf_ref, cnt_ref)


def _attn1(sinks, q, k, v, h, w_o, g_ffn, w_r, b_r, upper):
    blocks_per_step = ATT_TILE // ATT_BLOCK
    cur = lambda w: pl.BlockSpec((ATT_TILE, w), lambda i: (i, 0))
    prev = lambda w: pl.BlockSpec((ATT_BLOCK, w), lambda i: (jnp.maximum(i * blocks_per_step - 1, 0), 0))
    full = lambda a, b: pl.BlockSpec((a, b), lambda i: (0, 0))
    return pl.pallas_call(
        _attn1_kernel,
        grid=(N_TOK // ATT_TILE,),
        in_specs=[pl.BlockSpec(memory_space=pltpu.SMEM),
                  pl.BlockSpec((Q_PER_KV, ATT_TILE, KV_DIM), lambda i: (0, i, 0)),
                  cur(KV_DIM), prev(KV_DIM), cur(KV_DIM), prev(KV_DIM), cur(D_MODEL),
                  full(D_MODEL, D_MODEL), full(1, D_MODEL), full(D_MODEL, 2 * LANES), full(1, LANES),
                  full(ATT_TILE, ATT_TILE)],
        out_specs=[cur(D_MODEL),
                   pl.BlockSpec((ATT_TILE * SLAB_ROWS, LANES), lambda i: (i, 0)),
                   pl.BlockSpec((SUBLANES, ATT_TILE), lambda i: (0, i)), cur(LANES),
                   pl.BlockSpec((LANES, LANES), lambda i: (0, 0))],
        out_shape=[jax.ShapeDtypeStruct((N_TOK, D_MODEL), F32),
                   jax.ShapeDtypeStruct((N_TOK * SLAB_ROWS, LANES), U32),
                   jax.ShapeDtypeStruct((SUBLANES, N_TOK), jnp.int32),
                   jax.ShapeDtypeStruct((N_TOK, LANES), F32),
                   jax.ShapeDtypeStruct((LANES, LANES), F32)],
        scratch_shapes=[pltpu.VMEM((LANES, 1), F32),
                        pltpu.VMEM((ATT_TILE + ATT_BLOCK, KV_DIM), BF16),
                        pltpu.VMEM((ATT_TILE + ATT_BLOCK, KV_DIM), BF16),
                        pltpu.VMEM((Q_PER_KV, ATT_TILE, KV_DIM), BF16)],
        compiler_params=pltpu.CompilerParams(dimension_semantics=("arbitrary",), vmem_limit_bytes=48 << 20),
        name="attn1",
    )(sinks, q, k, k, v, v, h, w_o, g_ffn, w_r, b_r, upper)


def _plan_kernel(cnt_ref, meta_ref, pos_ref, vt_ref, ve_ref, gs_ref, ge_ref, nv_ref, nx_ref):
    def per_expert(e, carry):
        off, v = carry
        c = cnt_ref[e]
        gs_ref[e] = off
        ge_ref[e] = off + c
        first = off // EXPERT_TILE
        n_vis = jnp.where(c > 0, (off + c - 1) // EXPERT_TILE - first + 1, 0)

        def visit(j, _):
            vt_ref[v + j] = first + j
            ve_ref[v + j] = e
            return 0

        lax.fori_loop(0, n_vis, visit, 0)
        return off + c, v + n_vis

    _, total = lax.fori_loop(0, N_EXPERTS, per_expert, (jnp.int32(0), jnp.int32(0)))
    nv_ref[0] = total

    def pad(j, _):
        vt_ref[j] = vt_ref[total - 1]
        ve_ref[j] = ve_ref[total - 1]
        return 0

    lax.fori_loop(total, N_VISITS, pad, 0)

    def link(k, nxt):
        e = N_EXPERTS - 1 - k
        nx_ref[e] = jnp.where(nxt >= 0, nxt, e)
        return jnp.where(cnt_ref[e] > 0, e, nxt)

    lax.fori_loop(0, N_EXPERTS, link, jnp.int32(-1))

    meta = meta_ref[...]
    start = jnp.zeros_like(meta)
    for e in range(N_EXPERTS):
        start = jnp.where(meta == e, gs_ref[e], start)
    pos_ref[...] = start + pltpu.roll(meta, SUBLANES - 2, axis=0)


def _routing_tables(meta_i, counts_f):
    counts = counts_f[:N_EXPERTS, 0].astype(jnp.int32)
    smem = lambda: pl.BlockSpec(memory_space=pltpu.SMEM)
    i32 = lambda n: jax.ShapeDtypeStruct((n,), jnp.int32)
    pos, vt, ve, gs, ge, nv, nx = pl.pallas_call(
        _plan_kernel,
        in_specs=[smem(), pl.BlockSpec(memory_space=pltpu.VMEM)],
        out_specs=[pl.BlockSpec(memory_space=pltpu.VMEM), smem(), smem(), smem(), smem(), smem(), smem()],
        out_shape=[jax.ShapeDtypeStruct((SUBLANES, N_TOK), jnp.int32),
                   i32(N_VISITS), i32(N_VISITS), i32(N_EXPERTS), i32(N_EXPERTS), i32(1), i32(N_EXPERTS)],
        name="plan",
    )(counts, meta_i)
    return pos[0], pos[1], vt, ve, gs, ge, nv, nx


def _router_weights(w_grp, b_grp, w_exp, b_exp):
    w = jnp.zeros((D_MODEL, LANES), F32)
    w = w.at[:, :N_EXPERTS].set(w_exp).at[:, GROUP_LANE0:GROUP_LANE0 + N_GROUPS].set(w_grp)
    b = jnp.zeros((1, LANES), F32)
    b = b.at[0, :N_EXPERTS].set(b_exp).at[0, GROUP_LANE0:GROUP_LANE0 + N_GROUPS].set(b_grp)
    w_hi = w.astype(BF16)
    w_lo = (w - w_hi.astype(F32)).astype(BF16)
    return jnp.concatenate([w_hi, w_lo], axis=1), b


def _rope_tables(positions):
    inv = ROPE_THETA ** (-jnp.arange(0, HEAD_DIM, 2, dtype=F32) / HEAD_DIM)
    ang = positions.astype(F32).reshape(N_TOK, 1) * inv
    cos = jnp.cos(ang)
    sin = jnp.sin(ang)
    cos_t = jnp.concatenate([cos, cos, cos, cos], axis=1)
    sin_t = jnp.concatenate([-sin, sin, -sin, sin], axis=1)
    return cos_t, sin_t


def _moe(layer, meta_i, counts_f, xn_slab, w_gate, w_up, w_down):
    pos0, pos1, vt, ve, gs, ge, nv, nx = _routing_tables(meta_i, counts_f)
    xs_slab = _dispatch(pos0, pos1, xn_slab)
    ys_slab = _experts(layer, vt, ve, gs, ge, nv, nx, xs_slab, w_gate, w_up, w_down)
    return pos0, pos1, ys_slab


def kernel(x, p, positions, g_mix, g_ffn, pool_w, pool_scale, g_kv, w_k, w_v, w_q, w_o, sinks, w_group_router, b_group_router, w_expert_router, b_expert_router, w_exp_gate, w_exp_up, w_exp_down, g_ple, w_ple_gate, w_ple_up, g_final):
    row = lambda a: a.reshape(1, -1)
    x2 = x.reshape(N_TOK, D_MODEL)
    p2 = p.reshape(2, N_TOK, PLE_DIM)
    cos_t, sin_t = _rope_tables(positions)
    assert MIX_TILE == ATT_TILE
    upper = jnp.triu(jnp.ones((MIX_TILE, MIX_TILE), BF16), k=1)
    w_q_perm = w_q[0].reshape(D_MODEL, N_KV_HEADS, Q_PER_KV, HEAD_DIM).transpose(0, 2, 1, 3).reshape(D_MODEL, D_MODEL)
    w_o_perm = w_o[0].reshape(N_KV_HEADS, Q_PER_KV, HEAD_DIM, D_MODEL).transpose(1, 0, 2, 3).reshape(D_MODEL, D_MODEL)

    wr0, br0 = _router_weights(w_group_router[0], b_group_router[0], w_expert_router[0], b_expert_router[0])
    h1, xn_slab, meta_i, meta_f, counts = _mixer0(
        x2, row(g_mix[0]), pool_w[0].astype(BF16), row(pool_scale[0]), row(g_ffn[0]), wr0, br0, upper)
    pos0, pos1, ys_slab = _moe(0, meta_i, counts, xn_slab, w_exp_gate, w_exp_up, w_exp_down)
    h3, k, v, q = _combine0(
        pos0, pos1, h1, meta_f, p2, cos_t, sin_t, row(g_ple[0]), w_ple_gate[0].astype(BF16),
        w_ple_up[0].astype(BF16), row(g_kv), w_k.astype(BF16), w_v.astype(BF16), row(g_mix[1]),
        w_q_perm.astype(BF16), ys_slab)

    wr1, br1 = _router_weights(w_group_router[1], b_group_router[1], w_expert_router[1], b_expert_router[1])
    h4, xn_slab, meta_i, meta_f, counts = _attn1(
        sinks[0], q, k, v, h3, w_o_perm.astype(BF16), row(g_ffn[1]), wr1, br1, upper)
    pos0, pos1, ys_slab = _moe(1, meta_i, counts, xn_slab, w_exp_gate, w_exp_up, w_exp_down)
    out = _combine1(pos0, pos1, h4, meta_f, p2, row(g_ple[1]), w_ple_gate[1].astype(BF16),
                    w_ple_up[1].astype(BF16), row(g_final), ys_slab)
    return out.reshape(BATCH, SEQ, D_MODEL)
```

```python
import functools

import jax
import jax.numpy as jnp
from jax import lax
from jax.experimental import pallas as pl
from jax.experimental.pallas import tpu as pltpu

D_MODEL = 1024
BATCH = 2
SEQ = 8192
N_TOK = BATCH * SEQ
POOL_WINDOWS = (2, 4, 8, 16)
POOL_GROUP = D_MODEL // len(POOL_WINDOWS)
POOL_HALO = 16
HEAD_DIM = 64
N_Q_HEADS = 16
N_KV_HEADS = 4
Q_PER_KV = 4
KV_DIM = N_KV_HEADS * HEAD_DIM
WINDOW = 128
ATT_BLOCK = 128
ROPE_THETA = 10000.0
N_GROUPS = 4
EXPERTS_PER_GROUP = 8
N_EXPERTS = 32
D_EXPERT = 512
PLE_DIM = 256
EPS = 1e-6

LANES = 128
SUBLANES = 8
HALF = D_MODEL // 2
SLAB_ROWS = HALF // LANES
N_ASSIGN = 2 * N_TOK

TOK_TILE = 256
MIX_TILE = 512
ATT_TILE = 512
DISPATCH_TILE = 512
EXPERT_TILE = 256
N_ROW_TILES = N_ASSIGN // EXPERT_TILE
N_VISITS = N_ROW_TILES + N_EXPERTS - 1
GROUP_LANE0 = N_EXPERTS

F32 = jnp.float32
BF16 = jnp.bfloat16
U32 = jnp.uint32
NEG_INF = float("-inf")


def _rms(x, g):
    return x * lax.rsqrt(jnp.mean(x * x, axis=-1, keepdims=True) + EPS) * g


def _pack_rows(val):
    lo = pltpu.bitcast(val[:, :HALF].astype(BF16).astype(F32), U32)
    hi = pltpu.bitcast(val[:, HALF:].astype(BF16).astype(F32), U32)
    return hi | (lo >> 16)


def _unpack_rows(word):
    lo = pltpu.bitcast(word << 16, F32)
    hi = pltpu.bitcast(word & jnp.uint32(0xFFFF0000), F32)
    return jnp.concatenate([lo, hi], axis=1)


def _store_slab(ref, word):
    n = word.shape[0]
    for c in range(SLAB_ROWS):
        ref[pl.ds(c, n, stride=SLAB_ROWS), :] = word[:, c * LANES:(c + 1) * LANES]


def _load_slab(ref, n):
    return jnp.concatenate([ref[pl.ds(c, n, stride=SLAB_ROWS), :] for c in range(SLAB_ROWS)], axis=1)


def _route_and_rank(xn, wr_ref, br_ref, up_ref, run_ref, mi_ref, mf_ref, cnt_ref):
    n = xn.shape[0]
    x_hi = xn.astype(BF16)
    x_lo = (xn - x_hi.astype(F32)).astype(BF16)
    hh_hl = jnp.dot(x_hi, wr_ref[...], preferred_element_type=F32)
    lh = jnp.dot(x_lo, wr_ref[:, :LANES], preferred_element_type=F32)
    logits = (hh_hl[:, :LANES] + (hh_hl[:, LANES:] + lh) + br_ref[...]).T
    big = float(LANES)

    grow = lax.broadcasted_iota(jnp.int32, (SUBLANES, n), 0).astype(F32)
    gl = jnp.where(grow < N_GROUPS, logits[GROUP_LANE0:GROUP_LANE0 + SUBLANES, :], NEG_INF)
    gmax = jnp.max(gl, axis=0, keepdims=True)
    gidx = jnp.min(jnp.where(gl == gmax, grow, big), axis=0, keepdims=True)
    pg = 1.0 / jnp.sum(jnp.exp(gl - gmax), axis=0, keepdims=True)

    erow = lax.broadcasted_iota(jnp.int32, (N_EXPERTS, n), 0).astype(F32)
    lo = gidx * EXPERTS_PER_GROUP
    el = jnp.where((erow >= lo) & (erow < lo + EXPERTS_PER_GROUP), logits[:N_EXPERTS, :], NEG_INF)
    m1 = jnp.max(el, axis=0, keepdims=True)
    i1 = jnp.min(jnp.where(el == m1, erow, big), axis=0, keepdims=True)
    el2 = jnp.where(erow == i1, NEG_INF, el)
    m2 = jnp.max(el2, axis=0, keepdims=True)
    i2 = jnp.min(jnp.where(el2 == m2, erow, big), axis=0, keepdims=True)
    z = jnp.exp(m2 - m1)
    gate0 = pg / (1.0 + z)
    gate1 = pg * z / (1.0 + z)

    row = lax.broadcasted_iota(jnp.int32, (LANES, n), 0).astype(F32)
    oh0 = jnp.where(row == i1, 1.0, 0.0)
    oh1 = jnp.where(row == i2, 1.0, 0.0)
    oh = oh0 + oh1
    prefix = jnp.dot(oh.astype(BF16), up_ref[...], preferred_element_type=F32)
    tot = prefix + run_ref[...]
    rank0 = jnp.sum(oh0 * tot, axis=0, keepdims=True)
    rank1 = jnp.sum(oh1 * tot, axis=0, keepdims=True)
    new_run = run_ref[...] + jnp.sum(oh, axis=1, keepdims=True)
    run_ref[...] = new_run
    cnt_ref[...] = jnp.broadcast_to(new_run, cnt_ref.shape)

    r8 = lax.broadcasted_iota(jnp.int32, (SUBLANES, n), 0)
    meta = jnp.where(r8 == 0, i1, jnp.where(r8 == 1, i2, jnp.where(r8 == 2, rank0, jnp.where(r8 == 3, rank1, 0.0))))
    mi_ref[...] = meta.astype(jnp.int32)
    mf_ref[...] = jnp.where(row == 0.0, gate0, jnp.where(row == 1.0, gate1, 0.0)).T


def _mixer0_kernel(x_ref, halo_ref, gmix_ref, poolw_ref, pscale_ref, gffn_ref, wr_ref, br_ref, up_ref,
                   h_ref, xn_ref, mi_ref, mf_ref, cnt_ref, run_ref):
    i = pl.program_id(0)
    tiles_per_seq = SEQ // MIX_TILE

    @pl.when(i == 0)
    def _():
        run_ref[...] = jnp.zeros_like(run_ref)

    seq_tile = i % tiles_per_seq
    x = x_ref[...]
    g = gmix_ref[...]
    hn = _rms(x, g)
    halo = jnp.where(seq_tile == 0, 0.0, _rms(halo_ref[...], g))
    full = jnp.concatenate([halo, hn], axis=0)
    t = (seq_tile * MIX_TILE + lax.broadcasted_iota(jnp.int32, (MIX_TILE, 1), 0) + 1).astype(F32)
    outs = []
    for gi, w in enumerate(POOL_WINDOWS):
        acc = full[:, gi * POOL_GROUP:(gi + 1) * POOL_GROUP]
        shift = 1
        while shift < w:
            acc = acc + pltpu.roll(acc, shift, axis=0)
            shift *= 2
        win = acc[POOL_HALO:, :]
        mean = win / jnp.minimum(t, float(w))
        dlt = (mean - hn[:, gi * POOL_GROUP:(gi + 1) * POOL_GROUP]).astype(BF16)
        outs.append(jnp.dot(dlt, poolw_ref[gi], preferred_element_type=F32))
    y = jnp.concatenate(outs, axis=1)
    h = x + y * pscale_ref[...]
    h_ref[...] = h
    xn = _rms(h, gffn_ref[...])
    _store_slab(xn_ref, _pack_rows(xn))
    _route_and_rank(xn, wr_ref, br_ref, up_ref, run_ref, mi_ref, mf_ref, cnt_ref)


def _mixer0(x, g_mix, pool_w, pool_scale, g_ffn, w_r, b_r, upper):
    n_tiles = N_TOK // MIX_TILE
    halo_per_tile = MIX_TILE // POOL_HALO
    vec = lambda: pl.BlockSpec((1, D_MODEL), lambda i: (0, 0))
    return pl.pallas_call(
        _mixer0_kernel,
        grid=(n_tiles,),
        in_specs=[
            pl.BlockSpec((MIX_TILE, D_MODEL), lambda i: (i, 0)),
            pl.BlockSpec((POOL_HALO, D_MODEL), lambda i: (jnp.maximum(i * halo_per_tile - 1, 0), 0)),
            vec(),
            pl.BlockSpec((len(POOL_WINDOWS), POOL_GROUP, POOL_GROUP), lambda i: (0, 0, 0)),
            vec(), vec(),
            pl.BlockSpec((D_MODEL, 2 * LANES), lambda i: (0, 0)),
            pl.BlockSpec((1, LANES), lambda i: (0, 0)),
            pl.BlockSpec((MIX_TILE, MIX_TILE), lambda i: (0, 0)),
        ],
        out_specs=[
            pl.BlockSpec((MIX_TILE, D_MODEL), lambda i: (i, 0)),
            pl.BlockSpec((MIX_TILE * SLAB_ROWS, LANES), lambda i: (i, 0)),
            pl.BlockSpec((SUBLANES, MIX_TILE), lambda i: (0, i)),
            pl.BlockSpec((MIX_TILE, LANES), lambda i: (i, 0)),
            pl.BlockSpec((LANES, LANES), lambda i: (0, 0)),
        ],
        out_shape=[
            jax.ShapeDtypeStruct((N_TOK, D_MODEL), F32),
            jax.ShapeDtypeStruct((N_TOK * SLAB_ROWS, LANES), U32),
            jax.ShapeDtypeStruct((SUBLANES, N_TOK), jnp.int32),
            jax.ShapeDtypeStruct((N_TOK, LANES), F32),
            jax.ShapeDtypeStruct((LANES, LANES), F32),
        ],
        scratch_shapes=[pltpu.VMEM((LANES, 1), F32)],
        compiler_params=pltpu.CompilerParams(dimension_semantics=("arbitrary",), vmem_limit_bytes=48 << 20),
        name="mixer0",
    )(x, x, g_mix, pool_w, pool_scale, g_ffn, w_r, b_r, upper)


def _row_copy(src, src_row, dst, dst_row, sem):
    return pltpu.make_async_copy(
        src.at[pl.ds(pl.multiple_of(src_row * SLAB_ROWS, SLAB_ROWS), SLAB_ROWS), :],
        dst.at[pl.ds(pl.multiple_of(dst_row * SLAB_ROWS, SLAB_ROWS), SLAB_ROWS), :],
        sem)


def _dispatch_kernel(pos0_ref, pos1_ref, src_ref, dst_ref, sem):
    unroll = 8

    def issue(jo, carry):
        for u in range(unroll):
            j = jo * unroll + u
            _row_copy(src_ref, j, dst_ref, pos0_ref[j], sem).start(priority=0)
            _row_copy(src_ref, j, dst_ref, pos1_ref[j], sem).start(priority=1)
        return carry

    lax.fori_loop(0, DISPATCH_TILE // unroll, issue, 0)

    def drain(jo, carry):
        for u in range(2 * unroll):
            _row_copy(src_ref, 0, dst_ref, 0, sem).wait()
        return carry

    lax.fori_loop(0, DISPATCH_TILE // unroll, drain, 0)


def _dispatch(pos0, pos1, xn_slab):
    smem = lambda: pl.BlockSpec((DISPATCH_TILE,), lambda i: (i,), memory_space=pltpu.SMEM)
    return pl.pallas_call(
        _dispatch_kernel,
        grid=(N_TOK // DISPATCH_TILE,),
        in_specs=[smem(), smem(), pl.BlockSpec((DISPATCH_TILE * SLAB_ROWS, LANES), lambda i: (i, 0))],
        out_specs=pl.BlockSpec(memory_space=pl.ANY),
        out_shape=jax.ShapeDtypeStruct((N_ASSIGN * SLAB_ROWS, LANES), U32),
        scratch_shapes=[pltpu.SemaphoreType.DMA(())],
        compiler_params=pltpu.CompilerParams(dimension_semantics=("arbitrary",), has_side_effects=True),
        name="dispatch",
    )(pos0, pos1, xn_slab)


X_SLOTS = 3
Y_SLOTS = 2


def _experts_kernel(layer, vt_ref, ve_ref, gs_ref, ge_ref, nv_ref, nx_ref, xs_hbm, wg_hbm, wu_hbm, wd_hbm, ys_hbm,
                    xb, yb, wg_st, wu_st, wd_st, wgu_s, wd_s, xsem, ysem, wsem):
    rows = EXPERT_TILE * SLAB_ROWS
    n_visits = nv_ref[0]

    def x_copy(t):
        return pltpu.make_async_copy(xs_hbm.at[pl.ds(pl.multiple_of(t * rows, rows), rows), :],
                                     xb.at[t % X_SLOTS], xsem.at[t % X_SLOTS])

    def y_copy(t):
        return pltpu.make_async_copy(yb.at[t % Y_SLOTS],
                                     ys_hbm.at[pl.ds(pl.multiple_of(t * rows, rows), rows), :], ysem.at[t % Y_SLOTS])

    def w_copies(ex):
        return (pltpu.make_async_copy(wg_hbm.at[layer, ex], wg_st, wsem.at[0]),
                pltpu.make_async_copy(wu_hbm.at[layer, ex], wu_st, wsem.at[1]),
                pltpu.make_async_copy(wd_hbm.at[layer, ex], wd_st, wsem.at[2]))

    x_copy(0).start()
    x_copy(1).start()
    for c in w_copies(ve_ref[0]):
        c.start()

    def visit(i, carry):
        prev = jnp.maximum(i - 1, 0)
        e = ve_ref[i]
        tile = vt_ref[i]
        new_expert = (i == 0) | (e != ve_ref[prev])
        new_tile = (i == 0) | (tile != vt_ref[prev])

        @pl.when(new_tile)
        def _():
            x_copy(tile).wait()

            @pl.when(tile + 2 < N_ROW_TILES)
            def _():
                x_copy(tile + 2).start()

            @pl.when(tile >= 2)
            def _():
                y_copy(tile - 2).wait()

            @pl.when(tile >= 1)
            def _():
                y_copy(tile - 1).start()

        @pl.when(new_expert)
        def _():
            for c in w_copies(e):
                c.wait()
            wgu_s[:, :D_EXPERT] = wg_st[...].astype(BF16)
            wgu_s[:, D_EXPERT:] = wu_st[...].astype(BF16)
            wd_s[...] = wd_st[...].astype(BF16)
            nxt = nx_ref[e]

            @pl.when(nxt != e)
            def _():
                for c in w_copies(nxt):
                    c.start()

        xs = tile % X_SLOTS
        ys = tile % Y_SLOTS
        x = _unpack_rows(jnp.concatenate(
            [xb[xs, pl.ds(c, EXPERT_TILE, stride=SLAB_ROWS), :] for c in range(SLAB_ROWS)], axis=1))
        gu = jnp.dot(x.astype(BF16), wgu_s[...], preferred_element_type=F32)
        gate = gu[:, :D_EXPERT]
        act = (gate / (1.0 + jnp.exp(-gate)) * gu[:, D_EXPERT:]).astype(BF16)
        y = _pack_rows(jnp.dot(act, wd_s[...], preferred_element_type=F32))
        r = tile * EXPERT_TILE + lax.broadcasted_iota(jnp.int32, (EXPERT_TILE, 1), 0)
        mine = (r >= gs_ref[e]) & (r < ge_ref[e])

        @pl.when(new_tile)
        def _():
            for c in range(SLAB_ROWS):
                yb[ys, pl.ds(c, EXPERT_TILE, stride=SLAB_ROWS), :] = jnp.where(
                    mine, y[:, c * LANES:(c + 1) * LANES], jnp.uint32(0))

        @pl.when(jnp.logical_not(new_tile))
        def _():
            for c in range(SLAB_ROWS):
                old = yb[ys, pl.ds(c, EXPERT_TILE, stride=SLAB_ROWS), :]
                yb[ys, pl.ds(c, EXPERT_TILE, stride=SLAB_ROWS), :] = jnp.where(mine, y[:, c * LANES:(c + 1) * LANES], old)

        return carry

    lax.fori_loop(0, n_visits, visit, 0)
    last_tile = vt_ref[n_visits - 1]
    y_copy(last_tile).start()
    y_copy(last_tile).wait()

    @pl.when(last_tile >= 1)
    def _():
        y_copy(last_tile - 1).wait()


def _experts(layer, visit_tile, visit_expert, g_start, g_end, n_visits, next_expert, xs_slab, w_gate, w_up, w_down):
    rows = EXPERT_TILE * SLAB_ROWS
    any_spec = lambda: pl.BlockSpec(memory_space=pl.ANY)
    grid_spec = pltpu.PrefetchScalarGridSpec(
        num_scalar_prefetch=6,
        grid=(1,),
        in_specs=[any_spec(), any_spec(), any_spec(), any_spec()],
        out_specs=any_spec(),
        scratch_shapes=[pltpu.VMEM((X_SLOTS, rows, LANES), U32), pltpu.VMEM((Y_SLOTS, rows, LANES), U32),
                        pltpu.VMEM((D_MODEL, D_EXPERT), F32), pltpu.VMEM((D_MODEL, D_EXPERT), F32),
                        pltpu.VMEM((D_EXPERT, D_MODEL), F32),
                        pltpu.VMEM((D_MODEL, 2 * D_EXPERT), BF16), pltpu.VMEM((D_EXPERT, D_MODEL), BF16),
                        pltpu.SemaphoreType.DMA((X_SLOTS,)), pltpu.SemaphoreType.DMA((Y_SLOTS,)),
                        pltpu.SemaphoreType.DMA((3,))],
    )
    return pl.pallas_call(
        functools.partial(_experts_kernel, layer),
        grid_spec=grid_spec,
        out_shape=jax.ShapeDtypeStruct((N_ASSIGN * SLAB_ROWS, LANES), U32),
        compiler_params=pltpu.CompilerParams(dimension_semantics=("arbitrary",), vmem_limit_bytes=48 << 20,
                                             has_side_effects=True),
        name="experts",
    )(visit_tile, visit_expert, g_start, g_end, n_visits, next_expert, xs_slab, w_gate, w_up, w_down)


def _with_expert_rows(pos0_ref, pos1_ref, npos0_ref, npos1_ref, ys_ref, bufs, sem, body):
    i = pl.program_id(0)

    def copies(p0_ref, p1_ref, s):
        for j in range(TOK_TILE):
            yield _row_copy(ys_ref, p0_ref[j], bufs[s][0], j, sem.at[s])
            yield _row_copy(ys_ref, p1_ref[j], bufs[s][1], j, sem.at[s])

    @pl.when(i == 0)
    def _():
        for n, c in enumerate(copies(pos0_ref, pos1_ref, 0)):
            c.start(priority=n % 2)

    for s in range(2):
        @pl.when(i % 2 == s)
        def _():
            for n, c in enumerate(copies(npos0_ref, npos1_ref, 1 - s)):
                c.start(priority=n % 2)
            for c in copies(pos0_ref, pos1_ref, s):
                c.wait()
            body(_unpack_rows(_load_slab(bufs[s][0], TOK_TILE)), _unpack_rows(_load_slab(bufs[s][1], TOK_TILE)))

            @pl.when(i == pl.num_programs(0) - 1)
            def _():
                for c in copies(npos0_ref, npos1_ref, 1 - s):
                    c.wait()


def _moe_residual_and_ple(h_ref, mf_ref, y0, y1, p_ref, gple_ref, wpg_ref, wpu_ref):
    gates = mf_ref[...]
    h = h_ref[...] + (y0 * gates[:, 0:1] + y1 * gates[:, 1:2])
    hn = _rms(h, gple_ref[...]).astype(BF16)
    gate = jnp.dot(hn, wpg_ref[...], preferred_element_type=F32)
    gate = 1.0 / (1.0 + jnp.exp(-gate))
    up = jnp.dot(p_ref[...].astype(BF16), wpu_ref[...], preferred_element_type=F32)
    return h + gate * up


def _rope(x, cos, sin_signed):
    n, width = x.shape
    reps = width // LANES
    cos_t = jnp.concatenate([cos] * reps, axis=1)
    sin_t = jnp.concatenate([sin_signed] * reps, axis=1)
    lane = lax.broadcasted_iota(jnp.int32, (n, width), 1)
    first_half = (lane % HEAD_DIM) < (HEAD_DIM // 2)
    partner = jnp.where(first_half, pltpu.roll(x, width - HEAD_DIM // 2, axis=1), pltpu.roll(x, HEAD_DIM // 2, axis=1))
    return x * cos_t + partner * sin_t


def _combine0_kernel(pos0_ref, pos1_ref, npos0_ref, npos1_ref, h_ref, mf_ref, p_ref, cos_ref, sin_ref, gple_ref, wpg_ref, wpu_ref,
                     gkv_ref, wk_ref, wv_ref, gq_ref, wq_ref, ys_ref,
                     ho_ref, k_ref, v_ref, q_ref, b00, b01, b10, b11, sem):
    def body(y0, y1):
        h = _moe_residual_and_ple(h_ref, mf_ref, y0, y1, p_ref, gple_ref, wpg_ref, wpu_ref)
        ho_ref[...] = h
        cos = cos_ref[...]
        sin = sin_ref[...]
        kvn = _rms(h, gkv_ref[...]).astype(BF16)
        k = jnp.dot(kvn, wk_ref[...], preferred_element_type=F32)
        k_ref[...] = _rope(k, cos, sin).astype(BF16)
        v_ref[...] = jnp.dot(kvn, wv_ref[...], preferred_element_type=F32).astype(BF16)
        qn = _rms(h, gq_ref[...]).astype(BF16)
        q = jnp.dot(qn, wq_ref[...], preferred_element_type=F32)
        q = (_rope(q, cos, sin) * (HEAD_DIM ** -0.5)).astype(BF16)
        for g in range(Q_PER_KV):
            q_ref[g] = q[:, g * KV_DIM:(g + 1) * KV_DIM]

    _with_expert_rows(pos0_ref, pos1_ref, npos0_ref, npos1_ref, ys_ref, ((b00, b01), (b10, b11)), sem, body)


def _combine1_kernel(pos0_ref, pos1_ref, npos0_ref, npos1_ref, h_ref, mf_ref, p_ref, gple_ref, wpg_ref, wpu_ref, gfin_ref, ys_ref,
                     out_ref, b00, b01, b10, b11, sem):
    def body(y0, y1):
        h = _moe_residual_and_ple(h_ref, mf_ref, y0, y1, p_ref, gple_ref, wpg_ref, wpu_ref)
        out_ref[...] = _rms(h, gfin_ref[...])

    _with_expert_rows(pos0_ref, pos1_ref, npos0_ref, npos1_ref, ys_ref, ((b00, b01), (b10, b11)), sem, body)


def _combine_specs():
    last = N_TOK // TOK_TILE - 1
    smem = lambda: pl.BlockSpec((TOK_TILE,), lambda i: (i,), memory_space=pltpu.SMEM)
    smem_next = lambda: pl.BlockSpec((TOK_TILE,), lambda i: (jnp.minimum(i + 1, last),), memory_space=pltpu.SMEM)
    tok = lambda w: pl.BlockSpec((TOK_TILE, w), lambda i: (i, 0))
    full = lambda a, b: pl.BlockSpec((a, b), lambda i: (0, 0))
    scratch = [pltpu.VMEM((TOK_TILE * SLAB_ROWS, LANES), U32) for _ in range(4)] + [pltpu.SemaphoreType.DMA((2,))]
    ple = lambda layer: pl.BlockSpec((None, TOK_TILE, PLE_DIM), lambda i: (layer, i, 0))
    return smem, smem_next, tok, ple, full, scratch


def _combine0(pos0, pos1, h, mf, p, cos_t, sin_t, g_ple, w_pg, w_pu, g_kv, w_k, w_v, g_q, w_q, ys_slab):
    smem, smem_next, tok, ple, full, scratch = _combine_specs()
    return pl.pallas_call(
        _combine0_kernel,
        grid=(N_TOK // TOK_TILE,),
        in_specs=[smem(), smem(), smem_next(), smem_next(), tok(D_MODEL), tok(LANES), ple(0), tok(LANES), tok(LANES),
                  full(1, D_MODEL), full(D_MODEL, D_MODEL), full(PLE_DIM, D_MODEL),
                  full(1, D_MODEL), full(D_MODEL, KV_DIM), full(D_MODEL, KV_DIM),
                  full(1, D_MODEL), full(D_MODEL, D_MODEL),
                  pl.BlockSpec(memory_space=pl.ANY)],
        out_specs=[tok(D_MODEL), tok(KV_DIM), tok(KV_DIM),
                   pl.BlockSpec((Q_PER_KV, TOK_TILE, KV_DIM), lambda i: (0, i, 0))],
        out_shape=[jax.ShapeDtypeStruct((N_TOK, D_MODEL), F32),
                   jax.ShapeDtypeStruct((N_TOK, KV_DIM), BF16),
                   jax.ShapeDtypeStruct((N_TOK, KV_DIM), BF16),
                   jax.ShapeDtypeStruct((Q_PER_KV, N_TOK, KV_DIM), BF16)],
        scratch_shapes=scratch,
        compiler_params=pltpu.CompilerParams(dimension_semantics=("arbitrary",), vmem_limit_bytes=56 << 20),
        name="combine0",
    )(pos0, pos1, pos0, pos1, h, mf, p, cos_t, sin_t, g_ple, w_pg, w_pu, g_kv, w_k, w_v, g_q, w_q, ys_slab)


def _combine1(pos0, pos1, h, mf, p, g_ple, w_pg, w_pu, g_fin, ys_slab):
    smem, smem_next, tok, ple, full, scratch = _combine_specs()
    return pl.pallas_call(
        _combine1_kernel,
        grid=(N_TOK // TOK_TILE,),
        in_specs=[smem(), smem(), smem_next(), smem_next(), tok(D_MODEL), tok(LANES), ple(1),
                  full(1, D_MODEL), full(D_MODEL, D_MODEL), full(PLE_DIM, D_MODEL), full(1, D_MODEL),
                  pl.BlockSpec(memory_space=pl.ANY)],
        out_specs=tok(D_MODEL),
        out_shape=jax.ShapeDtypeStruct((N_TOK, D_MODEL), F32),
        scratch_shapes=scratch,
        compiler_params=pltpu.CompilerParams(dimension_semantics=("arbitrary",), vmem_limit_bytes=48 << 20),
        name="combine1",
    )(pos0, pos1, pos0, pos1, h, mf, p, g_ple, w_pg, w_pu, g_fin, ys_slab)


def _attn1_kernel(sink_ref, q_ref, kc_ref, kp_ref, vc_ref, vp_ref, h_ref, wo_ref, gffn_ref, wr_ref, br_ref, up_ref,
                  ho_ref, xn_ref, mi_ref, mf_ref, cnt_ref, run_ref, k_s, v_s, o_s):
    i = pl.program_id(0)
    steps_per_seq = SEQ // ATT_TILE

    @pl.when(i == 0)
    def _():
        run_ref[...] = jnp.zeros_like(run_ref)

    k_s[:ATT_BLOCK, :] = kp_ref[...]
    k_s[ATT_BLOCK:, :] = kc_ref[...]
    v_s[:ATT_BLOCK, :] = vp_ref[...]
    v_s[ATT_BLOCK:, :] = vc_ref[...]
    seq_start = (i % steps_per_seq) == 0

    n_rows = Q_PER_KV * ATT_BLOCK
    assert WINDOW == ATT_BLOCK
    row = lax.broadcasted_iota(jnp.int32, (n_rows, ATT_BLOCK), 0)
    col = lax.broadcasted_iota(jnp.int32, (n_rows, ATT_BLOCK), 1)
    upper = col > (row % ATT_BLOCK)
    head_of_lane = lax.broadcasted_iota(jnp.int32, (2 * ATT_BLOCK, KV_DIM), 1) // HEAD_DIM
    grp_of_row = lax.broadcasted_iota(jnp.int32, (n_rows, 1), 0) // ATT_BLOCK

    def block(b, carry):
        r0 = b * ATT_BLOCK
        has_prev = jnp.logical_not(seq_start) if b == 0 else True
        kcat = k_s[pl.ds(r0, 2 * ATT_BLOCK), :]
        vcat = v_s[pl.ds(r0, 2 * ATT_BLOCK), :]
        qs = jnp.concatenate([q_ref[g, pl.ds(r0, ATT_BLOCK), :] for g in range(Q_PER_KV)], axis=0)
        o = jnp.zeros((n_rows, KV_DIM), F32)
        for kh in range(N_KV_HEADS):
            k_h = jnp.where(head_of_lane == kh, kcat, jnp.zeros_like(kcat))
            v_h = jnp.where(head_of_lane == kh, vcat, jnp.zeros_like(vcat))
            s = lax.dot_general(qs, k_h, (((1,), (1,)), ((), ())), preferred_element_type=F32)
            s = jnp.where(upper, jnp.where(has_prev, s[:, :ATT_BLOCK], NEG_INF), s[:, ATT_BLOCK:])
            sink = jnp.zeros((n_rows, 1), F32)
            for g in range(Q_PER_KV):
                sink = jnp.where(grp_of_row == g, sink_ref[kh * Q_PER_KV + g], sink)
            m = jnp.maximum(jnp.max(s, axis=1, keepdims=True), sink)
            e = jnp.exp(s - m)
            denom = jnp.sum(e, axis=1, keepdims=True) + jnp.exp(sink - m)
            pr = e / denom
            pr = jnp.concatenate([jnp.where(upper, pr, 0.0), jnp.where(upper, 0.0, pr)], axis=1).astype(BF16)
            o = o + jnp.dot(pr, v_h, preferred_element_type=F32)
        for g in range(Q_PER_KV):
            o_s[g, pl.ds(r0, ATT_BLOCK), :] = o[g * ATT_BLOCK:(g + 1) * ATT_BLOCK, :].astype(BF16)
        return carry

    for b in range(ATT_TILE // ATT_BLOCK):
        block(b, 0)
    o_cat = jnp.concatenate([o_s[g] for g in range(Q_PER_KV)], axis=1)
    h = h_ref[...] + jnp.dot(o_cat, wo_ref[...], preferred_element_type=F32)
    ho_ref[...] = h
    xn = _rms(h, gffn_ref[...])
    _store_slab(xn_ref, _pack_rows(xn))
    _route_and_rank(xn, wr_ref, br_ref, up_ref, run_ref, mi_ref, mf_ref, cnt_ref)


def _attn1(sinks, q, k, v, h, w_o, g_ffn, w_r, b_r, upper):
    blocks_per_step = ATT_TILE // ATT_BLOCK
    cur = lambda w: pl.BlockSpec((ATT_TILE, w), lambda i: (i, 0))
    prev = lambda w: pl.BlockSpec((ATT_BLOCK, w), lambda i: (jnp.maximum(i * blocks_per_step - 1, 0), 0))
    full = lambda a, b: pl.BlockSpec((a, b), lambda i: (0, 0))
    return pl.pallas_call(
        _attn1_kernel,
        grid=(N_TOK // ATT_TILE,),
        in_specs=[pl.BlockSpec(memory_space=pltpu.SMEM),
                  pl.BlockSpec((Q_PER_KV, ATT_TILE, KV_DIM), lambda i: (0, i, 0)),
                  cur(KV_DIM), prev(KV_DIM), cur(KV_DIM), prev(KV_DIM), cur(D_MODEL),
                  full(D_MODEL, D_MODEL), full(1, D_MODEL), full(D_MODEL, 2 * LANES), full(1, LANES),
                  full(ATT_TILE, ATT_TILE)],
        out_specs=[cur(D_MODEL),
                   pl.BlockSpec((ATT_TILE * SLAB_ROWS, LANES), lambda i: (i, 0)),
                   pl.BlockSpec((SUBLANES, ATT_TILE), lambda i: (0, i)), cur(LANES),
                   pl.BlockSpec((LANES, LANES), lambda i: (0, 0))],
        out_shape=[jax.ShapeDtypeStruct((N_TOK, D_MODEL), F32),
                   jax.ShapeDtypeStruct((N_TOK * SLAB_ROWS, LANES), U32),
                   jax.ShapeDtypeStruct((SUBLANES, N_TOK), jnp.int32),
                   jax.ShapeDtypeStruct((N_TOK, LANES), F32),
                   jax.ShapeDtypeStruct((LANES, LANES), F32)],
        scratch_shapes=[pltpu.VMEM((LANES, 1), F32),
                        pltpu.VMEM((ATT_TILE + ATT_BLOCK, KV_DIM), BF16),
                        pltpu.VMEM((ATT_TILE + ATT_BLOCK, KV_DIM), BF16),
                        pltpu.VMEM((Q_PER_KV, ATT_TILE, KV_DIM), BF16)],
        compiler_params=pltpu.CompilerParams(dimension_semantics=("arbitrary",), vmem_limit_bytes=48 << 20),
        name="attn1",
    )(sinks, q, k, k, v, v, h, w_o, g_ffn, w_r, b_r, upper)


def _plan_kernel(cnt_ref, meta_ref, pos_ref, vt_ref, ve_ref, gs_ref, ge_ref, nv_ref, nx_ref):
    def per_expert(e, carry):
        off, v = carry
        c = cnt_ref[e]
        gs_ref[e] = off
        ge_ref[e] = off + c
        first = off // EXPERT_TILE
        n_vis = jnp.where(c > 0, (off + c - 1) // EXPERT_TILE - first + 1, 0)

        def visit(j, _):
            vt_ref[v + j] = first + j
            ve_ref[v + j] = e
            return 0

        lax.fori_loop(0, n_vis, visit, 0)
        return off + c, v + n_vis

    _, total = lax.fori_loop(0, N_EXPERTS, per_expert, (jnp.int32(0), jnp.int32(0)))
    nv_ref[0] = total

    def pad(j, _):
        vt_ref[j] = vt_ref[total - 1]
        ve_ref[j] = ve_ref[total - 1]
        return 0

    lax.fori_loop(total, N_VISITS, pad, 0)

    def link(k, nxt):
        e = N_EXPERTS - 1 - k
        nx_ref[e] = jnp.where(nxt >= 0, nxt, e)
        return jnp.where(cnt_ref[e] > 0, e, nxt)

    lax.fori_loop(0, N_EXPERTS, link, jnp.int32(-1))

    meta = meta_ref[...]
    start = jnp.zeros_like(meta)
    for e in range(N_EXPERTS):
        start = jnp.where(meta == e, gs_ref[e], start)
    pos_ref[...] = start + pltpu.roll(meta, SUBLANES - 2, axis=0)


def _routing_tables(meta_i, counts_f):
    counts = counts_f[:N_EXPERTS, 0].astype(jnp.int32)
    smem = lambda: pl.BlockSpec(memory_space=pltpu.SMEM)
    i32 = lambda n: jax.ShapeDtypeStruct((n,), jnp.int32)
    pos, vt, ve, gs, ge, nv, nx = pl.pallas_call(
        _plan_kernel,
        in_specs=[smem(), pl.BlockSpec(memory_space=pltpu.VMEM)],
        out_specs=[pl.BlockSpec(memory_space=pltpu.VMEM), smem(), smem(), smem(), smem(), smem(), smem()],
        out_shape=[jax.ShapeDtypeStruct((SUBLANES, N_TOK), jnp.int32),
                   i32(N_VISITS), i32(N_VISITS), i32(N_EXPERTS), i32(N_EXPERTS), i32(1), i32(N_EXPERTS)],
        name="plan",
    )(counts, meta_i)
    return pos[0], pos[1], vt, ve, gs, ge, nv, nx


def _router_weights(w_grp, b_grp, w_exp, b_exp):
    w = jnp.zeros((D_MODEL, LANES), F32)
    w = w.at[:, :N_EXPERTS].set(w_exp).at[:, GROUP_LANE0:GROUP_LANE0 + N_GROUPS].set(w_grp)
    b = jnp.zeros((1, LANES), F32)
    b = b.at[0, :N_EXPERTS].set(b_exp).at[0, GROUP_LANE0:GROUP_LANE0 + N_GROUPS].set(b_grp)
    w_hi = w.astype(BF16)
    w_lo = (w - w_hi.astype(F32)).astype(BF16)
    return jnp.concatenate([w_hi, w_lo], axis=1), b


def _rope_tables(positions):
    inv = ROPE_THETA ** (-jnp.arange(0, HEAD_DIM, 2, dtype=F32) / HEAD_DIM)
    ang = positions.astype(F32).reshape(N_TOK, 1) * inv
    cos = jnp.cos(ang)
    sin = jnp.sin(ang)
    cos_t = jnp.concatenate([cos, cos, cos, cos], axis=1)
    sin_t = jnp.concatenate([-sin, sin, -sin, sin], axis=1)
    return cos_t, sin_t


def _moe(layer, meta_i, counts_f, xn_slab, w_gate, w_up, w_down):
    pos0, pos1, vt, ve, gs, ge, nv, nx = _routing_tables(meta_i, counts_f)
    xs_slab = _dispatch(pos0, pos1, xn_slab)
    ys_slab = _experts(layer, vt, ve, gs, ge, nv, nx, xs_slab, w_gate, w_up, w_down)
    return pos0, pos1, ys_slab


def kernel(x, p, positions, g_mix, g_ffn, pool_w, pool_scale, g_kv, w_k, w_v, w_q, w_o, sinks, w_group_router, b_group_router, w_expert_router, b_expert_router, w_exp_gate, w_exp_up, w_exp_down, g_ple, w_ple_gate, w_ple_up, g_final):
    row = lambda a: a.reshape(1, -1)
    x2 = x.reshape(N_TOK, D_MODEL)
    p2 = p.reshape(2, N_TOK, PLE_DIM)
    cos_t, sin_t = _rope_tables(positions)
    assert MIX_TILE == ATT_TILE
    upper = jnp.triu(jnp.ones((MIX_TILE, MIX_TILE), BF16), k=1)
    w_q_perm = w_q[0].reshape(D_MODEL, N_KV_HEADS, Q_PER_KV, HEAD_DIM).transpose(0, 2, 1, 3).reshape(D_MODEL, D_MODEL)
    w_o_perm = w_o[0].reshape(N_KV_HEADS, Q_PER_KV, HEAD_DIM, D_MODEL).transpose(1, 0, 2, 3).reshape(D_MODEL, D_MODEL)

    wr0, br0 = _router_weights(w_group_router[0], b_group_router[0], w_expert_router[0], b_expert_router[0])
    h1, xn_slab, meta_i, meta_f, counts = _mixer0(
        x2, row(g_mix[0]), pool_w[0].astype(BF16), row(pool_scale[0]), row(g_ffn[0]), wr0, br0, upper)
    pos0, pos1, ys_slab = _moe(0, meta_i, counts, xn_slab, w_exp_gate, w_exp_up, w_exp_down)
    h3, k, v, q = _combine0(
        pos0, pos1, h1, meta_f, p2, cos_t, sin_t, row(g_ple[0]), w_ple_gate[0].astype(BF16),
        w_ple_up[0].astype(BF16), row(g_kv), w_k.astype(BF16), w_v.astype(BF16), row(g_mix[1]),
        w_q_perm.astype(BF16), ys_slab)

    wr1, br1 = _router_weights(w_group_router[1], b_group_router[1], w_expert_router[1], b_expert_router[1])
    h4, xn_slab, meta_i, meta_f, counts = _attn1(
        sinks[0], q, k, v, h3, w_o_perm.astype(BF16), row(g_ffn[1]), wr1, br1, upper)
    pos0, pos1, ys_slab = _moe(1, meta_i, counts, xn_slab, w_exp_gate, w_exp_up, w_exp_down)
    out = _combine1(pos0, pos1, h4, meta_f, p2, row(g_ple[1]), w_ple_gate[1].astype(BF16),
                    w_ple_up[1].astype(BF16), row(g_final), ys_slab)
    return out.reshape(BATCH, SEQ, D_MODEL)
```

```python
import functools

import jax
import jax.numpy as jnp
from jax import lax
from jax.experimental import pallas as pl
from jax.experimental.pallas import tpu as pltpu

D_MODEL = 1024
BATCH = 2
SEQ = 8192
N_TOK = BATCH * SEQ
POOL_WINDOWS = (2, 4, 8, 16)
POOL_GROUP = D_MODEL // len(POOL_WINDOWS)
POOL_HALO = 16
HEAD_DIM = 64
N_Q_HEADS = 16
N_KV_HEADS = 4
Q_PER_KV = 4
KV_DIM = N_KV_HEADS * HEAD_DIM
WINDOW = 128
ATT_BLOCK = 128
ROPE_THETA = 10000.0
N_GROUPS = 4
EXPERTS_PER_GROUP = 8
N_EXPERTS = 32
D_EXPERT = 512
PLE_DIM = 256
EPS = 1e-6

LANES = 128
SUBLANES = 8
HALF = D_MODEL // 2
SLAB_ROWS = HALF // LANES
N_ASSIGN = 2 * N_TOK

TOK_TILE = 256
MIX_TILE = 512
ATT_TILE = 512
DISPATCH_TILE = 2048
EXPERT_TILE = 256
N_ROW_TILES = N_ASSIGN // EXPERT_TILE
N_VISITS = N_ROW_TILES + N_EXPERTS - 1
GROUP_LANE0 = N_EXPERTS

F32 = jnp.float32
BF16 = jnp.bfloat16
U32 = jnp.uint32
NEG_INF = float("-inf")


def _rms(x, g):
    return x * lax.rsqrt(jnp.mean(x * x, axis=-1, keepdims=True) + EPS) * g


def _pack_rows(val):
    lo = pltpu.bitcast(val[:, :HALF].astype(BF16).astype(F32), U32)
    hi = pltpu.bitcast(val[:, HALF:].astype(BF16).astype(F32), U32)
    return hi | (lo >> 16)


def _unpack_rows(word):
    lo = pltpu.bitcast(word << 16, F32)
    hi = pltpu.bitcast(word & jnp.uint32(0xFFFF0000), F32)
    return jnp.concatenate([lo, hi], axis=1)


def _store_slab(ref, word):
    n = word.shape[0]
    for c in range(SLAB_ROWS):
        ref[pl.ds(c, n, stride=SLAB_ROWS), :] = word[:, c * LANES:(c + 1) * LANES]


def _load_slab(ref, n):
    return jnp.concatenate([ref[pl.ds(c, n, stride=SLAB_ROWS), :] for c in range(SLAB_ROWS)], axis=1)


def _route_and_rank(xn, wr_ref, br_ref, up_ref, run_ref, mi_ref, mf_ref, cnt_ref):
    n = xn.shape[0]
    x_hi = xn.astype(BF16)
    x_lo = (xn - x_hi.astype(F32)).astype(BF16)
    hh_hl = jnp.dot(x_hi, wr_ref[...], preferred_element_type=F32)
    lh = jnp.dot(x_lo, wr_ref[:, :LANES], preferred_element_type=F32)
    logits = (hh_hl[:, :LANES] + (hh_hl[:, LANES:] + lh) + br_ref[...]).T
    big = float(LANES)

    grow = lax.broadcasted_iota(jnp.int32, (SUBLANES, n), 0).astype(F32)
    gl = jnp.where(grow < N_GROUPS, logits[GROUP_LANE0:GROUP_LANE0 + SUBLANES, :], NEG_INF)
    gmax = jnp.max(gl, axis=0, keepdims=True)
    gidx = jnp.min(jnp.where(gl == gmax, grow, big), axis=0, keepdims=True)
    pg = 1.0 / jnp.sum(jnp.exp(gl - gmax), axis=0, keepdims=True)

    erow = lax.broadcasted_iota(jnp.int32, (N_EXPERTS, n), 0).astype(F32)
    lo = gidx * EXPERTS_PER_GROUP
    el = jnp.where((erow >= lo) & (erow < lo + EXPERTS_PER_GROUP), logits[:N_EXPERTS, :], NEG_INF)
    m1 = jnp.max(el, axis=0, keepdims=True)
    i1 = jnp.min(jnp.where(el == m1, erow, big), axis=0, keepdims=True)
    el2 = jnp.where(erow == i1, NEG_INF, el)
    m2 = jnp.max(el2, axis=0, keepdims=True)
    i2 = jnp.min(jnp.where(el2 == m2, erow, big), axis=0, keepdims=True)
    z = jnp.exp(m2 - m1)
    gate0 = pg / (1.0 + z)
    gate1 = pg * z / (1.0 + z)

    row = lax.broadcasted_iota(jnp.int32, (LANES, n), 0).astype(F32)
    oh0 = jnp.where(row == i1, 1.0, 0.0)
    oh1 = jnp.where(row == i2, 1.0, 0.0)
    oh = oh0 + oh1
    prefix = jnp.dot(oh.astype(BF16), up_ref[...], preferred_element_type=F32)
    tot = prefix + run_ref[...]
    rank0 = jnp.sum(oh0 * tot, axis=0, keepdims=True)
    rank1 = jnp.sum(oh1 * tot, axis=0, keepdims=True)
    new_run = run_ref[...] + jnp.sum(oh, axis=1, keepdims=True)
    run_ref[...] = new_run
    cnt_ref[...] = jnp.broadcast_to(new_run, cnt_ref.shape)

    r8 = lax.broadcasted_iota(jnp.int32, (SUBLANES, n), 0)
    meta = jnp.where(r8 == 0, i1, jnp.where(r8 == 1, i2, jnp.where(r8 == 2, rank0, jnp.where(r8 == 3, rank1, 0.0))))
    mi_ref[...] = meta.astype(jnp.int32)
    mf_ref[...] = jnp.where(row == 0.0, gate0, jnp.where(row == 1.0, gate1, 0.0)).T


def _mixer0_kernel(x_ref, halo_ref, gmix_ref, poolw_ref, pscale_ref, gffn_ref, wr_ref, br_ref, up_ref,
                   h_ref, xn_ref, mi_ref, mf_ref, cnt_ref, run_ref):
    i = pl.program_id(0)
    tiles_per_seq = SEQ // MIX_TILE

    @pl.when(i == 0)
    def _():
        run_ref[...] = jnp.zeros_like(run_ref)

    seq_tile = i % tiles_per_seq
    x = x_ref[...]
    g = gmix_ref[...]
    hn = _rms(x, g)
    halo = jnp.where(seq_tile == 0, 0.0, _rms(halo_ref[...], g))
    full = jnp.concatenate([halo, hn], axis=0)
    t = (seq_tile * MIX_TILE + lax.broadcasted_iota(jnp.int32, (MIX_TILE, 1), 0) + 1).astype(F32)
    outs = []
    for gi, w in enumerate(POOL_WINDOWS):
        acc = full[:, gi * POOL_GROUP:(gi + 1) * POOL_GROUP]
        shift = 1
        while shift < w:
            acc = acc + pltpu.roll(acc, shift, axis=0)
            shift *= 2
        win = acc[POOL_HALO:, :]
        mean = win / jnp.minimum(t, float(w))
        dlt = (mean - hn[:, gi * POOL_GROUP:(gi + 1) * POOL_GROUP]).astype(BF16)
        outs.append(jnp.dot(dlt, poolw_ref[gi], preferred_element_type=F32))
    y = jnp.concatenate(outs, axis=1)
    h = x + y * pscale_ref[...]
    h_ref[...] = h
    xn = _rms(h, gffn_ref[...])
    _store_slab(xn_ref, _pack_rows(xn))
    _route_and_rank(xn, wr_ref, br_ref, up_ref, run_ref, mi_ref, mf_ref, cnt_ref)


def _mixer0(x, g_mix, pool_w, pool_scale, g_ffn, w_r, b_r, upper):
    n_tiles = N_TOK // MIX_TILE
    halo_per_tile = MIX_TILE // POOL_HALO
    vec = lambda: pl.BlockSpec((1, D_MODEL), lambda i: (0, 0))
    return pl.pallas_call(
        _mixer0_kernel,
        grid=(n_tiles,),
        in_specs=[
            pl.BlockSpec((MIX_TILE, D_MODEL), lambda i: (i, 0)),
            pl.BlockSpec((POOL_HALO, D_MODEL), lambda i: (jnp.maximum(i * halo_per_tile - 1, 0), 0)),
            vec(),
            pl.BlockSpec((len(POOL_WINDOWS), POOL_GROUP, POOL_GROUP), lambda i: (0, 0, 0)),
            vec(), vec(),
            pl.BlockSpec((D_MODEL, 2 * LANES), lambda i: (0, 0)),
            pl.BlockSpec((1, LANES), lambda i: (0, 0)),
            pl.BlockSpec((MIX_TILE, MIX_TILE), lambda i: (0, 0)),
        ],
        out_specs=[
            pl.BlockSpec((MIX_TILE, D_MODEL), lambda i: (i, 0)),
            pl.BlockSpec((MIX_TILE * SLAB_ROWS, LANES), lambda i: (i, 0)),
            pl.BlockSpec((SUBLANES, MIX_TILE), lambda i: (0, i)),
            pl.BlockSpec((MIX_TILE, LANES), lambda i: (i, 0)),
            pl.BlockSpec((LANES, LANES), lambda i: (0, 0)),
        ],
        out_shape=[
            jax.ShapeDtypeStruct((N_TOK, D_MODEL), F32),
            jax.ShapeDtypeStruct((N_TOK * SLAB_ROWS, LANES), U32),
            jax.ShapeDtypeStruct((SUBLANES, N_TOK), jnp.int32),
            jax.ShapeDtypeStruct((N_TOK, LANES), F32),
            jax.ShapeDtypeStruct((LANES, LANES), F32),
        ],
        scratch_shapes=[pltpu.VMEM((LANES, 1), F32)],
        compiler_params=pltpu.CompilerParams(dimension_semantics=("arbitrary",), vmem_limit_bytes=48 << 20),
        name="mixer0",
    )(x, x, g_mix, pool_w, pool_scale, g_ffn, w_r, b_r, upper)


def _row_copy(src, src_row, dst, dst_row, sem):
    return pltpu.make_async_copy(
        src.at[pl.ds(pl.multiple_of(src_row * SLAB_ROWS, SLAB_ROWS), SLAB_ROWS), :],
        dst.at[pl.ds(pl.multiple_of(dst_row * SLAB_ROWS, SLAB_ROWS), SLAB_ROWS), :],
        sem)


def _dispatch_kernel(pos0_ref, pos1_ref, src_ref, dst_ref, sem):
    unroll = 8

    def issue(jo, carry):
        for u in range(unroll):
            j = jo * unroll + u
            _row_copy(src_ref, j, dst_ref, pos0_ref[j], sem).start(priority=0)
            _row_copy(src_ref, j, dst_ref, pos1_ref[j], sem).start(priority=1)
        return carry

    lax.fori_loop(0, DISPATCH_TILE // unroll, issue, 0)

    def drain(jo, carry):
        for u in range(2 * unroll):
            _row_copy(src_ref, 0, dst_ref, 0, sem).wait()
        return carry

    lax.fori_loop(0, DISPATCH_TILE // unroll, drain, 0)


def _dispatch(pos0, pos1, xn_slab):
    smem = lambda: pl.BlockSpec((DISPATCH_TILE,), lambda i: (i,), memory_space=pltpu.SMEM)
    return pl.pallas_call(
        _dispatch_kernel,
        grid=(N_TOK // DISPATCH_TILE,),
        in_specs=[smem(), smem(), pl.BlockSpec((DISPATCH_TILE * SLAB_ROWS, LANES), lambda i: (i, 0))],
        out_specs=pl.BlockSpec(memory_space=pl.ANY),
        out_shape=jax.ShapeDtypeStruct((N_ASSIGN * SLAB_ROWS, LANES), U32),
        scratch_shapes=[pltpu.SemaphoreType.DMA(())],
        compiler_params=pltpu.CompilerParams(dimension_semantics=("arbitrary",), has_side_effects=True),
        name="dispatch",
    )(pos0, pos1, xn_slab)


X_SLOTS = 3
Y_SLOTS = 2


def _experts_kernel(layer, vt_ref, ve_ref, gs_ref, ge_ref, nv_ref, nx_ref, xs_hbm, wg_hbm, wu_hbm, wd_hbm, ys_hbm,
                    xb, yb, wg_st, wu_st, wd_st, wgu_s, wd_s, xsem, ysem, wsem):
    rows = EXPERT_TILE * SLAB_ROWS
    n_visits = nv_ref[0]

    def x_copy(t):
        return pltpu.make_async_copy(xs_hbm.at[pl.ds(pl.multiple_of(t * rows, rows), rows), :],
                                     xb.at[t % X_SLOTS], xsem.at[t % X_SLOTS])

    def y_copy(t):
        return pltpu.make_async_copy(yb.at[t % Y_SLOTS],
                                     ys_hbm.at[pl.ds(pl.multiple_of(t * rows, rows), rows), :], ysem.at[t % Y_SLOTS])

    def w_copies(ex):
        return (pltpu.make_async_copy(wg_hbm.at[layer, ex], wg_st, wsem.at[0]),
                pltpu.make_async_copy(wu_hbm.at[layer, ex], wu_st, wsem.at[1]),
                pltpu.make_async_copy(wd_hbm.at[layer, ex], wd_st, wsem.at[2]))

    x_copy(0).start()
    x_copy(1).start()
    for c in w_copies(ve_ref[0]):
        c.start()

    def visit(i, carry):
        prev = jnp.maximum(i - 1, 0)
        e = ve_ref[i]
        tile = vt_ref[i]
        new_expert = (i == 0) | (e != ve_ref[prev])
        new_tile = (i == 0) | (tile != vt_ref[prev])

        @pl.when(new_tile)
        def _():
            x_copy(tile).wait()

            @pl.when(tile + 2 < N_ROW_TILES)
            def _():
                x_copy(tile + 2).start()

            @pl.when(tile >= 2)
            def _():
                y_copy(tile - 2).wait()

            @pl.when(tile >= 1)
            def _():
                y_copy(tile - 1).start()

        @pl.when(new_expert)
        def _():
            for c in w_copies(e):
                c.wait()
            wgu_s[:, :D_EXPERT] = wg_st[...].astype(BF16)
            wgu_s[:, D_EXPERT:] = wu_st[...].astype(BF16)
            wd_s[...] = wd_st[...].astype(BF16)
            nxt = nx_ref[e]

            @pl.when(nxt != e)
            def _():
                for c in w_copies(nxt):
                    c.start()

        xs = tile % X_SLOTS
        ys = tile % Y_SLOTS
        x = _unpack_rows(jnp.concatenate(
            [xb[xs, pl.ds(c, EXPERT_TILE, stride=SLAB_ROWS), :] for c in range(SLAB_ROWS)], axis=1))
        gu = jnp.dot(x.astype(BF16), wgu_s[...], preferred_element_type=F32)
        gate = gu[:, :D_EXPERT]
        act = (gate / (1.0 + jnp.exp(-gate)) * gu[:, D_EXPERT:]).astype(BF16)
        y = _pack_rows(jnp.dot(act, wd_s[...], preferred_element_type=F32))
        r = tile * EXPERT_TILE + lax.broadcasted_iota(jnp.int32, (EXPERT_TILE, 1), 0)
        mine = (r >= gs_ref[e]) & (r < ge_ref[e])

        @pl.when(new_tile)
        def _():
            for c in range(SLAB_ROWS):
                yb[ys, pl.ds(c, EXPERT_TILE, stride=SLAB_ROWS), :] = jnp.where(
                    mine, y[:, c * LANES:(c + 1) * LANES], jnp.uint32(0))

        @pl.when(jnp.logical_not(new_tile))
        def _():
            for c in range(SLAB_ROWS):
                old = yb[ys, pl.ds(c, EXPERT_TILE, stride=SLAB_ROWS), :]
                yb[ys, pl.ds(c, EXPERT_TILE, stride=SLAB_ROWS), :] = jnp.where(mine, y[:, c * LANES:(c + 1) * LANES], old)

        return carry

    lax.fori_loop(0, n_visits, visit, 0)
    last_tile = vt_ref[n_visits - 1]
    y_copy(last_tile).start()
    y_copy(last_tile).wait()

    @pl.when(last_tile >= 1)
    def _():
        y_copy(last_tile - 1).wait()


def _experts(layer, visit_tile, visit_expert, g_start, g_end, n_visits, next_expert, xs_slab, w_gate, w_up, w_down):
    rows = EXPERT_TILE * SLAB_ROWS
    any_spec = lambda: pl.BlockSpec(memory_space=pl.ANY)
    grid_spec = pltpu.PrefetchScalarGridSpec(
        num_scalar_prefetch=6,
        grid=(1,),
        in_specs=[any_spec(), any_spec(), any_spec(), any_spec()],
        out_specs=any_spec(),
        scratch_shapes=[pltpu.VMEM((X_SLOTS, rows, LANES), U32), pltpu.VMEM((Y_SLOTS, rows, LANES), U32),
                        pltpu.VMEM((D_MODEL, D_EXPERT), F32), pltpu.VMEM((D_MODEL, D_EXPERT), F32),
                        pltpu.VMEM((D_EXPERT, D_MODEL), F32),
                        pltpu.VMEM((D_MODEL, 2 * D_EXPERT), BF16), pltpu.VMEM((D_EXPERT, D_MODEL), BF16),
                        pltpu.SemaphoreType.DMA((X_SLOTS,)), pltpu.SemaphoreType.DMA((Y_SLOTS,)),
                        pltpu.SemaphoreType.DMA((3,))],
    )
    return pl.pallas_call(
        functools.partial(_experts_kernel, layer),
        grid_spec=grid_spec,
        out_shape=jax.ShapeDtypeStruct((N_ASSIGN * SLAB_ROWS, LANES), U32),
        compiler_params=pltpu.CompilerParams(dimension_semantics=("arbitrary",), vmem_limit_bytes=48 << 20,
                                             has_side_effects=True),
        name="experts",
    )(visit_tile, visit_expert, g_start, g_end, n_visits, next_expert, xs_slab, w_gate, w_up, w_down)


def _with_expert_rows(pos0_ref, pos1_ref, npos0_ref, npos1_ref, ys_ref, bufs, sem, body):
    i = pl.program_id(0)

    def copies(p0_ref, p1_ref, s):
        for j in range(TOK_TILE):
            yield _row_copy(ys_ref, p0_ref[j], bufs[s][0], j, sem.at[s])
            yield _row_copy(ys_ref, p1_ref[j], bufs[s][1], j, sem.at[s])

    @pl.when(i == 0)
    def _():
        for n, c in enumerate(copies(pos0_ref, pos1_ref, 0)):
            c.start(priority=n % 2)

    for s in range(2):
        @pl.when(i % 2 == s)
        def _():
            for n, c in enumerate(copies(npos0_ref, npos1_ref, 1 - s)):
                c.start(priority=n % 2)
            for c in copies(pos0_ref, pos1_ref, s):
                c.wait()
            body(_unpack_rows(_load_slab(bufs[s][0], TOK_TILE)), _unpack_rows(_load_slab(bufs[s][1], TOK_TILE)))

            @pl.when(i == pl.num_programs(0) - 1)
            def _():
                for c in copies(npos0_ref, npos1_ref, 1 - s):
                    c.wait()


def _moe_residual_and_ple(h_ref, mf_ref, y0, y1, p_ref, gple_ref, wpg_ref, wpu_ref):
    gates = mf_ref[...]
    h = h_ref[...] + (y0 * gates[:, 0:1] + y1 * gates[:, 1:2])
    hn = _rms(h, gple_ref[...]).astype(BF16)
    gate = jnp.dot(hn, wpg_ref[...], preferred_element_type=F32)
    gate = 1.0 / (1.0 + jnp.exp(-gate))
    up = jnp.dot(p_ref[...].astype(BF16), wpu_ref[...], preferred_element_type=F32)
    return h + gate * up


def _rope(x, cos, sin_signed):
    n, width = x.shape
    reps = width // LANES
    cos_t = jnp.concatenate([cos] * reps, axis=1)
    sin_t = jnp.concatenate([sin_signed] * reps, axis=1)
    lane = lax.broadcasted_iota(jnp.int32, (n, width), 1)
    first_half = (lane % HEAD_DIM) < (HEAD_DIM // 2)
    partner = jnp.where(first_half, pltpu.roll(x, width - HEAD_DIM // 2, axis=1), pltpu.roll(x, HEAD_DIM // 2, axis=1))
    return x * cos_t + partner * sin_t


def _combine0_kernel(pos0_ref, pos1_ref, npos0_ref, npos1_ref, h_ref, mf_ref, p_ref, cos_ref, sin_ref, gple_ref, wpg_ref, wpu_ref,
                     gkv_ref, wk_ref, wv_ref, gq_ref, wq_ref, ys_ref,
                     ho_ref, k_ref, v_ref, q_ref, b00, b01, b10, b11, sem):
    def body(y0, y1):
        h = _moe_residual_and_ple(h_ref, mf_ref, y0, y1, p_ref, gple_ref, wpg_ref, wpu_ref)
        ho_ref[...] = h
        cos = cos_ref[...]
        sin = sin_ref[...]
        hs = h * lax.rsqrt(jnp.mean(h * h, axis=-1, keepdims=True) + EPS)
        kvn = (hs * gkv_ref[...]).astype(BF16)
        k = jnp.dot(kvn, wk_ref[...], preferred_element_type=F32)
        k_ref[...] = _rope(k, cos, sin).astype(BF16)
        v_ref[...] = jnp.dot(kvn, wv_ref[...], preferred_element_type=F32).astype(BF16)
        qn = (hs * gq_ref[...]).astype(BF16)
        q = jnp.dot(qn, wq_ref[...], preferred_element_type=F32)
        q = (_rope(q, cos, sin) * (HEAD_DIM ** -0.5)).astype(BF16)
        for g in range(Q_PER_KV):
            q_ref[g] = q[:, g * KV_DIM:(g + 1) * KV_DIM]

    _with_expert_rows(pos0_ref, pos1_ref, npos0_ref, npos1_ref, ys_ref, ((b00, b01), (b10, b11)), sem, body)


def _combine1_kernel(pos0_ref, pos1_ref, npos0_ref, npos1_ref, h_ref, mf_ref, p_ref, gple_ref, wpg_ref, wpu_ref, gfin_ref, ys_ref,
                     out_ref, b00, b01, b10, b11, sem):
    def body(y0, y1):
        h = _moe_residual_and_ple(h_ref, mf_ref, y0, y1, p_ref, gple_ref, wpg_ref, wpu_ref)
        out_ref[...] = _rms(h, gfin_ref[...])

    _with_expert_rows(pos0_ref, pos1_ref, npos0_ref, npos1_ref, ys_ref, ((b00, b01), (b10, b11)), sem, body)


def _combine_specs():
    last = N_TOK // TOK_TILE - 1
    smem = lambda: pl.BlockSpec((TOK_TILE,), lambda i: (i,), memory_space=pltpu.SMEM)
    smem_next = lambda: pl.BlockSpec((TOK_TILE,), lambda i: (jnp.minimum(i + 1, last),), memory_space=pltpu.SMEM)
    tok = lambda w: pl.BlockSpec((TOK_TILE, w), lambda i: (i, 0))
    full = lambda a, b: pl.BlockSpec((a, b), lambda i: (0, 0))
    scratch = [pltpu.VMEM((TOK_TILE * SLAB_ROWS, LANES), U32) for _ in range(4)] + [pltpu.SemaphoreType.DMA((2,))]
    ple = lambda layer: pl.BlockSpec((None, TOK_TILE, PLE_DIM), lambda i: (layer, i, 0))
    return smem, smem_next, tok, ple, full, scratch


def _combine0(pos0, pos1, h, mf, p, cos_t, sin_t, g_ple, w_pg, w_pu, g_kv, w_k, w_v, g_q, w_q, ys_slab):
    smem, smem_next, tok, ple, full, scratch = _combine_specs()
    return pl.pallas_call(
        _combine0_kernel,
        grid=(N_TOK // TOK_TILE,),
        in_specs=[smem(), smem(), smem_next(), smem_next(), tok(D_MODEL), tok(LANES), ple(0), tok(LANES), tok(LANES),
                  full(1, D_MODEL), full(D_MODEL, D_MODEL), full(PLE_DIM, D_MODEL),
                  full(1, D_MODEL), full(D_MODEL, KV_DIM), full(D_MODEL, KV_DIM),
                  full(1, D_MODEL), full(D_MODEL, D_MODEL),
                  pl.BlockSpec(memory_space=pl.ANY)],
        out_specs=[tok(D_MODEL), tok(KV_DIM), tok(KV_DIM),
                   pl.BlockSpec((Q_PER_KV, TOK_TILE, KV_DIM), lambda i: (0, i, 0))],
        out_shape=[jax.ShapeDtypeStruct((N_TOK, D_MODEL), F32),
                   jax.ShapeDtypeStruct((N_TOK, KV_DIM), BF16),
                   jax.ShapeDtypeStruct((N_TOK, KV_DIM), BF16),
                   jax.ShapeDtypeStruct((Q_PER_KV, N_TOK, KV_DIM), BF16)],
        scratch_shapes=scratch,
        compiler_params=pltpu.CompilerParams(dimension_semantics=("arbitrary",), vmem_limit_bytes=56 << 20),
        name="combine0",
    )(pos0, pos1, pos0, pos1, h, mf, p, cos_t, sin_t, g_ple, w_pg, w_pu, g_kv, w_k, w_v, g_q, w_q, ys_slab)


def _combine1(pos0, pos1, h, mf, p, g_ple, w_pg, w_pu, g_fin, ys_slab):
    smem, smem_next, tok, ple, full, scratch = _combine_specs()
    return pl.pallas_call(
        _combine1_kernel,
        grid=(N_TOK // TOK_TILE,),
        in_specs=[smem(), smem(), smem_next(), smem_next(), tok(D_MODEL), tok(LANES), ple(1),
                  full(1, D_MODEL), full(D_MODEL, D_MODEL), full(PLE_DIM, D_MODEL), full(1, D_MODEL),
                  pl.BlockSpec(memory_space=pl.ANY)],
        out_specs=tok(D_MODEL),
        out_shape=jax.ShapeDtypeStruct((N_TOK, D_MODEL), F32),
        scratch_shapes=scratch,
        compiler_params=pltpu.CompilerParams(dimension_semantics=("arbitrary",), vmem_limit_bytes=48 << 20),
        name="combine1",
    )(pos0, pos1, pos0, pos1, h, mf, p, g_ple, w_pg, w_pu, g_fin, ys_slab)


def _attn1_kernel(sink_ref, q_ref, kc_ref, kp_ref, vc_ref, vp_ref, h_ref, wo_ref, gffn_ref, wr_ref, br_ref, up_ref,
                  ho_ref, xn_ref, mi_ref, mf_ref, cnt_ref, run_ref, k_s, v_s, o_s):
    i = pl.program_id(0)
    steps_per_seq = SEQ // ATT_TILE

    @pl.when(i == 0)
    def _():
        run_ref[...] = jnp.zeros_like(run_ref)

    k_s[:ATT_BLOCK, :] = kp_ref[...]
    k_s[ATT_BLOCK:, :] = kc_ref[...]
    v_s[:ATT_BLOCK, :] = vp_ref[...]
    v_s[ATT_BLOCK:, :] = vc_ref[...]
    seq_start = (i % steps_per_seq) == 0

    n_rows = Q_PER_KV * ATT_BLOCK
    assert WINDOW == ATT_BLOCK
    row = lax.broadcasted_iota(jnp.int32, (n_rows, ATT_BLOCK), 0)
    col = lax.broadcasted_iota(jnp.int32, (n_rows, ATT_BLOCK), 1)
    upper = col > (row % ATT_BLOCK)
    head_of_lane = lax.broadcasted_iota(jnp.int32, (2 * ATT_BLOCK, KV_DIM), 1) // HEAD_DIM
    grp_of_row = lax.broadcasted_iota(jnp.int32, (n_rows, 1), 0) // ATT_BLOCK

    def block(b, carry):
        r0 = b * ATT_BLOCK
        has_prev = jnp.logical_not(seq_start) if b == 0 else True
        kcat = k_s[pl.ds(r0, 2 * ATT_BLOCK), :]
        vcat = v_s[pl.ds(r0, 2 * ATT_BLOCK), :]
        qs = jnp.concatenate([q_ref[g, pl.ds(r0, ATT_BLOCK), :] for g in range(Q_PER_KV)], axis=0)
        o = jnp.zeros((n_rows, KV_DIM), F32)
        for kh in range(N_KV_HEADS):
            k_h = jnp.where(head_of_lane == kh, kcat, jnp.zeros_like(kcat))
            v_h = jnp.where(head_of_lane == kh, vcat, jnp.zeros_like(vcat))
            s = lax.dot_general(qs, k_h, (((1,), (1,)), ((), ())), preferred_element_type=F32)
            s = jnp.where(upper, jnp.where(has_prev, s[:, :ATT_BLOCK], NEG_INF), s[:, ATT_BLOCK:])
            sink = jnp.zeros((n_rows, 1), F32)
            for g in range(Q_PER_KV):
                sink = jnp.where(grp_of_row == g, sink_ref[kh * Q_PER_KV + g], sink)
            m = jnp.maximum(jnp.max(s, axis=1, keepdims=True), sink)
            e = jnp.exp(s - m)
            denom = jnp.sum(e, axis=1, keepdims=True) + jnp.exp(sink - m)
            pr = e / denom
            pr = jnp.concatenate([jnp.where(upper, pr, 0.0), jnp.where(upper, 0.0, pr)], axis=1).astype(BF16)
            o = o + jnp.dot(pr, v_h, preferred_element_type=F32)
        for g in range(Q_PER_KV):
            o_s[g, pl.ds(r0, ATT_BLOCK), :] = o[g * ATT_BLOCK:(g + 1) * ATT_BLOCK, :].astype(BF16)
        return carry

    for b in range(ATT_TILE // ATT_BLOCK):
        block(b, 0)
    o_cat = jnp.concatenate([o_s[g] for g in range(Q_PER_KV)], axis=1)
    h = h_ref[...] + jnp.dot(o_cat, wo_ref[...], preferred_element_type=F32)
    ho_ref[...] = h
    xn = _rms(h, gffn_ref[...])
    _store_slab(xn_ref, _pack_rows(xn))
    _route_and_rank(xn, wr_ref, br_ref, up_ref, run_ref, mi_ref, mf_ref, cnt_ref)


def _attn1(sinks, q, k, v, h, w_o, g_ffn, w_r, b_r, upper):
    blocks_per_step = ATT_TILE // ATT_BLOCK
    cur = lambda w: pl.BlockSpec((ATT_TILE, w), lambda i: (i, 0))
    prev = lambda w: pl.BlockSpec((ATT_BLOCK, w), lambda i: (jnp.maximum(i * blocks_per_step - 1, 0), 0))
    full = lambda a, b: pl.BlockSpec((a, b), lambda i: (0, 0))
    return pl.pallas_call(
        _attn1_kernel,
        grid=(N_TOK // ATT_TILE,),
        in_specs=[pl.BlockSpec(memory_space=pltpu.SMEM),
                  pl.BlockSpec((Q_PER_KV, ATT_TILE, KV_DIM), lambda i: (0, i, 0)),
                  cur(KV_DIM), prev(KV_DIM), cur(KV_DIM), prev(KV_DIM), cur(D_MODEL),
                  full(D_MODEL, D_MODEL), full(1, D_MODEL), full(D_MODEL, 2 * LANES), full(1, LANES),
                  full(ATT_TILE, ATT_TILE)],
        out_specs=[cur(D_MODEL),
                   pl.BlockSpec((ATT_TILE * SLAB_ROWS, LANES), lambda i: (i, 0)),
                   pl.BlockSpec((SUBLANES, ATT_TILE), lambda i: (0, i)), cur(LANES),
                   pl.BlockSpec((LANES, LANES), lambda i: (0, 0))],
        out_shape=[jax.ShapeDtypeStruct((N_TOK, D_MODEL), F32),
                   jax.ShapeDtypeStruct((N_TOK * SLAB_ROWS, LANES), U32),
                   jax.ShapeDtypeStruct((SUBLANES, N_TOK), jnp.int32),
                   jax.ShapeDtypeStruct((N_TOK, LANES), F32),
                   jax.ShapeDtypeStruct((LANES, LANES), F32)],
        scratch_shapes=[pltpu.VMEM((LANES, 1), F32),
                        pltpu.VMEM((ATT_TILE + ATT_BLOCK, KV_DIM), BF16),
                        pltpu.VMEM((ATT_TILE + ATT_BLOCK, KV_DIM), BF16),
                        pltpu.VMEM((Q_PER_KV, ATT_TILE, KV_DIM), BF16)],
        compiler_params=pltpu.CompilerParams(dimension_semantics=("arbitrary",), vmem_limit_bytes=48 << 20),
        name="attn1",
    )(sinks, q, k, k, v, v, h, w_o, g_ffn, w_r, b_r, upper)


def _plan_kernel(cnt_ref, meta_ref, pos_ref, vt_ref, ve_ref, gs_ref, ge_ref, nv_ref, nx_ref):
    def per_expert(e, carry):
        off, v = carry
        c = cnt_ref[e]
        gs_ref[e] = off
        ge_ref[e] = off + c
        first = off // EXPERT_TILE
        n_vis = jnp.where(c > 0, (off + c - 1) // EXPERT_TILE - first + 1, 0)

        def visit(j, _):
            vt_ref[v + j] = first + j
            ve_ref[v + j] = e
            return 0

        lax.fori_loop(0, n_vis, visit, 0)
        return off + c, v + n_vis

    _, total = lax.fori_loop(0, N_EXPERTS, per_expert, (jnp.int32(0), jnp.int32(0)))
    nv_ref[0] = total

    def pad(j, _):
        vt_ref[j] = vt_ref[total - 1]
        ve_ref[j] = ve_ref[total - 1]
        return 0

    lax.fori_loop(total, N_VISITS, pad, 0)

    def link(k, nxt):
        e = N_EXPERTS - 1 - k
        nx_ref[e] = jnp.where(nxt >= 0, nxt, e)
        return jnp.where(cnt_ref[e] > 0, e, nxt)

    lax.fori_loop(0, N_EXPERTS, link, jnp.int32(-1))

    meta = meta_ref[...]
    start = jnp.zeros_like(meta)
    for e in range(N_EXPERTS):
        start = jnp.where(meta == e, gs_ref[e], start)
    pos_ref[...] = start + pltpu.roll(meta, SUBLANES - 2, axis=0)


def _routing_tables(meta_i, counts_f):
    counts = counts_f[:N_EXPERTS, 0].astype(jnp.int32)
    smem = lambda: pl.BlockSpec(memory_space=pltpu.SMEM)
    i32 = lambda n: jax.ShapeDtypeStruct((n,), jnp.int32)
    pos, vt, ve, gs, ge, nv, nx = pl.pallas_call(
        _plan_kernel,
        in_specs=[smem(), pl.BlockSpec(memory_space=pltpu.VMEM)],
        out_specs=[pl.BlockSpec(memory_space=pltpu.VMEM), smem(), smem(), smem(), smem(), smem(), smem()],
        out_shape=[jax.ShapeDtypeStruct((SUBLANES, N_TOK), jnp.int32),
                   i32(N_VISITS), i32(N_VISITS), i32(N_EXPERTS), i32(N_EXPERTS), i32(1), i32(N_EXPERTS)],
        name="plan",
    )(counts, meta_i)
    return pos[0], pos[1], vt, ve, gs, ge, nv, nx


def _router_weights(w_grp, b_grp, w_exp, b_exp):
    w = jnp.zeros((D_MODEL, LANES), F32)
    w = w.at[:, :N_EXPERTS].set(w_exp).at[:, GROUP_LANE0:GROUP_LANE0 + N_GROUPS].set(w_grp)
    b = jnp.zeros((1, LANES), F32)
    b = b.at[0, :N_EXPERTS].set(b_exp).at[0, GROUP_LANE0:GROUP_LANE0 + N_GROUPS].set(b_grp)
    w_hi = w.astype(BF16)
    w_lo = (w - w_hi.astype(F32)).astype(BF16)
    return jnp.concatenate([w_hi, w_lo], axis=1), b


def _rope_tables(positions):
    inv = ROPE_THETA ** (-jnp.arange(0, HEAD_DIM, 2, dtype=F32) / HEAD_DIM)
    ang = positions.astype(F32).reshape(N_TOK, 1) * inv
    cos = jnp.cos(ang)
    sin = jnp.sin(ang)
    cos_t = jnp.concatenate([cos, cos, cos, cos], axis=1)
    sin_t = jnp.concatenate([-sin, sin, -sin, sin], axis=1)
    return cos_t, sin_t


def _moe(layer, meta_i, counts_f, xn_slab, w_gate, w_up, w_down):
    pos0, pos1, vt, ve, gs, ge, nv, nx = _routing_tables(meta_i, counts_f)
    xs_slab = _dispatch(pos0, pos1, xn_slab)
    ys_slab = _experts(layer, vt, ve, gs, ge, nv, nx, xs_slab, w_gate, w_up, w_down)
    return pos0, pos1, ys_slab


def kernel(x, p, positions, g_mix, g_ffn, pool_w, pool_scale, g_kv, w_k, w_v, w_q, w_o, sinks, w_group_router, b_group_router, w_expert_router, b_expert_router, w_exp_gate, w_exp_up, w_exp_down, g_ple, w_ple_gate, w_ple_up, g_final):
    row = lambda a: a.reshape(1, -1)
    x2 = x.reshape(N_TOK, D_MODEL)
    p2 = p.reshape(2, N_TOK, PLE_DIM)
    cos_t, sin_t = _rope_tables(positions)
    assert MIX_TILE == ATT_TILE
    upper = jnp.triu(jnp.ones((MIX_TILE, MIX_TILE), BF16), k=1)
    w_q_perm = w_q[0].reshape(D_MODEL, N_KV_HEADS, Q_PER_KV, HEAD_DIM).transpose(0, 2, 1, 3).reshape(D_MODEL, D_MODEL)
    w_o_perm = w_o[0].reshape(N_KV_HEADS, Q_PER_KV, HEAD_DIM, D_MODEL).transpose(1, 0, 2, 3).reshape(D_MODEL, D_MODEL)

    wr0, br0 = _router_weights(w_group_router[0], b_group_router[0], w_expert_router[0], b_expert_router[0])
    h1, xn_slab, meta_i, meta_f, counts = _mixer0(
        x2, row(g_mix[0]), pool_w[0].astype(BF16), row(pool_scale[0]), row(g_ffn[0]), wr0, br0, upper)
    pos0, pos1, ys_slab = _moe(0, meta_i, counts, xn_slab, w_exp_gate, w_exp_up, w_exp_down)
    h3, k, v, q = _combine0(
        pos0, pos1, h1, meta_f, p2, cos_t, sin_t, row(g_ple[0]), w_ple_gate[0].astype(BF16),
        w_ple_up[0].astype(BF16), row(g_kv), w_k.astype(BF16), w_v.astype(BF16), row(g_mix[1]),
        w_q_perm.astype(BF16), ys_slab)

    wr1, br1 = _router_weights(w_group_router[1], b_group_router[1], w_expert_router[1], b_expert_router[1])
    h4, xn_slab, meta_i, meta_f, counts = _attn1(
        sinks[0], q, k, v, h3, w_o_perm.astype(BF16), row(g_ffn[1]), wr1, br1, upper)
    pos0, pos1, ys_slab = _moe(1, meta_i, counts, xn_slab, w_exp_gate, w_exp_up, w_exp_down)
    out = _combine1(pos0, pos1, h4, meta_f, p2, row(g_ple[1]), w_ple_gate[1].astype(BF16),
                    w_ple_up[1].astype(BF16), row(g_final), ys_slab)
    return out.reshape(BATCH, SEQ, D_MODEL)
```

```python
import functools

import jax
import jax.numpy as jnp
from jax import lax
from jax.experimental import pallas as pl
from jax.experimental.pallas import tpu as pltpu

D_MODEL = 1024
BATCH = 2
SEQ = 8192
N_TOK = BATCH * SEQ
POOL_WINDOWS = (2, 4, 8, 16)
POOL_GROUP = D_MODEL // len(POOL_WINDOWS)
POOL_HALO = 16
HEAD_DIM = 64
N_Q_HEADS = 16
N_KV_HEADS = 4
Q_PER_KV = 4
KV_DIM = N_KV_HEADS * HEAD_DIM
WINDOW = 128
ATT_BLOCK = 128
ROPE_THETA = 10000.0
N_GROUPS = 4
EXPERTS_PER_GROUP = 8
N_EXPERTS = 32
D_EXPERT = 512
PLE_DIM = 256
EPS = 1e-6

LANES = 128
SUBLANES = 8
HALF = D_MODEL // 2
SLAB_ROWS = HALF // LANES
N_ASSIGN = 2 * N_TOK

TOK_TILE = 256
MIX_TILE = 512
ATT_TILE = 512
DISPATCH_TILE = 2048
EXPERT_TILE = 256
N_ROW_TILES = N_ASSIGN // EXPERT_TILE
N_VISITS = N_ROW_TILES + N_EXPERTS - 1
GROUP_LANE0 = N_EXPERTS

F32 = jnp.float32
BF16 = jnp.bfloat16
U32 = jnp.uint32
NEG_INF = float("-inf")


def _rms(x, g):
    return x * lax.rsqrt(jnp.mean(x * x, axis=-1, keepdims=True) + EPS) * g


def _pack_rows(val):
    lo = pltpu.bitcast(val[:, :HALF].astype(BF16).astype(F32), U32)
    hi = pltpu.bitcast(val[:, HALF:].astype(BF16).astype(F32), U32)
    return hi | (lo >> 16)


def _unpack_rows(word):
    lo = pltpu.bitcast(word << 16, F32)
    hi = pltpu.bitcast(word & jnp.uint32(0xFFFF0000), F32)
    return jnp.concatenate([lo, hi], axis=1)


def _store_slab(ref, word):
    n = word.shape[0]
    for c in range(SLAB_ROWS):
        ref[pl.ds(c, n, stride=SLAB_ROWS), :] = word[:, c * LANES:(c + 1) * LANES]


def _load_slab(ref, n):
    return jnp.concatenate([ref[pl.ds(c, n, stride=SLAB_ROWS), :] for c in range(SLAB_ROWS)], axis=1)


def _route_and_rank(xn, wr_ref, br_ref, up_ref, run_ref, mi_ref, mf_ref, cnt_ref):
    n = xn.shape[0]
    x_hi = xn.astype(BF16)
    x_lo = (xn - x_hi.astype(F32)).astype(BF16)
    hh_hl = jnp.dot(x_hi, wr_ref[...], preferred_element_type=F32)
    lh = jnp.dot(x_lo, wr_ref[:, :LANES], preferred_element_type=F32)
    logits = (hh_hl[:, :LANES] + (hh_hl[:, LANES:] + lh) + br_ref[...]).T
    big = float(LANES)

    grow = lax.broadcasted_iota(jnp.int32, (SUBLANES, n), 0).astype(F32)
    gl = jnp.where(grow < N_GROUPS, logits[GROUP_LANE0:GROUP_LANE0 + SUBLANES, :], NEG_INF)
    gmax = jnp.max(gl, axis=0, keepdims=True)
    gidx = jnp.min(jnp.where(gl == gmax, grow, big), axis=0, keepdims=True)
    pg = 1.0 / jnp.sum(jnp.exp(gl - gmax), axis=0, keepdims=True)

    erow = lax.broadcasted_iota(jnp.int32, (N_EXPERTS, n), 0).astype(F32)
    lo = gidx * EXPERTS_PER_GROUP
    el = jnp.where((erow >= lo) & (erow < lo + EXPERTS_PER_GROUP), logits[:N_EXPERTS, :], NEG_INF)
    m1 = jnp.max(el, axis=0, keepdims=True)
    i1 = jnp.min(jnp.where(el == m1, erow, big), axis=0, keepdims=True)
    el2 = jnp.where(erow == i1, NEG_INF, el)
    m2 = jnp.max(el2, axis=0, keepdims=True)
    i2 = jnp.min(jnp.where(el2 == m2, erow, big), axis=0, keepdims=True)
    z = jnp.exp(m2 - m1)
    gate0 = pg / (1.0 + z)
    gate1 = pg * z / (1.0 + z)

    row = lax.broadcasted_iota(jnp.int32, (LANES, n), 0).astype(F32)
    oh0 = jnp.where(row == i1, 1.0, 0.0)
    oh1 = jnp.where(row == i2, 1.0, 0.0)
    oh = oh0 + oh1
    prefix = jnp.dot(oh.astype(BF16), up_ref[...], preferred_element_type=F32)
    tot = prefix + run_ref[...]
    rank0 = jnp.sum(oh0 * tot, axis=0, keepdims=True)
    rank1 = jnp.sum(oh1 * tot, axis=0, keepdims=True)
    new_run = run_ref[...] + jnp.sum(oh, axis=1, keepdims=True)
    run_ref[...] = new_run
    cnt_ref[...] = jnp.broadcast_to(new_run, cnt_ref.shape)

    r8 = lax.broadcasted_iota(jnp.int32, (SUBLANES, n), 0)
    meta = jnp.where(r8 == 0, i1, jnp.where(r8 == 1, i2, jnp.where(r8 == 2, rank0, jnp.where(r8 == 3, rank1, 0.0))))
    mi_ref[...] = meta.astype(jnp.int32)
    mf_ref[...] = jnp.where(row == 0.0, gate0, jnp.where(row == 1.0, gate1, 0.0)).T


def _mixer0_kernel(x_ref, halo_ref, gmix_ref, poolw_ref, pscale_ref, gffn_ref, wr_ref, br_ref, up_ref,
                   h_ref, xn_ref, mi_ref, mf_ref, cnt_ref, run_ref):
    i = pl.program_id(0)
    tiles_per_seq = SEQ // MIX_TILE

    @pl.when(i == 0)
    def _():
        run_ref[...] = jnp.zeros_like(run_ref)

    seq_tile = i % tiles_per_seq
    x = x_ref[...]
    g = gmix_ref[...]
    hn = _rms(x, g)
    halo = jnp.where(seq_tile == 0, 0.0, _rms(halo_ref[...], g))
    full = jnp.concatenate([halo, hn], axis=0)
    t = (seq_tile * MIX_TILE + lax.broadcasted_iota(jnp.int32, (MIX_TILE, 1), 0) + 1).astype(F32)
    outs = []
    for gi, w in enumerate(POOL_WINDOWS):
        acc = full[:, gi * POOL_GROUP:(gi + 1) * POOL_GROUP]
        shift = 1
        while shift < w:
            acc = acc + pltpu.roll(acc, shift, axis=0)
            shift *= 2
        win = acc[POOL_HALO:, :]
        mean = win / jnp.minimum(t, float(w))
        dlt = (mean - hn[:, gi * POOL_GROUP:(gi + 1) * POOL_GROUP]).astype(BF16)
        outs.append(jnp.dot(dlt, poolw_ref[gi], preferred_element_type=F32))
    y = jnp.concatenate(outs, axis=1)
    h = x + y * pscale_ref[...]
    h_ref[...] = h
    xn = _rms(h, gffn_ref[...])
    _store_slab(xn_ref, _pack_rows(xn))
    _route_and_rank(xn, wr_ref, br_ref, up_ref, run_ref, mi_ref, mf_ref, cnt_ref)


def _mixer0(x, g_mix, pool_w, pool_scale, g_ffn, w_r, b_r, upper):
    n_tiles = N_TOK // MIX_TILE
    halo_per_tile = MIX_TILE // POOL_HALO
    vec = lambda: pl.BlockSpec((1, D_MODEL), lambda i: (0, 0))
    return pl.pallas_call(
        _mixer0_kernel,
        grid=(n_tiles,),
        in_specs=[
            pl.BlockSpec((MIX_TILE, D_MODEL), lambda i: (i, 0)),
            pl.BlockSpec((POOL_HALO, D_MODEL), lambda i: (jnp.maximum(i * halo_per_tile - 1, 0), 0)),
            vec(),
            pl.BlockSpec((len(POOL_WINDOWS), POOL_GROUP, POOL_GROUP), lambda i: (0, 0, 0)),
            vec(), vec(),
            pl.BlockSpec((D_MODEL, 2 * LANES), lambda i: (0, 0)),
            pl.BlockSpec((1, LANES), lambda i: (0, 0)),
            pl.BlockSpec((MIX_TILE, MIX_TILE), lambda i: (0, 0)),
        ],
        out_specs=[
            pl.BlockSpec((MIX_TILE, D_MODEL), lambda i: (i, 0)),
            pl.BlockSpec((MIX_TILE * SLAB_ROWS, LANES), lambda i: (i, 0)),
            pl.BlockSpec((SUBLANES, MIX_TILE), lambda i: (0, i)),
            pl.BlockSpec((MIX_TILE, LANES), lambda i: (i, 0)),
            pl.BlockSpec((LANES, LANES), lambda i: (0, 0)),
        ],
        out_shape=[
            jax.ShapeDtypeStruct((N_TOK, D_MODEL), F32),
            jax.ShapeDtypeStruct((N_TOK * SLAB_ROWS, LANES), U32),
            jax.ShapeDtypeStruct((SUBLANES, N_TOK), jnp.int32),
            jax.ShapeDtypeStruct((N_TOK, LANES), F32),
            jax.ShapeDtypeStruct((LANES, LANES), F32),
        ],
        scratch_shapes=[pltpu.VMEM((LANES, 1), F32)],
        compiler_params=pltpu.CompilerParams(dimension_semantics=("arbitrary",), vmem_limit_bytes=48 << 20),
        name="mixer0",
    )(x, x, g_mix, pool_w, pool_scale, g_ffn, w_r, b_r, upper)


def _row_copy(src, src_row, dst, dst_row, sem):
    return pltpu.make_async_copy(
        src.at[pl.ds(pl.multiple_of(src_row * SLAB_ROWS, SLAB_ROWS), SLAB_ROWS), :],
        dst.at[pl.ds(pl.multiple_of(dst_row * SLAB_ROWS, SLAB_ROWS), SLAB_ROWS), :],
        sem)


def _dispatch_kernel(pos0_ref, pos1_ref, src_ref, dst_ref, sem):
    unroll = 8

    def issue(jo, carry):
        for u in range(unroll):
            j = jo * unroll + u
            _row_copy(src_ref, j, dst_ref, pos0_ref[j], sem).start(priority=0)
            _row_copy(src_ref, j, dst_ref, pos1_ref[j], sem).start(priority=1)
        return carry

    lax.fori_loop(0, DISPATCH_TILE // unroll, issue, 0)

    def drain(jo, carry):
        for u in range(2 * unroll):
            _row_copy(src_ref, 0, dst_ref, 0, sem).wait()
        return carry

    lax.fori_loop(0, DISPATCH_TILE // unroll, drain, 0)


def _dispatch(pos0, pos1, xn_slab):
    smem = lambda: pl.BlockSpec((DISPATCH_TILE,), lambda i: (i,), memory_space=pltpu.SMEM)
    return pl.pallas_call(
        _dispatch_kernel,
        grid=(N_TOK // DISPATCH_TILE,),
        in_specs=[smem(), smem(), pl.BlockSpec((DISPATCH_TILE * SLAB_ROWS, LANES), lambda i: (i, 0))],
        out_specs=pl.BlockSpec(memory_space=pl.ANY),
        out_shape=jax.ShapeDtypeStruct((N_ASSIGN * SLAB_ROWS, LANES), U32),
        scratch_shapes=[pltpu.SemaphoreType.DMA(())],
        compiler_params=pltpu.CompilerParams(dimension_semantics=("arbitrary",), has_side_effects=True),
        name="dispatch",
    )(pos0, pos1, xn_slab)


X_SLOTS = 3
Y_SLOTS = 2


def _experts_kernel(layer, vt_ref, ve_ref, gs_ref, ge_ref, nv_ref, nx_ref, xs_hbm, wg_hbm, wu_hbm, wd_hbm, ys_hbm,
                    xb, yb, wg_st, wu_st, wd_st, wgu_s, wd_s, xsem, ysem, wsem):
    rows = EXPERT_TILE * SLAB_ROWS
    n_visits = nv_ref[0]

    def x_copy(t):
        return pltpu.make_async_copy(xs_hbm.at[pl.ds(pl.multiple_of(t * rows, rows), rows), :],
                                     xb.at[t % X_SLOTS], xsem.at[t % X_SLOTS])

    def y_copy(t):
        return pltpu.make_async_copy(yb.at[t % Y_SLOTS],
                                     ys_hbm.at[pl.ds(pl.multiple_of(t * rows, rows), rows), :], ysem.at[t % Y_SLOTS])

    def w_copies(ex):
        return (pltpu.make_async_copy(wg_hbm.at[layer, ex], wg_st, wsem.at[0]),
                pltpu.make_async_copy(wu_hbm.at[layer, ex], wu_st, wsem.at[1]),
                pltpu.make_async_copy(wd_hbm.at[layer, ex], wd_st, wsem.at[2]))

    x_copy(0).start()
    x_copy(1).start()
    for c in w_copies(ve_ref[0]):
        c.start()

    def visit(i, carry):
        prev = jnp.maximum(i - 1, 0)
        e = ve_ref[i]
        tile = vt_ref[i]
        new_expert = (i == 0) | (e != ve_ref[prev])
        new_tile = (i == 0) | (tile != vt_ref[prev])

        @pl.when(new_tile)
        def _():
            x_copy(tile).wait()

            @pl.when(tile + 2 < N_ROW_TILES)
            def _():
                x_copy(tile + 2).start()

            @pl.when(tile >= 2)
            def _():
                y_copy(tile - 2).wait()

            @pl.when(tile >= 1)
            def _():
                y_copy(tile - 1).start()

        @pl.when(new_expert)
        def _():
            for c in w_copies(e):
                c.wait()
            wgu_s[:, :D_EXPERT] = wg_st[...].astype(BF16)
            wgu_s[:, D_EXPERT:] = wu_st[...].astype(BF16)
            wd_s[...] = wd_st[...].astype(BF16)
            nxt = nx_ref[e]

            @pl.when(nxt != e)
            def _():
                for c in w_copies(nxt):
                    c.start()

        xs = tile % X_SLOTS
        ys = tile % Y_SLOTS
        x = _unpack_rows(jnp.concatenate(
            [xb[xs, pl.ds(c, EXPERT_TILE, stride=SLAB_ROWS), :] for c in range(SLAB_ROWS)], axis=1))
        gu = jnp.dot(x.astype(BF16), wgu_s[...], preferred_element_type=F32)
        gate = gu[:, :D_EXPERT]
        act = (gate / (1.0 + jnp.exp(-gate)) * gu[:, D_EXPERT:]).astype(BF16)
        y = _pack_rows(jnp.dot(act, wd_s[...], preferred_element_type=F32))
        r = tile * EXPERT_TILE + lax.broadcasted_iota(jnp.int32, (EXPERT_TILE, 1), 0)
        mine = (r >= gs_ref[e]) & (r < ge_ref[e])

        @pl.when(new_tile)
        def _():
            for c in range(SLAB_ROWS):
                yb[ys, pl.ds(c, EXPERT_TILE, stride=SLAB_ROWS), :] = jnp.where(
                    mine, y[:, c * LANES:(c + 1) * LANES], jnp.uint32(0))

        @pl.when(jnp.logical_not(new_tile))
        def _():
            for c in range(SLAB_ROWS):
                old = yb[ys, pl.ds(c, EXPERT_TILE, stride=SLAB_ROWS), :]
                yb[ys, pl.ds(c, EXPERT_TILE, stride=SLAB_ROWS), :] = jnp.where(mine, y[:, c * LANES:(c + 1) * LANES], old)

        return carry

    lax.fori_loop(0, n_visits, visit, 0)
    last_tile = vt_ref[n_visits - 1]
    y_copy(last_tile).start()
    y_copy(last_tile).wait()

    @pl.when(last_tile >= 1)
    def _():
        y_copy(last_tile - 1).wait()


def _experts(layer, visit_tile, visit_expert, g_start, g_end, n_visits, next_expert, xs_slab, w_gate, w_up, w_down):
    rows = EXPERT_TILE * SLAB_ROWS
    any_spec = lambda: pl.BlockSpec(memory_space=pl.ANY)
    grid_spec = pltpu.PrefetchScalarGridSpec(
        num_scalar_prefetch=6,
        grid=(1,),
        in_specs=[any_spec(), any_spec(), any_spec(), any_spec()],
        out_specs=any_spec(),
        scratch_shapes=[pltpu.VMEM((X_SLOTS, rows, LANES), U32), pltpu.VMEM((Y_SLOTS, rows, LANES), U32),
                        pltpu.VMEM((D_MODEL, D_EXPERT), F32), pltpu.VMEM((D_MODEL, D_EXPERT), F32),
                        pltpu.VMEM((D_EXPERT, D_MODEL), F32),
                        pltpu.VMEM((D_MODEL, 2 * D_EXPERT), BF16), pltpu.VMEM((D_EXPERT, D_MODEL), BF16),
                        pltpu.SemaphoreType.DMA((X_SLOTS,)), pltpu.SemaphoreType.DMA((Y_SLOTS,)),
                        pltpu.SemaphoreType.DMA((3,))],
    )
    return pl.pallas_call(
        functools.partial(_experts_kernel, layer),
        grid_spec=grid_spec,
        out_shape=jax.ShapeDtypeStruct((N_ASSIGN * SLAB_ROWS, LANES), U32),
        compiler_params=pltpu.CompilerParams(dimension_semantics=("arbitrary",), vmem_limit_bytes=48 << 20,
                                             has_side_effects=True),
        name="experts",
    )(visit_tile, visit_expert, g_start, g_end, n_visits, next_expert, xs_slab, w_gate, w_up, w_down)


def _with_expert_rows(pos_refs, ys_ref, bufs, sem, body):
    i = pl.program_id(0)
    n_slots = len(bufs)

    def copies(refs, s):
        for j in range(TOK_TILE):
            yield _row_copy(ys_ref, refs[0][j], bufs[s][0], j, sem.at[s])
            yield _row_copy(ys_ref, refs[1][j], bufs[s][1], j, sem.at[s])

    def start(refs, s):
        for n, c in enumerate(copies(refs, s)):
            c.start(priority=n % 2)

    def wait(s):
        for c in copies(pos_refs[0], s):
            c.wait()

    @pl.when(i == 0)
    def _():
        start(pos_refs[0], 0)
        start(pos_refs[1], 1)

    for s in range(n_slots):
        @pl.when(i % n_slots == s)
        def _():
            wait(s)
            start(pos_refs[2], (s + 2) % n_slots)
            body(_unpack_rows(_load_slab(bufs[s][0], TOK_TILE)), _unpack_rows(_load_slab(bufs[s][1], TOK_TILE)))

            @pl.when(i == pl.num_programs(0) - 1)
            def _():
                wait((s + 1) % n_slots)
                wait((s + 2) % n_slots)


def _moe_residual_and_ple(h_ref, mf_ref, y0, y1, p_ref, gple_ref, wpg_ref, wpu_ref):
    gates = mf_ref[...]
    h = h_ref[...] + (y0 * gates[:, 0:1] + y1 * gates[:, 1:2])
    hn = _rms(h, gple_ref[...]).astype(BF16)
    gate = jnp.dot(hn, wpg_ref[...], preferred_element_type=F32)
    gate = 1.0 / (1.0 + jnp.exp(-gate))
    up = jnp.dot(p_ref[...].astype(BF16), wpu_ref[...], preferred_element_type=F32)
    return h + gate * up


def _rope(x, cos, sin_signed):
    n, width = x.shape
    reps = width // LANES
    cos_t = jnp.concatenate([cos] * reps, axis=1)
    sin_t = jnp.concatenate([sin_signed] * reps, axis=1)
    lane = lax.broadcasted_iota(jnp.int32, (n, width), 1)
    first_half = (lane % HEAD_DIM) < (HEAD_DIM // 2)
    partner = jnp.where(first_half, pltpu.roll(x, width - HEAD_DIM // 2, axis=1), pltpu.roll(x, HEAD_DIM // 2, axis=1))
    return x * cos_t + partner * sin_t


def _combine0_kernel(p0a, p1a, p0b, p1b, p0c, p1c, h_ref, mf_ref, p_ref, cos_ref, sin_ref, gple_ref, wpg_ref, wpu_ref,
                     gkv_ref, wk_ref, wv_ref, gq_ref, wq_ref, ys_ref,
                     ho_ref, k_ref, v_ref, q_ref, b00, b01, b10, b11, b20, b21, sem):
    def body(y0, y1):
        h = _moe_residual_and_ple(h_ref, mf_ref, y0, y1, p_ref, gple_ref, wpg_ref, wpu_ref)
        ho_ref[...] = h
        cos = cos_ref[...]
        sin = sin_ref[...]
        hs = h * lax.rsqrt(jnp.mean(h * h, axis=-1, keepdims=True) + EPS)
        kvn = (hs * gkv_ref[...]).astype(BF16)
        k = jnp.dot(kvn, wk_ref[...], preferred_element_type=F32)
        k_ref[...] = _rope(k, cos, sin).astype(BF16)
        v_ref[...] = jnp.dot(kvn, wv_ref[...], preferred_element_type=F32).astype(BF16)
        qn = (hs * gq_ref[...]).astype(BF16)
        q = jnp.dot(qn, wq_ref[...], preferred_element_type=F32)
        q = (_rope(q, cos, sin) * (HEAD_DIM ** -0.5)).astype(BF16)
        for g in range(Q_PER_KV):
            q_ref[g] = q[:, g * KV_DIM:(g + 1) * KV_DIM]

    _with_expert_rows(((p0a, p1a), (p0b, p1b), (p0c, p1c)), ys_ref, ((b00, b01), (b10, b11), (b20, b21)), sem, body)


def _combine1_kernel(p0a, p1a, p0b, p1b, p0c, p1c, h_ref, mf_ref, p_ref, gple_ref, wpg_ref, wpu_ref, gfin_ref, ys_ref,
                     out_ref, b00, b01, b10, b11, b20, b21, sem):
    def body(y0, y1):
        h = _moe_residual_and_ple(h_ref, mf_ref, y0, y1, p_ref, gple_ref, wpg_ref, wpu_ref)
        out_ref[...] = _rms(h, gfin_ref[...])

    _with_expert_rows(((p0a, p1a), (p0b, p1b), (p0c, p1c)), ys_ref, ((b00, b01), (b10, b11), (b20, b21)), sem, body)


def _combine_specs():
    last = N_TOK // TOK_TILE - 1
    smem = lambda: pl.BlockSpec((TOK_TILE,), lambda i: (i,), memory_space=pltpu.SMEM)
    smem_next = lambda d: pl.BlockSpec((TOK_TILE,), lambda i: (jnp.minimum(i + d, last),), memory_space=pltpu.SMEM)
    tok = lambda w: pl.BlockSpec((TOK_TILE, w), lambda i: (i, 0))
    full = lambda a, b: pl.BlockSpec((a, b), lambda i: (0, 0))
    scratch = [pltpu.VMEM((TOK_TILE * SLAB_ROWS, LANES), U32) for _ in range(6)] + [pltpu.SemaphoreType.DMA((3,))]
    ple = lambda layer: pl.BlockSpec((None, TOK_TILE, PLE_DIM), lambda i: (layer, i, 0))
    return smem, smem_next, tok, ple, full, scratch


def _combine0(pos0, pos1, h, mf, p, cos_t, sin_t, g_ple, w_pg, w_pu, g_kv, w_k, w_v, g_q, w_q, ys_slab):
    smem, smem_next, tok, ple, full, scratch = _combine_specs()
    return pl.pallas_call(
        _combine0_kernel,
        grid=(N_TOK // TOK_TILE,),
        in_specs=[smem(), smem(), smem_next(1), smem_next(1), smem_next(2), smem_next(2), tok(D_MODEL), tok(LANES), ple(0), tok(LANES), tok(LANES),
                  full(1, D_MODEL), full(D_MODEL, D_MODEL), full(PLE_DIM, D_MODEL),
                  full(1, D_MODEL), full(D_MODEL, KV_DIM), full(D_MODEL, KV_DIM),
                  full(1, D_MODEL), full(D_MODEL, D_MODEL),
                  pl.BlockSpec(memory_space=pl.ANY)],
        out_specs=[tok(D_MODEL), tok(KV_DIM), tok(KV_DIM),
                   pl.BlockSpec((Q_PER_KV, TOK_TILE, KV_DIM), lambda i: (0, i, 0))],
        out_shape=[jax.ShapeDtypeStruct((N_TOK, D_MODEL), F32),
                   jax.ShapeDtypeStruct((N_TOK, KV_DIM), BF16),
                   jax.ShapeDtypeStruct((N_TOK, KV_DIM), BF16),
                   jax.ShapeDtypeStruct((Q_PER_KV, N_TOK, KV_DIM), BF16)],
        scratch_shapes=scratch,
        compiler_params=pltpu.CompilerParams(dimension_semantics=("arbitrary",), vmem_limit_bytes=56 << 20),
        name="combine0",
    )(pos0, pos1, pos0, pos1, pos0, pos1, h, mf, p, cos_t, sin_t, g_ple, w_pg, w_pu, g_kv, w_k, w_v, g_q, w_q, ys_slab)


def _combine1(pos0, pos1, h, mf, p, g_ple, w_pg, w_pu, g_fin, ys_slab):
    smem, smem_next, tok, ple, full, scratch = _combine_specs()
    return pl.pallas_call(
        _combine1_kernel,
        grid=(N_TOK // TOK_TILE,),
        in_specs=[smem(), smem(), smem_next(1), smem_next(1), smem_next(2), smem_next(2), tok(D_MODEL), tok(LANES), ple(1),
                  full(1, D_MODEL), full(D_MODEL, D_MODEL), full(PLE_DIM, D_MODEL), full(1, D_MODEL),
                  pl.BlockSpec(memory_space=pl.ANY)],
        out_specs=tok(D_MODEL),
        out_shape=jax.ShapeDtypeStruct((N_TOK, D_MODEL), F32),
        scratch_shapes=scratch,
        compiler_params=pltpu.CompilerParams(dimension_semantics=("arbitrary",), vmem_limit_bytes=48 << 20),
        name="combine1",
    )(pos0, pos1, pos0, pos1, pos0, pos1, h, mf, p, g_ple, w_pg, w_pu, g_fin, ys_slab)


def _attn1_kernel(sink_ref, q_ref, kc_ref, kp_ref, vc_ref, vp_ref, h_ref, wo_ref, gffn_ref, wr_ref, br_ref, up_ref,
                  ho_ref, xn_ref, mi_ref, mf_ref, cnt_ref, run_ref, k_s, v_s, o_s):
    i = pl.program_id(0)
    steps_per_seq = SEQ // ATT_TILE

    @pl.when(i == 0)
    def _():
        run_ref[...] = jnp.zeros_like(run_ref)

    k_s[:ATT_BLOCK, :] = kp_ref[...]
    k_s[ATT_BLOCK:, :] = kc_ref[...]
    v_s[:ATT_BLOCK, :] = vp_ref[...]
    v_s[ATT_BLOCK:, :] = vc_ref[...]
    seq_start = (i % steps_per_seq) == 0

    n_rows = Q_PER_KV * ATT_BLOCK
    assert WINDOW == ATT_BLOCK
    row = lax.broadcasted_iota(jnp.int32, (n_rows, ATT_BLOCK), 0)
    col = lax.broadcasted_iota(jnp.int32, (n_rows, ATT_BLOCK), 1)
    upper = col > (row % ATT_BLOCK)
    head_of_lane = lax.broadcasted_iota(jnp.int32, (2 * ATT_BLOCK, KV_DIM), 1) // HEAD_DIM
    grp_of_row = lax.broadcasted_iota(jnp.int32, (n_rows, 1), 0) // ATT_BLOCK

    def block(b, carry):
        r0 = b * ATT_BLOCK
        has_prev = jnp.logical_not(seq_start) if b == 0 else True
        kcat = k_s[pl.ds(r0, 2 * ATT_BLOCK), :]
        vcat = v_s[pl.ds(r0, 2 * ATT_BLOCK), :]
        qs = jnp.concatenate([q_ref[g, pl.ds(r0, ATT_BLOCK), :] for g in range(Q_PER_KV)], axis=0)
        o = jnp.zeros((n_rows, KV_DIM), F32)
        for kh in range(N_KV_HEADS):
            k_h = jnp.where(head_of_lane == kh, kcat, jnp.zeros_like(kcat))
            v_h = jnp.where(head_of_lane == kh, vcat, jnp.zeros_like(vcat))
            s = lax.dot_general(qs, k_h, (((1,), (1,)), ((), ())), preferred_element_type=F32)
            s = jnp.where(upper, jnp.where(has_prev, s[:, :ATT_BLOCK], NEG_INF), s[:, ATT_BLOCK:])
            sink = jnp.zeros((n_rows, 1), F32)
            for g in range(Q_PER_KV):
                sink = jnp.where(grp_of_row == g, sink_ref[kh * Q_PER_KV + g], sink)
            m = jnp.maximum(jnp.max(s, axis=1, keepdims=True), sink)
            e = jnp.exp(s - m)
            denom = jnp.sum(e, axis=1, keepdims=True) + jnp.exp(sink - m)
            pr = e / denom
            pr = jnp.concatenate([jnp.where(upper, pr, 0.0), jnp.where(upper, 0.0, pr)], axis=1).astype(BF16)
            o = o + jnp.dot(pr, v_h, preferred_element_type=F32)
        for g in range(Q_PER_KV):
            o_s[g, pl.ds(r0, ATT_BLOCK), :] = o[g * ATT_BLOCK:(g + 1) * ATT_BLOCK, :].astype(BF16)
        return carry

    for b in range(ATT_TILE // ATT_BLOCK):
        block(b, 0)
    o_cat = jnp.concatenate([o_s[g] for g in range(Q_PER_KV)], axis=1)
    h = h_ref[...] + jnp.dot(o_cat, wo_ref[...], preferred_element_type=F32)
    ho_ref[...] = h
    xn = _rms(h, gffn_ref[...])
    _store_slab(xn_ref, _pack_rows(xn))
    _route_and_rank(xn, wr_ref, br_ref, up_ref, run_ref, mi_ref, mf_ref, cnt_ref)


def _attn1(sinks, q, k, v, h, w_o, g_ffn, w_r, b_r, upper):
    blocks_per_step = ATT_TILE // ATT_BLOCK
    cur = lambda w: pl.BlockSpec((ATT_TILE, w), lambda i: (i, 0))
    prev = lambda w: pl.BlockSpec((ATT_BLOCK, w), lambda i: (jnp.maximum(i * blocks_per_step - 1, 0), 0))
    full = lambda a, b: pl.BlockSpec((a, b), lambda i: (0, 0))
    return pl.pallas_call(
        _attn1_kernel,
        grid=(N_TOK // ATT_TILE,),
        in_specs=[pl.BlockSpec(memory_space=pltpu.SMEM),
                  pl.BlockSpec((Q_PER_KV, ATT_TILE, KV_DIM), lambda i: (0, i, 0)),
                  cur(KV_DIM), prev(KV_DIM), cur(KV_DIM), prev(KV_DIM), cur(D_MODEL),
                  full(D_MODEL, D_MODEL), full(1, D_MODEL), full(D_MODEL, 2 * LANES), full(1, LANES),
                  full(ATT_TILE, ATT_TILE)],
        out_specs=[cur(D_MODEL),
                   pl.BlockSpec((ATT_TILE * SLAB_ROWS, LANES), lambda i: (i, 0)),
                   pl.BlockSpec((SUBLANES, ATT_TILE), lambda i: (0, i)), cur(LANES),
                   pl.BlockSpec((LANES, LANES), lambda i: (0, 0))],
        out_shape=[jax.ShapeDtypeStruct((N_TOK, D_MODEL), F32),
                   jax.ShapeDtypeStruct((N_TOK * SLAB_ROWS, LANES), U32),
                   jax.ShapeDtypeStruct((SUBLANES, N_TOK), jnp.int32),
                   jax.ShapeDtypeStruct((N_TOK, LANES), F32),
                   jax.ShapeDtypeStruct((LANES, LANES), F32)],
        scratch_shapes=[pltpu.VMEM((LANES, 1), F32),
                        pltpu.VMEM((ATT_TILE + ATT_BLOCK, KV_DIM), BF16),
                        pltpu.VMEM((ATT_TILE + ATT_BLOCK, KV_DIM), BF16),
                        pltpu.VMEM((Q_PER_KV, ATT_TILE, KV_DIM), BF16)],
        compiler_params=pltpu.CompilerParams(dimension_semantics=("arbitrary",), vmem_limit_bytes=48 << 20),
        name="attn1",
    )(sinks, q, k, k, v, v, h, w_o, g_ffn, w_r, b_r, upper)


def _plan_kernel(cnt_ref, meta_ref, pos_ref, vt_ref, ve_ref, gs_ref, ge_ref, nv_ref, nx_ref):
    def per_expert(e, carry):
        off, v = carry
        c = cnt_ref[e]
        gs_ref[e] = off
        ge_ref[e] = off + c
        first = off // EXPERT_TILE
        n_vis = jnp.where(c > 0, (off + c - 1) // EXPERT_TILE - first + 1, 0)

        def visit(j, _):
            vt_ref[v + j] = first + j
            ve_ref[v + j] = e
            return 0

        lax.fori_loop(0, n_vis, visit, 0)
        return off + c, v + n_vis

    _, total = lax.fori_loop(0, N_EXPERTS, per_expert, (jnp.int32(0), jnp.int32(0)))
    nv_ref[0] = total

    def pad(j, _):
        vt_ref[j] = vt_ref[total - 1]
        ve_ref[j] = ve_ref[total - 1]
        return 0

    lax.fori_loop(total, N_VISITS, pad, 0)

    def link(k, nxt):
        e = N_EXPERTS - 1 - k
        nx_ref[e] = jnp.where(nxt >= 0, nxt, e)
        return jnp.where(cnt_ref[e] > 0, e, nxt)

    lax.fori_loop(0, N_EXPERTS, link, jnp.int32(-1))

    meta = meta_ref[...]
    start = jnp.zeros_like(meta)
    for e in range(N_EXPERTS):
        start = jnp.where(meta == e, gs_ref[e], start)
    pos_ref[...] = start + pltpu.roll(meta, SUBLANES - 2, axis=0)


def _routing_tables(meta_i, counts_f):
    counts = counts_f[:N_EXPERTS, 0].astype(jnp.int32)
    smem = lambda: pl.BlockSpec(memory_space=pltpu.SMEM)
    i32 = lambda n: jax.ShapeDtypeStruct((n,), jnp.int32)
    pos, vt, ve, gs, ge, nv, nx = pl.pallas_call(
        _plan_kernel,
        in_specs=[smem(), pl.BlockSpec(memory_space=pltpu.VMEM)],
        out_specs=[pl.BlockSpec(memory_space=pltpu.VMEM), smem(), smem(), smem(), smem(), smem(), smem()],
        out_shape=[jax.ShapeDtypeStruct((SUBLANES, N_TOK), jnp.int32),
                   i32(N_VISITS), i32(N_VISITS), i32(N_EXPERTS), i32(N_EXPERTS), i32(1), i32(N_EXPERTS)],
        name="plan",
    )(counts, meta_i)
    return pos[0], pos[1], vt, ve, gs, ge, nv, nx


def _router_weights(w_grp, b_grp, w_exp, b_exp):
    w = jnp.zeros((D_MODEL, LANES), F32)
    w = w.at[:, :N_EXPERTS].set(w_exp).at[:, GROUP_LANE0:GROUP_LANE0 + N_GROUPS].set(w_grp)
    b = jnp.zeros((1, LANES), F32)
    b = b.at[0, :N_EXPERTS].set(b_exp).at[0, GROUP_LANE0:GROUP_LANE0 + N_GROUPS].set(b_grp)
    w_hi = w.astype(BF16)
    w_lo = (w - w_hi.astype(F32)).astype(BF16)
    return jnp.concatenate([w_hi, w_lo], axis=1), b


def _rope_tables(positions):
    inv = ROPE_THETA ** (-jnp.arange(0, HEAD_DIM, 2, dtype=F32) / HEAD_DIM)
    ang = positions.astype(F32).reshape(N_TOK, 1) * inv
    cos = jnp.cos(ang)
    sin = jnp.sin(ang)
    cos_t = jnp.concatenate([cos, cos, cos, cos], axis=1)
    sin_t = jnp.concatenate([-sin, sin, -sin, sin], axis=1)
    return cos_t, sin_t


def _moe(layer, meta_i, counts_f, xn_slab, w_gate, w_up, w_down):
    pos0, pos1, vt, ve, gs, ge, nv, nx = _routing_tables(meta_i, counts_f)
    xs_slab = _dispatch(pos0, pos1, xn_slab)
    ys_slab = _experts(layer, vt, ve, gs, ge, nv, nx, xs_slab, w_gate, w_up, w_down)
    return pos0, pos1, ys_slab


def kernel(x, p, positions, g_mix, g_ffn, pool_w, pool_scale, g_kv, w_k, w_v, w_q, w_o, sinks, w_group_router, b_group_router, w_expert_router, b_expert_router, w_exp_gate, w_exp_up, w_exp_down, g_ple, w_ple_gate, w_ple_up, g_final):
    row = lambda a: a.reshape(1, -1)
    x2 = x.reshape(N_TOK, D_MODEL)
    p2 = p.reshape(2, N_TOK, PLE_DIM)
    cos_t, sin_t = _rope_tables(positions)
    assert MIX_TILE == ATT_TILE
    upper = jnp.triu(jnp.ones((MIX_TILE, MIX_TILE), BF16), k=1)
    w_q_perm = w_q[0].reshape(D_MODEL, N_KV_HEADS, Q_PER_KV, HEAD_DIM).transpose(0, 2, 1, 3).reshape(D_MODEL, D_MODEL)
    w_o_perm = w_o[0].reshape(N_KV_HEADS, Q_PER_KV, HEAD_DIM, D_MODEL).transpose(1, 0, 2, 3).reshape(D_MODEL, D_MODEL)

    wr0, br0 = _router_weights(w_group_router[0], b_group_router[0], w_expert_router[0], b_expert_router[0])
    h1, xn_slab, meta_i, meta_f, counts = _mixer0(
        x2, row(g_mix[0]), pool_w[0].astype(BF16), row(pool_scale[0]), row(g_ffn[0]), wr0, br0, upper)
    pos0, pos1, ys_slab = _moe(0, meta_i, counts, xn_slab, w_exp_gate, w_exp_up, w_exp_down)
    h3, k, v, q = _combine0(
        pos0, pos1, h1, meta_f, p2, cos_t, sin_t, row(g_ple[0]), w_ple_gate[0].astype(BF16),
        w_ple_up[0].astype(BF16), row(g_kv), w_k.astype(BF16), w_v.astype(BF16), row(g_mix[1]),
        w_q_perm.astype(BF16), ys_slab)

    wr1, br1 = _router_weights(w_group_router[1], b_group_router[1], w_expert_router[1], b_expert_router[1])
    h4, xn_slab, meta_i, meta_f, counts = _attn1(
        sinks[0], q, k, v, h3, w_o_perm.astype(BF16), row(g_ffn[1]), wr1, br1, upper)
    pos0, pos1, ys_slab = _moe(1, meta_i, counts, xn_slab, w_exp_gate, w_exp_up, w_exp_down)
    out = _combine1(pos0, pos1, h4, meta_f, p2, row(g_ple[1]), w_ple_gate[1].astype(BF16),
                    w_ple_up[1].astype(BF16), row(g_final), ys_slab)
    return out.reshape(BATCH, SEQ, D_MODEL)
```
